```python
import functools
import jax, jax.numpy as jnp
from jax import lax
import numpy as np

D_MODEL = 1024
BATCH = 2
SEQ = 8192
DEPTH = 1
DEC_BATCH = 8
DEC_SEQ = 64
PAST_LEN = 2048

CHUNK = 64
Q_BLOCK = 128
RET_HEADS = 4
RET_DK = 128
RET_DV = 256
MLA_HEADS = 8
Q_LORA = 512
KV_LORA = 256
NOPE_DIM = 128
ROPE_DIM = 64
V_DIM = 128
D_FF = ((8 * D_MODEL // 3 + 255) // 256) * 256
ROPE_BASE = 10000.0
EPS = 1e-6

RET_QK_W = RET_HEADS * RET_DK
RET_V_W = RET_HEADS * RET_DV
MLA_Q_W = MLA_HEADS * (NOPE_DIM + ROPE_DIM)
MLA_KV_W = MLA_HEADS * (NOPE_DIM + V_DIM)
MLA_V_W = MLA_HEADS * V_DIM
IN_W = 2 * RET_QK_W + 2 * RET_V_W + Q_LORA + KV_LORA + ROPE_DIM + 2 * D_MODEL

kernel_name = "retnet_mla_parallel_gated_stream_step"


def rms_norm(x, g):
    xf = x.astype(jnp.float32)
    y = xf * lax.rsqrt(jnp.mean(xf * xf, axis=-1, keepdims=True) + EPS)
    return (y * g.astype(jnp.float32)).astype(x.dtype)


def rope(x, pos):
    d = x.shape[-1]
    inv = ROPE_BASE ** (-jnp.arange(0, d, 2, dtype=jnp.float32) / d)
    ang = pos.astype(jnp.float32)[:, None] * inv[None, :]
    ang = ang.reshape(ang.shape[:1] + (1,) * (x.ndim - 3) + ang.shape[1:])
    cos, sin = jnp.cos(ang), jnp.sin(ang)
    xf = x.astype(jnp.float32)
    x1, x2 = xf[..., : d // 2], xf[..., d // 2:]
    return jnp.concatenate([x1 * cos - x2 * sin, x1 * sin + x2 * cos], axis=-1).astype(x.dtype)


def split_columns(z):
    sizes = (RET_QK_W, RET_QK_W, RET_V_W, RET_V_W, Q_LORA, KV_LORA, ROPE_DIM, D_MODEL, D_MODEL)
    offs = np.cumsum(sizes)[:-1].tolist()
    return jnp.split(z, offs, axis=-1)


def ret_log_gamma():
    return jnp.log(1.0 - 2.0 ** (-5.0 - jnp.arange(RET_HEADS, dtype=jnp.float32)))


def retention_chunk(state, q, k, v):
    L = q.shape[1]
    lg = ret_log_gamma()
    n = jnp.arange(L, dtype=jnp.float32)
    diff = n[:, None] - n[None, :]
    decay = jnp.where(diff >= 0, jnp.exp(lg[:, None, None] * jnp.maximum(diff, 0.0)), 0.0)
    scores = jnp.einsum('blhd,bmhd->bhlm', q, k) * decay
    inner = jnp.einsum('bhlm,bmhe->blhe', scores, v)
    q_decay = jnp.exp(lg[None, :] * (n[:, None] + 1.0))
    cross = jnp.einsum('blhd,bhde->blhe', q * q_decay[None, :, :, None], state)
    k_decay = jnp.exp(lg[None, :] * (L - 1.0 - n[:, None]))
    new_state = jnp.exp(lg * L)[None, :, None, None] * state + jnp.einsum('blhd,blhe->bhde', k * k_decay[None, :, :, None], v)
    return new_state, inner + cross


def retention_prompt(q, k, v):
    B, S, H, _ = q.shape
    nc = S // CHUNK
    def to_chunks(t):
        return t.astype(jnp.float32).reshape(B, nc, CHUNK, H, t.shape[-1]).swapaxes(0, 1)
    s0 = jnp.zeros((B, H, RET_DK, RET_DV), jnp.float32)
    s_fin, y = lax.scan(lambda s, c: retention_chunk(s, *c), s0, (to_chunks(q), to_chunks(k), to_chunks(v)))
    return y.swapaxes(0, 1).reshape(B, S, H, RET_DV), s_fin


def head_group_norm(y, g):
    mu = jnp.mean(y, axis=-1, keepdims=True)
    yc = y - mu
    yn = yc * lax.rsqrt(jnp.mean(yc * yc, axis=-1, keepdims=True) + EPS)
    B, S = y.shape[:2]
    return yn.reshape(B, S, RET_V_W) * g.astype(jnp.float32)


def mla_decompress(ckv, w_ukv):
    B, T, _ = ckv.shape
    kv = (ckv @ w_ukv).reshape(B, T, MLA_HEADS, NOPE_DIM + V_DIM)
    return kv[..., :NOPE_DIM], kv[..., NOPE_DIM:]


def mla_attend(q, k_nope, kpe, v, mask):
    scale = (NOPE_DIM + ROPE_DIM) ** -0.5
    s = (jnp.einsum('bqhd,bkhd->bhqk', q[..., :NOPE_DIM], k_nope).astype(jnp.float32)
         + jnp.einsum('bqhr,bkr->bhqk', q[..., NOPE_DIM:], kpe).astype(jnp.float32)) * scale
    s = jnp.where(mask, s, jnp.finfo(jnp.float32).min)
    p = jax.nn.softmax(s, axis=-1)
    return jnp.einsum('bhqk,bkhe->bqhe', p.astype(v.dtype), v)


def mla_prompt(q, ckv, kpe, w_ukv):
    B, S = q.shape[:2]
    k_nope, v = mla_decompress(ckv, w_ukv)
    nb = S // Q_BLOCK
    qb = q.reshape(B, nb, Q_BLOCK, MLA_HEADS, NOPE_DIM + ROPE_DIM).swapaxes(0, 1)
    key_chunk = jnp.arange(S) // CHUNK
    def one_block(args):
        qi, i = args
        q_chunk = (i * Q_BLOCK + jnp.arange(Q_BLOCK)) // CHUNK
        mask = key_chunk[None, :] <= q_chunk[:, None]
        return mla_attend(qi, k_nope, kpe, v, mask)
    out = lax.map(one_block, (qb, jnp.arange(nb)))
    return out.swapaxes(0, 1).reshape(B, S, MLA_V_W)


def prompt_core(rq, rk, rv, q, ckv, kpe, w_ukv):
    ret_y, s_fin = retention_prompt(rq, rk, rv)
    mla_y = mla_prompt(q, ckv, kpe, w_ukv)
    return ret_y, mla_y, (ckv, kpe, s_fin.astype(ckv.dtype))


def sample_core(cache_ckv, cache_kpe, state_ret, rq, rk, rv, q, ckv, kpe, w_ukv):
    f32 = jnp.float32
    B, L = q.shape[:2]
    s_new, ret_y = retention_chunk(state_ret.astype(f32), rq.astype(f32), rk.astype(f32), rv.astype(f32))
    keys_ckv = jnp.concatenate([cache_ckv.astype(ckv.dtype), ckv], axis=1)
    keys_kpe = jnp.concatenate([cache_kpe.astype(kpe.dtype), kpe], axis=1)
    k_nope, v = mla_decompress(keys_ckv, w_ukv)
    mask = jnp.ones((L, keys_ckv.shape[1]), dtype=bool)
    mla_y = mla_attend(q, k_nope, keys_kpe, v, mask).reshape(B, L, MLA_V_W)
    return ret_y, mla_y, (ckv, kpe, s_new.astype(state_ret.dtype))


def layer_forward(x, pos, mixer_core, norm_mix_w, w_in, q_norm_w, w_uq, kv_norm_w, w_ukv,
                  ret_gn_w, w_ret_o, w_mla_o, w_out, norm_ffn_w, w_gate_up, w_down):
    B, S, _ = x.shape
    xn = rms_norm(x, norm_mix_w)
    rq, rk, rv, rg, cq, ckv, kpe, ga, gb = split_columns(xn @ w_in)
    rq = rope(rq.reshape(B, S, RET_HEADS, RET_DK), pos)
    rk = rope(rk.reshape(B, S, RET_HEADS, RET_DK), pos) * (RET_DK ** -0.5)
    rv = rv.reshape(B, S, RET_HEADS, RET_DV)
    q = (rms_norm(cq, q_norm_w) @ w_uq).reshape(B, S, MLA_HEADS, NOPE_DIM + ROPE_DIM)
    q = jnp.concatenate([q[..., :NOPE_DIM], rope(q[..., NOPE_DIM:], pos)], axis=-1)
    ckv = rms_norm(ckv, kv_norm_w)
    kpe = rope(kpe, pos)
    ret_y, mla_y, new_state = mixer_core(rq, rk, rv, q, ckv, kpe, w_ukv)
    ret_branch = (head_group_norm(ret_y, ret_gn_w).astype(x.dtype) * jax.nn.silu(rg)) @ w_ret_o
    mla_branch = mla_y @ w_mla_o
    merged = jax.nn.sigmoid(ga) * ret_branch + jax.nn.sigmoid(gb) * mla_branch
    h = x + merged @ w_out
    g, u = jnp.split(rms_norm(h, norm_ffn_w) @ w_gate_up, 2, axis=-1)
    h = h + (jax.nn.silu(g) * u) @ w_down
    return h, new_state


def setup_inputs(seed: int = 0) -> dict:
    key = jax.random.key(seed)
    ks = jax.random.split(key, 24)
    f32 = jnp.float32
    def w(k, shape, fan_in):
        return jax.random.normal(k, shape, f32) * fan_in ** -0.5
    def gain(k, shape):
        return 1.0 + 0.02 * jax.random.normal(k, shape, f32)
    return {
        "x_prompt": jax.random.normal(ks[0], (BATCH, SEQ, D_MODEL), f32),
        "x_sample": jax.random.normal(ks[1], (DEC_BATCH, DEC_SEQ, D_MODEL), f32),
        "cache_ckv": jax.random.normal(ks[2], (DEPTH, DEC_BATCH, PAST_LEN, KV_LORA), f32),
        "cache_kpe": jax.random.normal(ks[3], (DEPTH, DEC_BATCH, PAST_LEN, ROPE_DIM), f32),
        "state_ret": 0.5 * jax.random.normal(ks[4], (DEPTH, DEC_BATCH, RET_HEADS, RET_DK, RET_DV), f32),
        "norm_mix_w": gain(ks[5], (DEPTH, D_MODEL)),
        "w_in": w(ks[6], (DEPTH, D_MODEL, IN_W), D_MODEL),
        "q_norm_w": gain(ks[7], (DEPTH, Q_LORA)),
        "w_uq": w(ks[8], (DEPTH, Q_LORA, MLA_Q_W), Q_LORA),
        "kv_norm_w": gain(ks[9], (DEPTH, KV_LORA)),
        "w_ukv": w(ks[10], (DEPTH, KV_LORA, MLA_KV_W), KV_LORA),
        "ret_gn_w": gain(ks[11], (DEPTH, RET_V_W)),
        "w_ret_o": w(ks[12], (DEPTH, RET_V_W, D_MODEL), RET_V_W),
        "w_mla_o": w(ks[13], (DEPTH, MLA_V_W, D_MODEL), MLA_V_W),
        "w_out": w(ks[14], (DEPTH, D_MODEL, D_MODEL), D_MODEL),
        "norm_ffn_w": gain(ks[15], (DEPTH, D_MODEL)),
        "w_gate_up": w(ks[16], (DEPTH, D_MODEL, 2 * D_FF), D_MODEL),
        "w_down": w(ks[17], (DEPTH, D_FF, D_MODEL), D_FF),
        "norm_final_w": gain(ks[18], (D_MODEL,)),
    }


def reference(x_prompt, x_sample, cache_ckv, cache_kpe, state_ret, norm_mix_w, w_in, q_norm_w, w_uq,
              kv_norm_w, w_ukv, ret_gn_w, w_ret_o, w_mla_o, w_out, norm_ffn_w, w_gate_up, w_down, norm_final_w):
    past = cache_ckv.shape[2]
    pos_p = jnp.arange(x_prompt.shape[1])
    pos_s = past + jnp.arange(x_sample.shape[1])
    hp, hs = x_prompt, x_sample
    p_ckv, p_kpe, p_ret, s_ckv, s_kpe, s_ret = [], [], [], [], [], []
    for l in range(DEPTH):
        lw = (norm_mix_w[l], w_in[l], q_norm_w[l], w_uq[l], kv_norm_w[l], w_ukv[l], ret_gn_w[l],
              w_ret_o[l], w_mla_o[l], w_out[l], norm_ffn_w[l], w_gate_up[l], w_down[l])
        hp, (a, b, c) = layer_forward(hp, pos_p, prompt_core, *lw)
        core = functools.partial(sample_core, cache_ckv[l], cache_kpe[l], state_ret[l])
        hs, (d, e, f) = layer_forward(hs, pos_s, core, *lw)
        p_ckv.append(a); p_kpe.append(b); p_ret.append(c)
        s_ckv.append(d); s_kpe.append(e); s_ret.append(f)
    y_prompt = rms_norm(hp, norm_final_w)
    y_sample = rms_norm(hs, norm_final_w)
    return (y_prompt, y_sample, jnp.stack(p_ckv), jnp.stack(p_kpe), jnp.stack(p_ret),
            jnp.stack(s_ckv), jnp.stack(s_kpe), jnp.stack(s_ret))
```

```python
import functools
import math

import jax
import jax.numpy as jnp
from jax import lax
from jax.experimental import pallas as pl
from jax.experimental.pallas import tpu as pltpu

F32 = jnp.float32
BF16 = jnp.bfloat16

CHUNK = 64
RET_HEADS = 4
RET_DK = 128
RET_DV = 256
MLA_HEADS = 8
NOPE_DIM = 128
ROPE_DIM = 64
V_DIM = 128
ROPE_BASE = 10000.0
EPS = 1e-6

LANES = 128
QK_PAD = 2 * LANES
V7X_VMEM_BYTES = 64 * 2**20
VMEM_LIMIT = V7X_VMEM_BYTES * 7 // 8
TOKEN_TILE = 512
RET_BLOCK = 256
ATTN_BLOCK = 512
FF_CHUNK = 256
NEG_BIG = -1e30

_dot = functools.partial(jnp.dot, preferred_element_type=F32)


def _dot_nt(a, b):
    return lax.dot_general(a, b, (((1,), (1,)), ((), ())), preferred_element_type=F32)


def _dot_tn(a, b):
    return lax.dot_general(a, b, (((0,), (0,)), ((), ())), preferred_element_type=F32)


def _rms(x, g):
    ms = jnp.mean(x * x, axis=-1, keepdims=True)
    return x * lax.rsqrt(ms + EPS) * g


def _sigmoid(x):
    return 1.0 / (1.0 + jnp.exp(-x))


def _silu(x):
    return x * _sigmoid(x)


def _params(*sem):
    return pltpu.CompilerParams(dimension_semantics=sem, vmem_limit_bytes=VMEM_LIMIT)


def _const_spec(shape):
    return pl.BlockSpec(shape, lambda *_: (0,) * len(shape), pipeline_mode=pl.Buffered(1))


def _row_spec(tm, width):
    return pl.BlockSpec((tm, width), lambda i: (i, 0))


def _rope_full(x, cos2, sin2):
    return x * cos2 + pltpu.roll(x, LANES // 2, 1) * sin2


def _rope_half(x, c, sa, sb):
    q = ROPE_DIM // 2
    return x * c + pltpu.roll(x, LANES - q, 1) * sa + pltpu.roll(x, q, 1) * sb


def _inproj_kernel(x_ref, nw_ref, wa_ref, qnw_ref, kvnw_ref, wuq_ref,
                   cr_ref, sr_ref, cm_ref, sa_ref, sb_ref,
                   rq_ref, rk_ref, rv_ref, q_ref, ckv_ref, kpe_ref):
    qk_w = RET_HEADS * RET_DK
    v_w = RET_HEADS * RET_DV
    q_lora = wuq_ref.shape[0]
    kv_lora = ckv_ref.shape[1]
    o_rk, o_rv, o_cq = qk_w, 2 * qk_w, 2 * qk_w + v_w
    o_ckv = o_cq + q_lora
    o_kpe = o_ckv + kv_lora

    xn = _rms(x_ref[...], nw_ref[...]).astype(BF16)
    cr, sr = cr_ref[...], sr_ref[...]
    cm, sa, sb = cm_ref[...], sa_ref[...], sb_ref[...]

    zq = _dot(xn, wa_ref[:, 0:qk_w])
    zk = _dot(xn, wa_ref[:, o_rk:o_rk + qk_w])
    k_scale = RET_DK ** -0.5
    for h in range(RET_HEADS):
        sl = slice(h * RET_DK, (h + 1) * RET_DK)
        rq_ref[:, sl] = _rope_full(zq[:, sl], cr, sr).astype(BF16)
        rk_ref[:, sl] = (_rope_full(zk[:, sl], cr, sr) * k_scale).astype(BF16)
    rv_ref[...] = _dot(xn, wa_ref[:, o_rv:o_rv + v_w]).astype(BF16)

    cq = _dot(xn, wa_ref[:, o_cq:o_cq + q_lora])
    cqn = _rms(cq, qnw_ref[...]).astype(BF16)
    for h in range(MLA_HEADS):
        qh = _dot(cqn, wuq_ref[:, h * QK_PAD:(h + 1) * QK_PAD])
        q_ref[:, h * QK_PAD:h * QK_PAD + LANES] = qh[:, :LANES].astype(BF16)
        q_ref[:, h * QK_PAD + LANES:(h + 1) * QK_PAD] = _rope_half(qh[:, LANES:], cm, sa, sb).astype(BF16)

    ckv = _dot(xn, wa_ref[:, o_ckv:o_ckv + kv_lora])
    ckv_ref[...] = _rms(ckv, kvnw_ref[...])
    kp = _dot(xn, wa_ref[:, o_kpe:o_kpe + LANES])
    kpe_ref[...] = _rope_half(kp, cm, sa, sb)[:, :ROPE_DIM]


def _inproj(x2, tabs, nw, wa, qnw, kvnw, wuq, *, tab_tiles):
    n, d = x2.shape
    tm = min(TOKEN_TILE, n)
    qk_w, v_w = RET_HEADS * RET_DK, RET_HEADS * RET_DV
    kv_lora = kvnw.shape[1]
    tab_spec = pl.BlockSpec((tm, LANES), lambda i: (i % tab_tiles, 0))
    return pl.pallas_call(
        _inproj_kernel,
        grid=(n // tm,),
        in_specs=[_row_spec(tm, d), _const_spec(nw.shape), _const_spec(wa.shape), _const_spec(qnw.shape),
                  _const_spec(kvnw.shape), _const_spec(wuq.shape)] + [tab_spec] * 5,
        out_specs=[_row_spec(tm, qk_w), _row_spec(tm, qk_w), _row_spec(tm, v_w),
                   _row_spec(tm, MLA_HEADS * QK_PAD), _row_spec(tm, kv_lora), _row_spec(tm, ROPE_DIM)],
        out_shape=[jax.ShapeDtypeStruct((n, qk_w), BF16), jax.ShapeDtypeStruct((n, qk_w), BF16),
                   jax.ShapeDtypeStruct((n, v_w), BF16), jax.ShapeDtypeStruct((n, MLA_HEADS * QK_PAD), BF16),
                   jax.ShapeDtypeStruct((n, kv_lora), F32), jax.ShapeDtypeStruct((n, ROPE_DIM), F32)],
        compiler_params=_params("parallel"),
        name="inproj",
    )(x2, nw, wa, qnw, kvnw, wuq, *tabs)


def _decompress_kernel(ckv_ref, kpe_ref, wuk_ref, wuv_ref, k_ref, v_ref):
    c = ckv_ref[...].astype(BF16)
    kn = _dot(c, wuk_ref[...])
    v_ref[...] = _dot(c, wuv_ref[...]).astype(BF16)
    kpe = kpe_ref[...]
    kpe_pad = jnp.concatenate([kpe, jnp.zeros_like(kpe)], axis=-1).astype(BF16)
    for h in range(MLA_HEADS):
        k_ref[:, h * QK_PAD:h * QK_PAD + LANES] = kn[:, h * NOPE_DIM:(h + 1) * NOPE_DIM].astype(BF16)
        k_ref[:, h * QK_PAD + LANES:(h + 1) * QK_PAD] = kpe_pad


def _decompress(ckv2, kpe2, wuk, wuv):
    n = ckv2.shape[0]
    tm = min(TOKEN_TILE, n)
    return pl.pallas_call(
        _decompress_kernel,
        grid=(n // tm,),
        in_specs=[_row_spec(tm, ckv2.shape[1]), _row_spec(tm, ROPE_DIM), _const_spec(wuk.shape),
                  _const_spec(wuv.shape)],
        out_specs=[_row_spec(tm, MLA_HEADS * QK_PAD), _row_spec(tm, MLA_HEADS * V_DIM)],
        out_shape=[jax.ShapeDtypeStruct((n, MLA_HEADS * QK_PAD), BF16),
                   jax.ShapeDtypeStruct((n, MLA_HEADS * V_DIM), BF16)],
        compiler_params=_params("parallel"),
        name="decompress",
    )(ckv2, kpe2, wuk, wuv)


def _retention_kernel(q_ref, k_ref, v_ref, s0_ref, gnw_ref, y_ref, sfin_ref, state_ref, *, blk, nblk):
    c = pl.program_id(1)

    @pl.when(c == 0)
    def _():
        state_ref[...] = s0_ref[...]

    ri = lax.broadcasted_iota(jnp.int32, (blk, blk), 0)
    ci = lax.broadcasted_iota(jnp.int32, (blk, blk), 1)
    diff = (ri - ci).astype(F32)
    n = lax.broadcasted_iota(jnp.int32, (blk, 1), 0).astype(F32)
    for h in range(RET_HEADS):
        lg = math.log(1.0 - 2.0 ** (-5.0 - h))
        q = q_ref[:, h * RET_DK:(h + 1) * RET_DK]
        k = k_ref[:, h * RET_DK:(h + 1) * RET_DK]
        v = v_ref[:, h * RET_DV:(h + 1) * RET_DV]
        decay = jnp.where(diff >= 0, jnp.exp(lg * jnp.maximum(diff, 0.0)), 0.0)
        scores = _dot_nt(q, k) * decay
        inner = _dot(scores.astype(BF16), v)
        st = state_ref[h]
        qd = (q.astype(F32) * jnp.exp(lg * (n + 1.0))).astype(BF16)
        cross = _dot(qd, st.astype(BF16))
        kd = (k.astype(F32) * jnp.exp(lg * (blk - 1.0 - n))).astype(BF16)
        state_ref[h] = math.exp(lg * blk) * st + _dot_tn(kd, v)
        y = inner + cross
        mu = jnp.mean(y, axis=-1, keepdims=True)
        yc = y - mu
        yn = yc * lax.rsqrt(jnp.mean(yc * yc, axis=-1, keepdims=True) + EPS)
        sl = slice(h * RET_DV, (h + 1) * RET_DV)
        y_ref[:, sl] = (yn * gnw_ref[:, sl]).astype(BF16)

    @pl.when(c == nblk - 1)
    def _():
        sfin_ref[...] = state_ref[...]


def _retention(rq, rk, rv, state0, gnw, blk):
    b, s, _ = rq.shape
    nblk = s // blk
    qk_w, v_w = RET_HEADS * RET_DK, RET_HEADS * RET_DV
    st_spec = pl.BlockSpec((None, RET_HEADS, RET_DK, RET_DV), lambda i, c: (i, 0, 0, 0))
    return pl.pallas_call(
        functools.partial(_retention_kernel, blk=blk, nblk=nblk),
        grid=(b, nblk),
        in_specs=[pl.BlockSpec((None, blk, qk_w), lambda i, c: (i, c, 0)),
                  pl.BlockSpec((None, blk, qk_w), lambda i, c: (i, c, 0)),
                  pl.BlockSpec((None, blk, v_w), lambda i, c: (i, c, 0)),
                  st_spec,
                  pl.BlockSpec((1, v_w), lambda i, c: (0, 0))],
        out_specs=[pl.BlockSpec((None, blk, v_w), lambda i, c: (i, c, 0)), st_spec],
        out_shape=[jax.ShapeDtypeStruct((b, s, v_w), BF16),
                   jax.ShapeDtypeStruct((b, RET_HEADS, RET_DK, RET_DV), F32)],
        scratch_shapes=[pltpu.VMEM((RET_HEADS, RET_DK, RET_DV), F32)],
        compiler_params=_params("parallel", "arbitrary"),
        name="retention",
    )(rq, rk, rv, state0, gnw)


_SOFTMAX_C = (NOPE_DIM + ROPE_DIM) ** -0.5 * math.log2(math.e)


def _flash_step(q, kb, vb, carry, mask):
    m, l, acc = carry
    s = _dot_nt(q, kb)
    if mask is not None:
        s = jnp.where(mask, s, NEG_BIG)
    m_new = jnp.maximum(m, jnp.max(s, axis=-1, keepdims=True))
    alpha = jnp.exp2((m - m_new) * _SOFTMAX_C)
    p = jnp.exp2((s - m_new) * _SOFTMAX_C)
    l = alpha * l + jnp.sum(p, axis=-1, keepdims=True)
    acc = alpha * acc + _dot(p.astype(BF16), vb)
    return m_new, l, acc


def _attn_prompt_kernel(q_ref, k_ref, v_ref, o_ref, *, blk):
    qi = pl.program_id(2)
    q = q_ref[...]

    def body(j, carry):
        off = pl.multiple_of(j * blk, blk)
        return _flash_step(q, k_ref[pl.ds(off, blk), :], v_ref[pl.ds(off, blk), :], carry, None)

    init = (jnp.full((blk, 1), NEG_BIG, F32), jnp.zeros((blk, 1), F32), jnp.zeros((blk, V_DIM), F32))
    carry = lax.fori_loop(0, qi, body, init)
    row = lax.broadcasted_iota(jnp.int32, (blk, blk), 0) // CHUNK
    col = lax.broadcasted_iota(jnp.int32, (blk, blk), 1) // CHUNK
    off = pl.multiple_of(qi * blk, blk)
    _, l, acc = _flash_step(q, k_ref[pl.ds(off, blk), :], v_ref[pl.ds(off, blk), :], carry, col <= row)
    o_ref[...] = (acc / l).astype(BF16)


def _attn_prompt(q, k, v):
    b, s, _ = q.shape
    blk = ATTN_BLOCK
    return pl.pallas_call(
        functools.partial(_attn_prompt_kernel, blk=blk),
        grid=(b, MLA_HEADS, s // blk),
        in_specs=[pl.BlockSpec((None, blk, QK_PAD), lambda i, h, j: (i, j, h)),
                  pl.BlockSpec((None, s, QK_PAD), lambda i, h, j: (i, 0, h)),
                  pl.BlockSpec((None, s, V_DIM), lambda i, h, j: (i, 0, h))],
        out_specs=pl.BlockSpec((None, blk, V_DIM), lambda i, h, j: (i, j, h)),
        out_shape=jax.ShapeDtypeStruct((b, s, MLA_HEADS * V_DIM), BF16),
        compiler_params=_params("parallel", "parallel", "arbitrary"),
        name="attn_prompt",
    )(q, k, v)


def _attn_sample_kernel(q_ref, k_ref, v_ref, o_ref):
    s = _dot_nt(q_ref[...], k_ref[...])
    m = jnp.max(s, axis=-1, keepdims=True)
    p = jnp.exp2((s - m) * _SOFTMAX_C)
    l = jnp.sum(p, axis=-1, keepdims=True)
    o_ref[...] = (_dot(p.astype(BF16), v_ref[...]) / l).astype(BF16)


def _attn_sample(q, k, v):
    b, s, _ = q.shape
    t = k.shape[1]
    return pl.pallas_call(
        _attn_sample_kernel,
        grid=(b, MLA_HEADS),
        in_specs=[pl.BlockSpec((None, s, QK_PAD), lambda i, h: (i, 0, h)),
                  pl.BlockSpec((None, t, QK_PAD), lambda i, h: (i, 0, h)),
                  pl.BlockSpec((None, t, V_DIM), lambda i, h: (i, 0, h))],
        out_specs=pl.BlockSpec((None, s, V_DIM), lambda i, h: (i, 0, h)),
        out_shape=jax.ShapeDtypeStruct((b, s, MLA_HEADS * V_DIM), BF16),
        compiler_params=_params("parallel", "parallel"),
        name="attn_sample",
    )(q, k, v)


def _mix_kernel(x_ref, ret_ref, mla_ref, nw_ref, wg_ref, wro_ref, wmo_ref, wo_ref, h_ref):
    d = x_ref.shape[1]
    v_w = ret_ref.shape[1]
    x = x_ref[...]
    xn = _rms(x, nw_ref[...]).astype(BF16)
    rg = _dot(xn, wg_ref[:, 0:v_w])
    ret_b = _dot((ret_ref[...].astype(F32) * _silu(rg)).astype(BF16), wro_ref[...])
    ga = _dot(xn, wg_ref[:, v_w:v_w + d])
    merged = _sigmoid(ga) * ret_b
    mla_b = _dot(mla_ref[...], wmo_ref[...])
    gb = _dot(xn, wg_ref[:, v_w + d:v_w + 2 * d])
    merged = merged + _sigmoid(gb) * mla_b
    h_ref[...] = x + _dot(merged.astype(BF16), wo_ref[...])


def _mix(x2, ret2, mla2, nw, wg, wro, wmo, wo):
    n, d = x2.shape
    tm = min(TOKEN_TILE, n)
    return pl.pallas_call(
        _mix_kernel,
        grid=(n // tm,),
        in_specs=[_row_spec(tm, d), _row_spec(tm, ret2.shape[1]), _row_spec(tm, mla2.shape[1]),
                  _const_spec(nw.shape), _const_spec(wg.shape), _const_spec(wro.shape),
                  _const_spec(wmo.shape), _const_spec(wo.shape)],
        out_specs=_row_spec(tm, d),
        out_shape=jax.ShapeDtypeStruct((n, d), F32),
        compiler_params=_params("parallel"),
        name="mix",
    )(x2, ret2, mla2, nw, wg, wro, wmo, wo)


def _ffn_kernel(h_ref, nw_ref, wgu_ref, wd_ref, fw_ref, y_ref, *, final_norm):
    h = h_ref[...]
    hn = _rms(h, nw_ref[...]).astype(BF16)
    acc = h
    for c in range(wd_ref.shape[0] // FF_CHUNK):
        gu = _dot(hn, wgu_ref[:, 2 * c * FF_CHUNK:2 * (c + 1) * FF_CHUNK])
        act = (_silu(gu[:, :FF_CHUNK]) * gu[:, FF_CHUNK:]).astype(BF16)
        acc = acc + _dot(act, wd_ref[c * FF_CHUNK:(c + 1) * FF_CHUNK, :])
    y_ref[...] = _rms(acc, fw_ref[...]) if final_norm else acc


def _ffn(h2, nw, wgu, wd, fw, final_norm):
    n, d = h2.shape
    tm = min(TOKEN_TILE, n)
    return pl.pallas_call(
        functools.partial(_ffn_kernel, final_norm=final_norm),
        grid=(n // tm,),
        in_specs=[_row_spec(tm, d), _const_spec(nw.shape), _const_spec(wgu.shape), _const_spec(wd.shape),
                  _const_spec(fw.shape)],
        out_specs=_row_spec(tm, d),
        out_shape=jax.ShapeDtypeStruct((n, d), F32),
        compiler_params=_params("parallel"),
        name="ffn",
    )(h2, nw, wgu, wd, fw)


def _rope_tables(pos):
    p = pos.astype(F32)[:, None]

    def cs(d):
        inv = ROPE_BASE ** (-jnp.arange(0, d, 2, dtype=F32) / d)
        ang = p * inv[None, :]
        return jnp.cos(ang), jnp.sin(ang)

    cr, sr = cs(RET_DK)
    cm, sm = cs(ROPE_DIM)
    z = jnp.zeros_like(sm)
    return (jnp.concatenate([cr, cr], -1), jnp.concatenate([-sr, sr], -1),
            jnp.concatenate([cm] * 4, -1), jnp.concatenate([-sm, z, -sm, z], -1),
            jnp.concatenate([z, sm, z, sm], -1))


def _prep_layer_weights(w_in, w_uq, w_ukv, w_ret_o, w_mla_o, w_out, w_gate_up, w_down):
    d = w_in.shape[0]
    qk_w, v_w = RET_HEADS * RET_DK, RET_HEADS * RET_DV
    q_lora = w_uq.shape[0]
    kv_lora = w_ukv.shape[0]
    o_rg = 2 * qk_w + v_w
    o_cq = o_rg + v_w
    o_kpe = o_cq + q_lora + kv_lora
    o_ga = o_kpe + ROPE_DIM
    wa = jnp.concatenate([w_in[:, :o_rg], w_in[:, o_cq:o_ga], jnp.zeros((d, LANES - ROPE_DIM), w_in.dtype)],
                         axis=1).astype(BF16)
    wg = jnp.concatenate([w_in[:, o_rg:o_cq], w_in[:, o_ga:]], axis=1).astype(BF16)
    head_w = NOPE_DIM + ROPE_DIM
    wuq = jnp.pad(w_uq.reshape(q_lora, MLA_HEADS, head_w), ((0, 0), (0, 0), (0, QK_PAD - head_w)))
    wuq = wuq.reshape(q_lora, MLA_HEADS * QK_PAD).astype(BF16)
    wkv = w_ukv.reshape(kv_lora, MLA_HEADS, NOPE_DIM + V_DIM)
    wuk = wkv[:, :, :NOPE_DIM].reshape(kv_lora, MLA_HEADS * NOPE_DIM).astype(BF16)
    wuv = wkv[:, :, NOPE_DIM:].reshape(kv_lora, MLA_HEADS * V_DIM).astype(BF16)
    d_ff = w_down.shape[0]
    nch = d_ff // FF_CHUNK
    wgu = w_gate_up.reshape(d, 2, nch, FF_CHUNK).transpose(0, 2, 1, 3).reshape(d, 2 * d_ff).astype(BF16)
    return dict(wa=wa, wg=wg, wuq=wuq, wuk=wuk, wuv=wuv, wro=w_ret_o.astype(BF16), wmo=w_mla_o.astype(BF16),
                wo=w_out.astype(BF16), wgu=wgu, wd=w_down.astype(BF16))


def _layer(x, tabs, tab_tiles, lw, norms, state0, cache, fw, final_norm):
    b, s, d = x.shape
    n = b * s
    x2 = x.reshape(n, d)
    nmw, qnw, kvnw, gnw, nfw = norms
    rq, rk, rv, q, ckv, kpe = _inproj(x2, tabs, nmw, lw["wa"], qnw, kvnw, lw["wuq"], tab_tiles=tab_tiles)
    kv_lora = ckv.shape[1]
    ckv3 = ckv.reshape(b, s, kv_lora)
    kpe3 = kpe.reshape(b, s, ROPE_DIM)
    if cache is None:
        keys_ckv, keys_kpe = ckv3, kpe3
    else:
        keys_ckv = jnp.concatenate([cache[0].astype(F32), ckv3], axis=1)
        keys_kpe = jnp.concatenate([cache[1].astype(F32), kpe3], axis=1)
    t = keys_ckv.shape[1]
    kk, vv = _decompress(keys_ckv.reshape(b * t, kv_lora), keys_kpe.reshape(b * t, ROPE_DIM), lw["wuk"], lw["wuv"])
    kk = kk.reshape(b, t, MLA_HEADS * QK_PAD)
    vv = vv.reshape(b, t, MLA_HEADS * V_DIM)
    q3 = q.reshape(b, s, MLA_HEADS * QK_PAD)
    mla = _attn_prompt(q3, kk, vv) if cache is None else _attn_sample(q3, kk, vv)
    blk = RET_BLOCK if s % RET_BLOCK == 0 else CHUNK
    ret, s_fin = _retention(rq.reshape(b, s, -1), rk.reshape(b, s, -1), rv.reshape(b, s, -1), state0, gnw, blk)
    h2 = _mix(x2, ret.reshape(n, -1), mla.reshape(n, -1), nmw, lw["wg"], lw["wro"], lw["wmo"], lw["wo"])
    y2 = _ffn(h2, nfw, lw["wgu"], lw["wd"], fw, final_norm)
    return y2.reshape(b, s, d), (ckv3, kpe3, s_fin)


def kernel(x_prompt, x_sample, cache_ckv, cache_kpe, state_ret, norm_mix_w, w_in, q_norm_w, w_uq, kv_norm_w, w_ukv, ret_gn_w, w_ret_o, w_mla_o, w_out, norm_ffn_w, w_gate_up, w_down, norm_final_w):
    depth = w_in.shape[0]
    bp, sp, _ = x_prompt.shape
    bs, ss, _ = x_sample.shape
    past = cache_ckv.shape[2]
    assert sp % TOKEN_TILE == 0 and sp % ATTN_BLOCK == 0 and (bs * ss) % min(TOKEN_TILE, bs * ss) == 0
    assert ss == CHUNK, "the sample group is one streaming chunk"

    tabs_p = _rope_tables(jnp.arange(sp))
    tabs_s = tuple(jnp.tile(t, (bs, 1)) for t in _rope_tables(past + jnp.arange(ss)))
    fw = norm_final_w.reshape(1, -1)
    state0_p = jnp.zeros((bp, RET_HEADS, RET_DK, RET_DV), F32)

    hp, hs = x_prompt, x_sample
    outs = [[] for _ in range(6)]
    for l in range(depth):
        lw = _prep_layer_weights(w_in[l], w_uq[l], w_ukv[l], w_ret_o[l], w_mla_o[l], w_out[l], w_gate_up[l],
                                 w_down[l])
        norms = (norm_mix_w[l].reshape(1, -1), q_norm_w[l].reshape(1, -1), kv_norm_w[l].reshape(1, -1),
                 ret_gn_w[l].reshape(1, -1), norm_ffn_w[l].reshape(1, -1))
        final = l == depth - 1
        hp, (a, b_, c) = _layer(hp, tabs_p, sp // TOKEN_TILE, lw, norms, state0_p, None, fw, final)
        hs, (d_, e, f) = _layer(hs, tabs_s, 1, lw, norms, state_ret[l].astype(F32),
                                (cache_ckv[l], cache_kpe[l]), fw, final)
        for lst, val in zip(outs, (a, b_, c, d_, e, f)):
            lst.append(val)
    return (hp, hs) + tuple(jnp.stack(o) for o in outs)
```

```python
import functools
import math

import jax
import jax.numpy as jnp
from jax import lax
from jax.experimental import pallas as pl
from jax.experimental.pallas import tpu as pltpu

F32 = jnp.float32
BF16 = jnp.bfloat16

CHUNK = 64
RET_HEADS = 4
RET_DK = 128
RET_DV = 256
MLA_HEADS = 8
NOPE_DIM = 128
ROPE_DIM = 64
V_DIM = 128
ROPE_BASE = 10000.0
EPS = 1e-6

LANES = 128
QK_PAD = 2 * LANES
V7X_VMEM_BYTES = 64 * 2**20
VMEM_LIMIT = V7X_VMEM_BYTES * 7 // 8
TOKEN_TILE = 512
RET_BLOCK = 256
ATTN_Q_TILE = 1024
FF_CHUNK = 256
NEG_BIG = -1e30

_dot = functools.partial(jnp.dot, preferred_element_type=F32)


def _dot_nt(a, b):
    return lax.dot_general(a, b, (((1,), (1,)), ((), ())), preferred_element_type=F32)


def _dot_tn(a, b):
    return lax.dot_general(a, b, (((0,), (0,)), ((), ())), preferred_element_type=F32)


def _rms(x, g):
    ms = jnp.mean(x * x, axis=-1, keepdims=True)
    return x * lax.rsqrt(ms + EPS) * g


def _sigmoid(x):
    return 1.0 / (1.0 + jnp.exp(-x))


def _silu(x):
    return x * _sigmoid(x)


def _params(*sem):
    return pltpu.CompilerParams(dimension_semantics=sem, vmem_limit_bytes=VMEM_LIMIT)


def _const_spec(shape):
    return pl.BlockSpec(shape, lambda *_: (0,) * len(shape), pipeline_mode=pl.Buffered(1))


def _row_spec(tm, width):
    return pl.BlockSpec((tm, width), lambda i: (i, 0))


def _rope_full(x, cos2, sin2):
    return x * cos2 + pltpu.roll(x, LANES // 2, 1) * sin2


def _rope_half(x, c, sa, sb):
    q = ROPE_DIM // 2
    return x * c + pltpu.roll(x, LANES - q, 1) * sa + pltpu.roll(x, q, 1) * sb


def _inproj_kernel(x_ref, nw_ref, wa_ref, qnw_ref, kvnw_ref, wuq_ref,
                   cr_ref, sr_ref, cm_ref, sa_ref, sb_ref,
                   rq_ref, rk_ref, rv_ref, q_ref, ckv_ref, kpe_ref):
    qk_w = RET_HEADS * RET_DK
    v_w = RET_HEADS * RET_DV
    q_lora = wuq_ref.shape[0]
    kv_lora = ckv_ref.shape[1]
    o_rk, o_rv, o_cq = qk_w, 2 * qk_w, 2 * qk_w + v_w
    o_ckv = o_cq + q_lora
    o_kpe = o_ckv + kv_lora

    xn = _rms(x_ref[...], nw_ref[...]).astype(BF16)
    cr, sr = cr_ref[...], sr_ref[...]
    cm, sa, sb = cm_ref[...], sa_ref[...], sb_ref[...]

    zq = _dot(xn, wa_ref[:, 0:qk_w])
    zk = _dot(xn, wa_ref[:, o_rk:o_rk + qk_w])
    k_scale = RET_DK ** -0.5
    for h in range(RET_HEADS):
        sl = slice(h * RET_DK, (h + 1) * RET_DK)
        rq_ref[:, sl] = _rope_full(zq[:, sl], cr, sr).astype(BF16)
        rk_ref[:, sl] = (_rope_full(zk[:, sl], cr, sr) * k_scale).astype(BF16)
    rv_ref[...] = _dot(xn, wa_ref[:, o_rv:o_rv + v_w]).astype(BF16)

    cq = _dot(xn, wa_ref[:, o_cq:o_cq + q_lora])
    cqn = _rms(cq, qnw_ref[...]).astype(BF16)
    for h in range(MLA_HEADS):
        qh = _dot(cqn, wuq_ref[:, h * QK_PAD:(h + 1) * QK_PAD])
        q_ref[:, h * QK_PAD:h * QK_PAD + LANES] = qh[:, :LANES].astype(BF16)
        q_ref[:, h * QK_PAD + LANES:(h + 1) * QK_PAD] = _rope_half(qh[:, LANES:], cm, sa, sb).astype(BF16)

    ckv = _dot(xn, wa_ref[:, o_ckv:o_ckv + kv_lora])
    ckv_ref[...] = _rms(ckv, kvnw_ref[...])
    kp = _dot(xn, wa_ref[:, o_kpe:o_kpe + LANES])
    kpe_ref[...] = _rope_half(kp, cm, sa, sb)[:, :ROPE_DIM]


def _inproj(x2, tabs, nw, wa, qnw, kvnw, wuq, *, tab_tiles):
    n, d = x2.shape
    tm = min(TOKEN_TILE, n)
    qk_w, v_w = RET_HEADS * RET_DK, RET_HEADS * RET_DV
    kv_lora = kvnw.shape[1]
    tab_spec = pl.BlockSpec((tm, LANES), lambda i: (i % tab_tiles, 0))
    return pl.pallas_call(
        _inproj_kernel,
        grid=(n // tm,),
        in_specs=[_row_spec(tm, d), _const_spec(nw.shape), _const_spec(wa.shape), _const_spec(qnw.shape),
                  _const_spec(kvnw.shape), _const_spec(wuq.shape)] + [tab_spec] * 5,
        out_specs=[_row_spec(tm, qk_w), _row_spec(tm, qk_w), _row_spec(tm, v_w),
                   _row_spec(tm, MLA_HEADS * QK_PAD), _row_spec(tm, kv_lora), _row_spec(tm, ROPE_DIM)],
        out_shape=[jax.ShapeDtypeStruct((n, qk_w), BF16), jax.ShapeDtypeStruct((n, qk_w), BF16),
                   jax.ShapeDtypeStruct((n, v_w), BF16), jax.ShapeDtypeStruct((n, MLA_HEADS * QK_PAD), BF16),
                   jax.ShapeDtypeStruct((n, kv_lora), F32), jax.ShapeDtypeStruct((n, ROPE_DIM), F32)],
        compiler_params=_params("parallel"),
        name="inproj",
    )(x2, nw, wa, qnw, kvnw, wuq, *tabs)


def _decompress_kernel(ckv_ref, kpe_ref, wuk_ref, wuv_ref, k_ref, v_ref):
    c = ckv_ref[...].astype(BF16)
    kn = _dot(c, wuk_ref[...])
    v_ref[...] = _dot(c, wuv_ref[...]).astype(BF16)
    kpe = kpe_ref[...]
    kpe_pad = jnp.concatenate([kpe, jnp.zeros_like(kpe)], axis=-1).astype(BF16)
    for h in range(MLA_HEADS):
        k_ref[:, h * QK_PAD:h * QK_PAD + LANES] = kn[:, h * NOPE_DIM:(h + 1) * NOPE_DIM].astype(BF16)
        k_ref[:, h * QK_PAD + LANES:(h + 1) * QK_PAD] = kpe_pad


def _decompress(ckv2, kpe2, wuk, wuv):
    n = ckv2.shape[0]
    tm = min(TOKEN_TILE, n)
    return pl.pallas_call(
        _decompress_kernel,
        grid=(n // tm,),
        in_specs=[_row_spec(tm, ckv2.shape[1]), _row_spec(tm, ROPE_DIM), _const_spec(wuk.shape),
                  _const_spec(wuv.shape)],
        out_specs=[_row_spec(tm, MLA_HEADS * QK_PAD), _row_spec(tm, MLA_HEADS * V_DIM)],
        out_shape=[jax.ShapeDtypeStruct((n, MLA_HEADS * QK_PAD), BF16),
                   jax.ShapeDtypeStruct((n, MLA_HEADS * V_DIM), BF16)],
        compiler_params=_params("parallel"),
        name="decompress",
    )(ckv2, kpe2, wuk, wuv)


def _retention_kernel(q_ref, k_ref, v_ref, s0_ref, gnw_ref, y_ref, sfin_ref, state_ref, *, blk, nblk):
    c = pl.program_id(1)

    @pl.when(c == 0)
    def _():
        state_ref[...] = s0_ref[...]

    ri = lax.broadcasted_iota(jnp.int32, (blk, blk), 0)
    ci = lax.broadcasted_iota(jnp.int32, (blk, blk), 1)
    diff = (ri - ci).astype(F32)
    n = lax.broadcasted_iota(jnp.int32, (blk, 1), 0).astype(F32)
    for h in range(RET_HEADS):
        lg = math.log(1.0 - 2.0 ** (-5.0 - h))
        q = q_ref[:, h * RET_DK:(h + 1) * RET_DK]
        k = k_ref[:, h * RET_DK:(h + 1) * RET_DK]
        v = v_ref[:, h * RET_DV:(h + 1) * RET_DV]
        decay = jnp.where(diff >= 0, jnp.exp(lg * jnp.maximum(diff, 0.0)), 0.0)
        scores = _dot_nt(q, k) * decay
        inner = _dot(scores.astype(BF16), v)
        st = state_ref[h]
        qd = (q.astype(F32) * jnp.exp(lg * (n + 1.0))).astype(BF16)
        cross = _dot(qd, st.astype(BF16))
        kd = (k.astype(F32) * jnp.exp(lg * (blk - 1.0 - n))).astype(BF16)
        state_ref[h] = math.exp(lg * blk) * st + _dot_tn(kd, v)
        y = inner + cross
        mu = jnp.mean(y, axis=-1, keepdims=True)
        yc = y - mu
        yn = yc * lax.rsqrt(jnp.mean(yc * yc, axis=-1, keepdims=True) + EPS)
        sl = slice(h * RET_DV, (h + 1) * RET_DV)
        y_ref[:, sl] = (yn * gnw_ref[:, sl]).astype(BF16)

    @pl.when(c == nblk - 1)
    def _():
        sfin_ref[...] = state_ref[...]


def _retention(rq, rk, rv, state0, gnw, blk):
    b, s, _ = rq.shape
    nblk = s // blk
    qk_w, v_w = RET_HEADS * RET_DK, RET_HEADS * RET_DV
    st_spec = pl.BlockSpec((None, RET_HEADS, RET_DK, RET_DV), lambda i, c: (i, 0, 0, 0))
    return pl.pallas_call(
        functools.partial(_retention_kernel, blk=blk, nblk=nblk),
        grid=(b, nblk),
        in_specs=[pl.BlockSpec((None, blk, qk_w), lambda i, c: (i, c, 0)),
                  pl.BlockSpec((None, blk, qk_w), lambda i, c: (i, c, 0)),
                  pl.BlockSpec((None, blk, v_w), lambda i, c: (i, c, 0)),
                  st_spec,
                  pl.BlockSpec((1, v_w), lambda i, c: (0, 0))],
        out_specs=[pl.BlockSpec((None, blk, v_w), lambda i, c: (i, c, 0)), st_spec],
        out_shape=[jax.ShapeDtypeStruct((b, s, v_w), BF16),
                   jax.ShapeDtypeStruct((b, RET_HEADS, RET_DK, RET_DV), F32)],
        scratch_shapes=[pltpu.VMEM((RET_HEADS, RET_DK, RET_DV), F32)],
        compiler_params=_params("parallel", "arbitrary"),
        name="retention",
    )(rq, rk, rv, state0, gnw)


_SOFTMAX_C = (NOPE_DIM + ROPE_DIM) ** -0.5 * math.log2(math.e)


def _attn_prompt_kernel(q_ref, k_ref, v_ref, o_ref, s_scr, *, tq, tk):
    qi = pl.program_id(2)
    q = q_ref[...]
    ones = jnp.ones((tk, V_DIM), BF16)

    def scores(t):
        return _dot_nt(q, k_ref[pl.ds(pl.multiple_of(t * tk, tk), tk), :])

    def consume(s, t, carry, mask):
        m, acc = carry
        if mask is not None:
            s = jnp.where(mask, s, NEG_BIG)
        m_new = jnp.maximum(m, jnp.max(s, axis=-1, keepdims=True))
        alpha = jnp.exp2((m - m_new) * _SOFTMAX_C)
        p = jnp.exp2((s - m_new) * _SOFTMAX_C).astype(BF16)
        v_ext = jnp.concatenate([v_ref[pl.ds(pl.multiple_of(t * tk, tk), tk), :], ones], axis=1)
        return m_new, alpha * acc + _dot(p, v_ext)

    def body(j, carry):
        s_odd = scores(2 * j + 1)
        carry = consume(s_scr[...], 2 * j, carry, None)
        s_scr[...] = scores(2 * j + 2)
        return consume(s_odd, 2 * j + 1, carry, None)

    s_scr[...] = scores(0)
    init = (jnp.full((tq, 1), NEG_BIG, F32), jnp.zeros((tq, 2 * V_DIM), F32))
    carry = lax.fori_loop(0, qi, body, init)
    row = lax.broadcasted_iota(jnp.int32, (tq, tk), 0) // CHUNK
    col = lax.broadcasted_iota(jnp.int32, (tq, tk), 1) // CHUNK
    s_odd = scores(2 * qi + 1)
    carry = consume(s_scr[...], 2 * qi, carry, col <= row)
    _, acc = consume(s_odd, 2 * qi + 1, carry, col + tk // CHUNK <= row)
    o_ref[...] = (acc[:, :V_DIM] / acc[:, V_DIM:]).astype(BF16)


def _attn_prompt(q, k, v):
    b, s, _ = q.shape
    tq, tk = ATTN_Q_TILE, ATTN_Q_TILE // 2
    return pl.pallas_call(
        functools.partial(_attn_prompt_kernel, tq=tq, tk=tk),
        grid=(b, MLA_HEADS, s // tq),
        in_specs=[pl.BlockSpec((None, tq, QK_PAD), lambda i, h, j: (i, j, h)),
                  pl.BlockSpec((None, s, QK_PAD), lambda i, h, j: (i, 0, h)),
                  pl.BlockSpec((None, s, V_DIM), lambda i, h, j: (i, 0, h))],
        out_specs=pl.BlockSpec((None, tq, V_DIM), lambda i, h, j: (i, j, h)),
        out_shape=jax.ShapeDtypeStruct((b, s, MLA_HEADS * V_DIM), BF16),
        scratch_shapes=[pltpu.VMEM((tq, tk), F32)],
        compiler_params=_params("parallel", "parallel", "arbitrary"),
        name="attn_prompt",
    )(q, k, v)


def _attn_sample_kernel(q_ref, k_ref, v_ref, o_ref):
    s = _dot_nt(q_ref[...], k_ref[...])
    m = jnp.max(s, axis=-1, keepdims=True)
    p = jnp.exp2((s - m) * _SOFTMAX_C)
    l = jnp.sum(p, axis=-1, keepdims=True)
    o_ref[...] = (_dot(p.astype(BF16), v_ref[...]) / l).astype(BF16)


def _attn_sample(q, k, v):
    b, s, _ = q.shape
    t = k.shape[1]
    return pl.pallas_call(
        _attn_sample_kernel,
        grid=(b, MLA_HEADS),
        in_specs=[pl.BlockSpec((None, s, QK_PAD), lambda i, h: (i, 0, h)),
                  pl.BlockSpec((None, t, QK_PAD), lambda i, h: (i, 0, h)),
                  pl.BlockSpec((None, t, V_DIM), lambda i, h: (i, 0, h))],
        out_specs=pl.BlockSpec((None, s, V_DIM), lambda i, h: (i, 0, h)),
        out_shape=jax.ShapeDtypeStruct((b, s, MLA_HEADS * V_DIM), BF16),
        compiler_params=_params("parallel", "parallel"),
        name="attn_sample",
    )(q, k, v)


def _mix_kernel(x_ref, ret_ref, mla_ref, nw_ref, wg_ref, wro_ref, wmo_ref, wo_ref, h_ref):
    d = x_ref.shape[1]
    v_w = ret_ref.shape[1]
    x = x_ref[...]
    xn = _rms(x, nw_ref[...]).astype(BF16)
    rg = _dot(xn, wg_ref[:, 0:v_w])
    ret_b = _dot((ret_ref[...].astype(F32) * _silu(rg)).astype(BF16), wro_ref[...])
    ga = _dot(xn, wg_ref[:, v_w:v_w + d])
    merged = _sigmoid(ga) * ret_b
    mla_b = _dot(mla_ref[...], wmo_ref[...])
    gb = _dot(xn, wg_ref[:, v_w + d:v_w + 2 * d])
    merged = merged + _sigmoid(gb) * mla_b
    h_ref[...] = x + _dot(merged.astype(BF16), wo_ref[...])


def _mix(x2, ret2, mla2, nw, wg, wro, wmo, wo):
    n, d = x2.shape
    tm = min(TOKEN_TILE, n)
    return pl.pallas_call(
        _mix_kernel,
        grid=(n // tm,),
        in_specs=[_row_spec(tm, d), _row_spec(tm, ret2.shape[1]), _row_spec(tm, mla2.shape[1]),
                  _const_spec(nw.shape), _const_spec(wg.shape), _const_spec(wro.shape),
                  _const_spec(wmo.shape), _const_spec(wo.shape)],
        out_specs=_row_spec(tm, d),
        out_shape=jax.ShapeDtypeStruct((n, d), F32),
        compiler_params=_params("parallel"),
        name="mix",
    )(x2, ret2, mla2, nw, wg, wro, wmo, wo)


def _ffn_kernel(h_ref, nw_ref, wgu_ref, wd_ref, fw_ref, y_ref, *, final_norm):
    h = h_ref[...]
    hn = _rms(h, nw_ref[...]).astype(BF16)
    acc = h
    for c in range(wd_ref.shape[0] // FF_CHUNK):
        gu = _dot(hn, wgu_ref[:, 2 * c * FF_CHUNK:2 * (c + 1) * FF_CHUNK])
        act = (_silu(gu[:, :FF_CHUNK]) * gu[:, FF_CHUNK:]).astype(BF16)
        acc = acc + _dot(act, wd_ref[c * FF_CHUNK:(c + 1) * FF_CHUNK, :])
    y_ref[...] = _rms(acc, fw_ref[...]) if final_norm else acc


def _ffn(h2, nw, wgu, wd, fw, final_norm):
    n, d = h2.shape
    tm = min(TOKEN_TILE, n)
    return pl.pallas_call(
        functools.partial(_ffn_kernel, final_norm=final_norm),
        grid=(n // tm,),
        in_specs=[_row_spec(tm, d), _const_spec(nw.shape), _const_spec(wgu.shape), _const_spec(wd.shape),
                  _const_spec(fw.shape)],
        out_specs=_row_spec(tm, d),
        out_shape=jax.ShapeDtypeStruct((n, d), F32),
        compiler_params=_params("parallel"),
        name="ffn",
    )(h2, nw, wgu, wd, fw)


def _rope_tables(pos):
    p = pos.astype(F32)[:, None]

    def cs(d):
        inv = ROPE_BASE ** (-jnp.arange(0, d, 2, dtype=F32) / d)
        ang = p * inv[None, :]
        return jnp.cos(ang), jnp.sin(ang)

    cr, sr = cs(RET_DK)
    cm, sm = cs(ROPE_DIM)
    z = jnp.zeros_like(sm)
    return (jnp.concatenate([cr, cr], -1), jnp.concatenate([-sr, sr], -1),
            jnp.concatenate([cm] * 4, -1), jnp.concatenate([-sm, z, -sm, z], -1),
            jnp.concatenate([z, sm, z, sm], -1))


def _prep_layer_weights(w_in, w_uq, w_ukv, w_ret_o, w_mla_o, w_out, w_gate_up, w_down):
    d = w_in.shape[0]
    qk_w, v_w = RET_HEADS * RET_DK, RET_HEADS * RET_DV
    q_lora = w_uq.shape[0]
    kv_lora = w_ukv.shape[0]
    o_rg = 2 * qk_w + v_w
    o_cq = o_rg + v_w
    o_kpe = o_cq + q_lora + kv_lora
    o_ga = o_kpe + ROPE_DIM
    wa = jnp.concatenate([w_in[:, :o_rg], w_in[:, o_cq:o_ga], jnp.zeros((d, LANES - ROPE_DIM), w_in.dtype)],
                         axis=1).astype(BF16)
    wg = jnp.concatenate([w_in[:, o_rg:o_cq], w_in[:, o_ga:]], axis=1).astype(BF16)
    head_w = NOPE_DIM + ROPE_DIM
    wuq = jnp.pad(w_uq.reshape(q_lora, MLA_HEADS, head_w), ((0, 0), (0, 0), (0, QK_PAD - head_w)))
    wuq = wuq.reshape(q_lora, MLA_HEADS * QK_PAD).astype(BF16)
    wkv = w_ukv.reshape(kv_lora, MLA_HEADS, NOPE_DIM + V_DIM)
    wuk = wkv[:, :, :NOPE_DIM].reshape(kv_lora, MLA_HEADS * NOPE_DIM).astype(BF16)
    wuv = wkv[:, :, NOPE_DIM:].reshape(kv_lora, MLA_HEADS * V_DIM).astype(BF16)
    d_ff = w_down.shape[0]
    nch = d_ff // FF_CHUNK
    wgu = w_gate_up.reshape(d, 2, nch, FF_CHUNK).transpose(0, 2, 1, 3).reshape(d, 2 * d_ff).astype(BF16)
    return dict(wa=wa, wg=wg, wuq=wuq, wuk=wuk, wuv=wuv, wro=w_ret_o.astype(BF16), wmo=w_mla_o.astype(BF16),
                wo=w_out.astype(BF16), wgu=wgu, wd=w_down.astype(BF16))


def _layer(x, tabs, tab_tiles, lw, norms, state0, cache, fw, final_norm):
    b, s, d = x.shape
    n = b * s
    x2 = x.reshape(n, d)
    nmw, qnw, kvnw, gnw, nfw = norms
    rq, rk, rv, q, ckv, kpe = _inproj(x2, tabs, nmw, lw["wa"], qnw, kvnw, lw["wuq"], tab_tiles=tab_tiles)
    kv_lora = ckv.shape[1]
    ckv3 = ckv.reshape(b, s, kv_lora)
    kpe3 = kpe.reshape(b, s, ROPE_DIM)
    if cache is None:
        keys_ckv, keys_kpe = ckv3, kpe3
    else:
        keys_ckv = jnp.concatenate([cache[0].astype(F32), ckv3], axis=1)
        keys_kpe = jnp.concatenate([cache[1].astype(F32), kpe3], axis=1)
    t = keys_ckv.shape[1]
    kk, vv = _decompress(keys_ckv.reshape(b * t, kv_lora), keys_kpe.reshape(b * t, ROPE_DIM), lw["wuk"], lw["wuv"])
    kk = kk.reshape(b, t, MLA_HEADS * QK_PAD)
    vv = vv.reshape(b, t, MLA_HEADS * V_DIM)
    q3 = q.reshape(b, s, MLA_HEADS * QK_PAD)
    mla = _attn_prompt(q3, kk, vv) if cache is None else _attn_sample(q3, kk, vv)
    blk = RET_BLOCK if s % RET_BLOCK == 0 else CHUNK
    ret, s_fin = _retention(rq.reshape(b, s, -1), rk.reshape(b, s, -1), rv.reshape(b, s, -1), state0, gnw, blk)
    h2 = _mix(x2, ret.reshape(n, -1), mla.reshape(n, -1), nmw, lw["wg"], lw["wro"], lw["wmo"], lw["wo"])
    y2 = _ffn(h2, nfw, lw["wgu"], lw["wd"], fw, final_norm)
    return y2.reshape(b, s, d), (ckv3, kpe3, s_fin)


def kernel(x_prompt, x_sample, cache_ckv, cache_kpe, state_ret, norm_mix_w, w_in, q_norm_w, w_uq, kv_norm_w, w_ukv, ret_gn_w, w_ret_o, w_mla_o, w_out, norm_ffn_w, w_gate_up, w_down, norm_final_w):
    depth = w_in.shape[0]
    bp, sp, _ = x_prompt.shape
    bs, ss, _ = x_sample.shape
    past = cache_ckv.shape[2]
    assert sp % TOKEN_TILE == 0 and sp % ATTN_Q_TILE == 0 and (bs * ss) % min(TOKEN_TILE, bs * ss) == 0
    assert ss == CHUNK, "the sample group is one streaming chunk"

    tabs_p = _rope_tables(jnp.arange(sp))
    tabs_s = tuple(jnp.tile(t, (bs, 1)) for t in _rope_tables(past + jnp.arange(ss)))
    fw = norm_final_w.reshape(1, -1)
    state0_p = jnp.zeros((bp, RET_HEADS, RET_DK, RET_DV), F32)

    hp, hs = x_prompt, x_sample
    outs = [[] for _ in range(6)]
    for l in range(depth):
        lw = _prep_layer_weights(w_in[l], w_uq[l], w_ukv[l], w_ret_o[l], w_mla_o[l], w_out[l], w_gate_up[l],
                                 w_down[l])
        norms = (norm_mix_w[l].reshape(1, -1), q_norm_w[l].reshape(1, -1), kv_norm_w[l].reshape(1, -1),
                 ret_gn_w[l].reshape(1, -1), norm_ffn_w[l].reshape(1, -1))
        final = l == depth - 1
        hp, (a, b_, c) = _layer(hp, tabs_p, sp // TOKEN_TILE, lw, norms, state0_p, None, fw, final)
        hs, (d_, e, f) = _layer(hs, tabs_s, 1, lw, norms, state_ret[l].astype(F32),
                                (cache_ckv[l], cache_kpe[l]), fw, final)
        for lst, val in zip(outs, (a, b_, c, d_, e, f)):
            lst.append(val)
    return (hp, hs) + tuple(jnp.stack(o) for o in outs)
```

```python
import functools
import math

import jax
import jax.numpy as jnp
from jax import lax
from jax.experimental import pallas as pl
from jax.experimental.pallas import tpu as pltpu

F32 = jnp.float32
BF16 = jnp.bfloat16

CHUNK = 64
RET_HEADS = 4
RET_DK = 128
RET_DV = 256
MLA_HEADS = 8
NOPE_DIM = 128
ROPE_DIM = 64
V_DIM = 128
ROPE_BASE = 10000.0
EPS = 1e-6

LANES = 128
QK_PAD = 2 * LANES
V7X_VMEM_BYTES = 64 * 2**20
VMEM_LIMIT = V7X_VMEM_BYTES * 7 // 8
TOKEN_TILE = 512
RET_BLOCK = 256
ATTN_Q_TILE = 1024
FF_CHUNK = 256
NEG_BIG = -1e30

_dot = functools.partial(jnp.dot, preferred_element_type=F32)


def _dot_nt(a, b):
    return lax.dot_general(a, b, (((1,), (1,)), ((), ())), preferred_element_type=F32)


def _dot_tn(a, b):
    return lax.dot_general(a, b, (((0,), (0,)), ((), ())), preferred_element_type=F32)


def _rms(x, g):
    ms = jnp.mean(x * x, axis=-1, keepdims=True)
    return x * lax.rsqrt(ms + EPS) * g


def _sigmoid(x):
    return 1.0 / (1.0 + jnp.exp(-x))


def _silu(x):
    return x * _sigmoid(x)


def _params(*sem):
    return pltpu.CompilerParams(dimension_semantics=sem, vmem_limit_bytes=VMEM_LIMIT)


def _const_spec(shape):
    return pl.BlockSpec(shape, lambda *_: (0,) * len(shape), pipeline_mode=pl.Buffered(1))


def _row_spec(tm, width):
    return pl.BlockSpec((tm, width), lambda i: (i, 0))


def _rope_full(x, cos2, sin2):
    return x * cos2 + pltpu.roll(x, LANES // 2, 1) * sin2


def _rope_half(x, c, sa, sb):
    q = ROPE_DIM // 2
    return x * c + pltpu.roll(x, LANES - q, 1) * sa + pltpu.roll(x, q, 1) * sb


def _inproj_kernel(x_ref, nw_ref, wa_ref, qnw_ref, kvnw_ref, wuq_ref,
                   cr_ref, sr_ref, cm_ref, sa_ref, sb_ref,
                   rq_ref, rk_ref, rv_ref, q_ref, ckv_ref, kpe_ref):
    qk_w = RET_HEADS * RET_DK
    v_w = RET_HEADS * RET_DV
    q_lora = wuq_ref.shape[0]
    kv_lora = ckv_ref.shape[1]
    o_rk, o_rv, o_cq = qk_w, 2 * qk_w, 2 * qk_w + v_w
    o_ckv = o_cq + q_lora
    o_kpe = o_ckv + kv_lora

    xn = _rms(x_ref[...], nw_ref[...]).astype(BF16)
    cr, sr = cr_ref[...], sr_ref[...]
    cm, sa, sb = cm_ref[...], sa_ref[...], sb_ref[...]

    zq = _dot(xn, wa_ref[:, 0:qk_w])
    zk = _dot(xn, wa_ref[:, o_rk:o_rk + qk_w])
    k_scale = RET_DK ** -0.5
    for h in range(RET_HEADS):
        sl = slice(h * RET_DK, (h + 1) * RET_DK)
        rq_ref[:, sl] = _rope_full(zq[:, sl], cr, sr).astype(BF16)
        rk_ref[:, sl] = (_rope_full(zk[:, sl], cr, sr) * k_scale).astype(BF16)
    rv_ref[...] = _dot(xn, wa_ref[:, o_rv:o_rv + v_w]).astype(BF16)

    cq = _dot(xn, wa_ref[:, o_cq:o_cq + q_lora])
    cqn = _rms(cq, qnw_ref[...]).astype(BF16)
    for h in range(MLA_HEADS):
        qh = _dot(cqn, wuq_ref[:, h * QK_PAD:(h + 1) * QK_PAD])
        q_ref[:, h * QK_PAD:h * QK_PAD + LANES] = qh[:, :LANES].astype(BF16)
        q_ref[:, h * QK_PAD + LANES:(h + 1) * QK_PAD] = _rope_half(qh[:, LANES:], cm, sa, sb).astype(BF16)

    ckv = _dot(xn, wa_ref[:, o_ckv:o_ckv + kv_lora])
    ckv_ref[...] = _rms(ckv, kvnw_ref[...])
    kp = _dot(xn, wa_ref[:, o_kpe:o_kpe + LANES])
    kpe_ref[...] = _rope_half(kp, cm, sa, sb)[:, :ROPE_DIM]


def _inproj(x2, tabs, nw, wa, qnw, kvnw, wuq, *, tab_tiles):
    n, d = x2.shape
    tm = min(TOKEN_TILE, n)
    qk_w, v_w = RET_HEADS * RET_DK, RET_HEADS * RET_DV
    kv_lora = kvnw.shape[1]
    tab_spec = pl.BlockSpec((tm, LANES), lambda i: (i % tab_tiles, 0))
    return pl.pallas_call(
        _inproj_kernel,
        grid=(n // tm,),
        in_specs=[_row_spec(tm, d), _const_spec(nw.shape), _const_spec(wa.shape), _const_spec(qnw.shape),
                  _const_spec(kvnw.shape), _const_spec(wuq.shape)] + [tab_spec] * 5,
        out_specs=[_row_spec(tm, qk_w), _row_spec(tm, qk_w), _row_spec(tm, v_w),
                   _row_spec(tm, MLA_HEADS * QK_PAD), _row_spec(tm, kv_lora), _row_spec(tm, ROPE_DIM)],
        out_shape=[jax.ShapeDtypeStruct((n, qk_w), BF16), jax.ShapeDtypeStruct((n, qk_w), BF16),
                   jax.ShapeDtypeStruct((n, v_w), BF16), jax.ShapeDtypeStruct((n, MLA_HEADS * QK_PAD), BF16),
                   jax.ShapeDtypeStruct((n, kv_lora), F32), jax.ShapeDtypeStruct((n, ROPE_DIM), F32)],
        compiler_params=_params("parallel"),
        name="inproj",
    )(x2, nw, wa, qnw, kvnw, wuq, *tabs)


def _decompress_kernel(ckv_ref, kpe_ref, wuk_ref, wuv_ref, k_ref, v_ref):
    c = ckv_ref[...].astype(BF16)
    kn = _dot(c, wuk_ref[...])
    v_ref[...] = _dot(c, wuv_ref[...]).astype(BF16)
    kpe = kpe_ref[...]
    kpe_pad = jnp.concatenate([kpe, jnp.zeros_like(kpe)], axis=-1).astype(BF16)
    for h in range(MLA_HEADS):
        k_ref[:, h * QK_PAD:h * QK_PAD + LANES] = kn[:, h * NOPE_DIM:(h + 1) * NOPE_DIM].astype(BF16)
        k_ref[:, h * QK_PAD + LANES:(h + 1) * QK_PAD] = kpe_pad


def _decompress(ckv2, kpe2, wuk, wuv):
    n = ckv2.shape[0]
    tm = min(TOKEN_TILE, n)
    return pl.pallas_call(
        _decompress_kernel,
        grid=(n // tm,),
        in_specs=[_row_spec(tm, ckv2.shape[1]), _row_spec(tm, ROPE_DIM), _const_spec(wuk.shape),
                  _const_spec(wuv.shape)],
        out_specs=[_row_spec(tm, MLA_HEADS * QK_PAD), _row_spec(tm, MLA_HEADS * V_DIM)],
        out_shape=[jax.ShapeDtypeStruct((n, MLA_HEADS * QK_PAD), BF16),
                   jax.ShapeDtypeStruct((n, MLA_HEADS * V_DIM), BF16)],
        compiler_params=_params("parallel"),
        name="decompress",
    )(ckv2, kpe2, wuk, wuv)


def _retention_kernel(q_ref, k_ref, v_ref, s0_ref, gnw_ref, y_ref, sfin_ref, state_ref, *, blk, nblk):
    c = pl.program_id(1)

    @pl.when(c == 0)
    def _():
        state_ref[...] = s0_ref[...]

    ri = lax.broadcasted_iota(jnp.int32, (blk, blk), 0)
    ci = lax.broadcasted_iota(jnp.int32, (blk, blk), 1)
    diff = (ri - ci).astype(F32)
    n = lax.broadcasted_iota(jnp.int32, (blk, 1), 0).astype(F32)
    for h in range(RET_HEADS):
        lg = math.log(1.0 - 2.0 ** (-5.0 - h))
        q = q_ref[:, h * RET_DK:(h + 1) * RET_DK]
        k = k_ref[:, h * RET_DK:(h + 1) * RET_DK]
        v = v_ref[:, h * RET_DV:(h + 1) * RET_DV]
        decay = jnp.where(diff >= 0, jnp.exp(lg * jnp.maximum(diff, 0.0)), 0.0)
        scores = _dot_nt(q, k) * decay
        inner = _dot(scores.astype(BF16), v)
        st = state_ref[h]
        qd = (q.astype(F32) * jnp.exp(lg * (n + 1.0))).astype(BF16)
        cross = _dot(qd, st.astype(BF16))
        kd = (k.astype(F32) * jnp.exp(lg * (blk - 1.0 - n))).astype(BF16)
        state_ref[h] = math.exp(lg * blk) * st + _dot_tn(kd, v)
        y = inner + cross
        mu = jnp.mean(y, axis=-1, keepdims=True)
        yc = y - mu
        yn = yc * lax.rsqrt(jnp.mean(yc * yc, axis=-1, keepdims=True) + EPS)
        sl = slice(h * RET_DV, (h + 1) * RET_DV)
        y_ref[:, sl] = (yn * gnw_ref[:, sl]).astype(BF16)

    @pl.when(c == nblk - 1)
    def _():
        sfin_ref[...] = state_ref[...]


def _retention(rq, rk, rv, state0, gnw, blk):
    b, s, _ = rq.shape
    nblk = s // blk
    qk_w, v_w = RET_HEADS * RET_DK, RET_HEADS * RET_DV
    st_spec = pl.BlockSpec((None, RET_HEADS, RET_DK, RET_DV), lambda i, c: (i, 0, 0, 0))
    return pl.pallas_call(
        functools.partial(_retention_kernel, blk=blk, nblk=nblk),
        grid=(b, nblk),
        in_specs=[pl.BlockSpec((None, blk, qk_w), lambda i, c: (i, c, 0)),
                  pl.BlockSpec((None, blk, qk_w), lambda i, c: (i, c, 0)),
                  pl.BlockSpec((None, blk, v_w), lambda i, c: (i, c, 0)),
                  st_spec,
                  pl.BlockSpec((1, v_w), lambda i, c: (0, 0))],
        out_specs=[pl.BlockSpec((None, blk, v_w), lambda i, c: (i, c, 0)), st_spec],
        out_shape=[jax.ShapeDtypeStruct((b, s, v_w), BF16),
                   jax.ShapeDtypeStruct((b, RET_HEADS, RET_DK, RET_DV), F32)],
        scratch_shapes=[pltpu.VMEM((RET_HEADS, RET_DK, RET_DV), F32)],
        compiler_params=_params("parallel", "arbitrary"),
        name="retention",
    )(rq, rk, rv, state0, gnw)


_SOFTMAX_C = (NOPE_DIM + ROPE_DIM) ** -0.5 * math.log2(math.e)


def _attn_prompt_kernel(q_ref, k_ref, v_ref, o_ref, s_scr, m_scr, acc_scr, *, tq, tk):
    nq = q_ref.shape[0] // tq
    ones = jnp.ones((tk, V_DIM), BF16)

    def rows(ref, t, n):
        return ref[pl.ds(pl.multiple_of(t * n, n), n), :]

    def scores(qi, t):
        return _dot_nt(rows(q_ref, qi, tq), rows(k_ref, t, tk))

    def consume(s, t, mask=None):
        if mask is not None:
            s = jnp.where(mask, s, NEG_BIG)
        m_old = m_scr[...]
        m_new = jnp.max(jnp.concatenate([s, m_old], axis=1), axis=-1, keepdims=True)
        alpha = jnp.exp2((m_old - m_new) * _SOFTMAX_C)
        p = jnp.exp2((s - m_new) * _SOFTMAX_C).astype(BF16)
        v_ext = jnp.concatenate([rows(v_ref, t, tk), ones], axis=1)
        m_scr[...] = jnp.broadcast_to(m_new, m_scr.shape)
        acc_scr[...] = jnp.concatenate([alpha, alpha], axis=1) * acc_scr[...] + _dot(p, v_ext)

    def restart():
        m_scr[...] = jnp.full_like(m_scr, NEG_BIG)
        acc_scr[...] = jnp.zeros_like(acc_scr)

    row = lax.broadcasted_iota(jnp.int32, (tq, tk), 0) // CHUNK
    col = lax.broadcasted_iota(jnp.int32, (tq, tk), 1) // CHUNK
    diag_masks = (col <= row, col + tk // CHUNK <= row)

    restart()
    s_scr[...] = scores(0, 0)

    def q_tile(qi, _):
        def pair(j, _):
            s_odd = scores(qi, 2 * j + 1)
            consume(s_scr[...], 2 * j)
            s_scr[...] = scores(qi, 2 * j + 2)
            consume(s_odd, 2 * j + 1)
            return 0

        lax.fori_loop(0, qi, pair, 0)
        s_odd = scores(qi, 2 * qi + 1)
        consume(s_scr[...], 2 * qi, diag_masks[0])
        s_scr[...] = scores(jnp.minimum(qi + 1, nq - 1), 0)
        consume(s_odd, 2 * qi + 1, diag_masks[1])
        acc = acc_scr[...]
        o_ref[pl.ds(pl.multiple_of(qi * tq, tq), tq), :] = (acc[:, :V_DIM] / acc[:, V_DIM:]).astype(BF16)
        restart()
        return 0

    lax.fori_loop(0, nq, q_tile, 0)


def _attn_prompt(q, k, v):
    b, s, _ = q.shape
    tq, tk = ATTN_Q_TILE, ATTN_Q_TILE // 2
    return pl.pallas_call(
        functools.partial(_attn_prompt_kernel, tq=tq, tk=tk),
        grid=(b, MLA_HEADS),
        in_specs=[pl.BlockSpec((None, s, QK_PAD), lambda i, h: (i, 0, h)),
                  pl.BlockSpec((None, s, QK_PAD), lambda i, h: (i, 0, h)),
                  pl.BlockSpec((None, s, V_DIM), lambda i, h: (i, 0, h))],
        out_specs=pl.BlockSpec((None, s, V_DIM), lambda i, h: (i, 0, h)),
        out_shape=jax.ShapeDtypeStruct((b, s, MLA_HEADS * V_DIM), BF16),
        scratch_shapes=[pltpu.VMEM((tq, tk), F32), pltpu.VMEM((tq, LANES), F32),
                        pltpu.VMEM((tq, 2 * V_DIM), F32)],
        compiler_params=_params("parallel", "parallel"),
        name="attn_prompt",
    )(q, k, v)


def _attn_sample_kernel(q_ref, k_ref, v_ref, o_ref):
    s = _dot_nt(q_ref[...], k_ref[...])
    m = jnp.max(s, axis=-1, keepdims=True)
    p = jnp.exp2((s - m) * _SOFTMAX_C)
    l = jnp.sum(p, axis=-1, keepdims=True)
    o_ref[...] = (_dot(p.astype(BF16), v_ref[...]) / l).astype(BF16)


def _attn_sample(q, k, v):
    b, s, _ = q.shape
    t = k.shape[1]
    return pl.pallas_call(
        _attn_sample_kernel,
        grid=(b, MLA_HEADS),
        in_specs=[pl.BlockSpec((None, s, QK_PAD), lambda i, h: (i, 0, h)),
                  pl.BlockSpec((None, t, QK_PAD), lambda i, h: (i, 0, h)),
                  pl.BlockSpec((None, t, V_DIM), lambda i, h: (i, 0, h))],
        out_specs=pl.BlockSpec((None, s, V_DIM), lambda i, h: (i, 0, h)),
        out_shape=jax.ShapeDtypeStruct((b, s, MLA_HEADS * V_DIM), BF16),
        compiler_params=_params("parallel", "parallel"),
        name="attn_sample",
    )(q, k, v)


def _mix_kernel(x_ref, ret_ref, mla_ref, nw_ref, wg_ref, wro_ref, wmo_ref, wo_ref, h_ref):
    d = x_ref.shape[1]
    v_w = ret_ref.shape[1]
    x = x_ref[...]
    xn = _rms(x, nw_ref[...]).astype(BF16)
    rg = _dot(xn, wg_ref[:, 0:v_w])
    ret_b = _dot((ret_ref[...].astype(F32) * _silu(rg)).astype(BF16), wro_ref[...])
    ga = _dot(xn, wg_ref[:, v_w:v_w + d])
    merged = _sigmoid(ga) * ret_b
    mla_b = _dot(mla_ref[...], wmo_ref[...])
    gb = _dot(xn, wg_ref[:, v_w + d:v_w + 2 * d])
    merged = merged + _sigmoid(gb) * mla_b
    h_ref[...] = x + _dot(merged.astype(BF16), wo_ref[...])


def _mix(x2, ret2, mla2, nw, wg, wro, wmo, wo):
    n, d = x2.shape
    tm = min(TOKEN_TILE, n)
    return pl.pallas_call(
        _mix_kernel,
        grid=(n // tm,),
        in_specs=[_row_spec(tm, d), _row_spec(tm, ret2.shape[1]), _row_spec(tm, mla2.shape[1]),
                  _const_spec(nw.shape), _const_spec(wg.shape), _const_spec(wro.shape),
                  _const_spec(wmo.shape), _const_spec(wo.shape)],
        out_specs=_row_spec(tm, d),
        out_shape=jax.ShapeDtypeStruct((n, d), F32),
        compiler_params=_params("parallel"),
        name="mix",
    )(x2, ret2, mla2, nw, wg, wro, wmo, wo)


def _ffn_kernel(h_ref, nw_ref, wgu_ref, wd_ref, fw_ref, y_ref, *, final_norm):
    h = h_ref[...]
    hn = _rms(h, nw_ref[...]).astype(BF16)
    acc = h
    for c in range(wd_ref.shape[0] // FF_CHUNK):
        gu = _dot(hn, wgu_ref[:, 2 * c * FF_CHUNK:2 * (c + 1) * FF_CHUNK])
        act = (_silu(gu[:, :FF_CHUNK]) * gu[:, FF_CHUNK:]).astype(BF16)
        acc = acc + _dot(act, wd_ref[c * FF_CHUNK:(c + 1) * FF_CHUNK, :])
    y_ref[...] = _rms(acc, fw_ref[...]) if final_norm else acc


def _ffn(h2, nw, wgu, wd, fw, final_norm):
    n, d = h2.shape
    tm = min(TOKEN_TILE, n)
    return pl.pallas_call(
        functools.partial(_ffn_kernel, final_norm=final_norm),
        grid=(n // tm,),
        in_specs=[_row_spec(tm, d), _const_spec(nw.shape), _const_spec(wgu.shape), _const_spec(wd.shape),
                  _const_spec(fw.shape)],
        out_specs=_row_spec(tm, d),
        out_shape=jax.ShapeDtypeStruct((n, d), F32),
        compiler_params=_params("parallel"),
        name="ffn",
    )(h2, nw, wgu, wd, fw)


def _rope_tables(pos):
    p = pos.astype(F32)[:, None]

    def cs(d):
        inv = ROPE_BASE ** (-jnp.arange(0, d, 2, dtype=F32) / d)
        ang = p * inv[None, :]
        return jnp.cos(ang), jnp.sin(ang)

    cr, sr = cs(RET_DK)
    cm, sm = cs(ROPE_DIM)
    z = jnp.zeros_like(sm)
    return (jnp.concatenate([cr, cr], -1), jnp.concatenate([-sr, sr], -1),
            jnp.concatenate([cm] * 4, -1), jnp.concatenate([-sm, z, -sm, z], -1),
            jnp.concatenate([z, sm, z, sm], -1))


def _prep_layer_weights(w_in, w_uq, w_ukv, w_ret_o, w_mla_o, w_out, w_gate_up, w_down):
    d = w_in.shape[0]
    qk_w, v_w = RET_HEADS * RET_DK, RET_HEADS * RET_DV
    q_lora = w_uq.shape[0]
    kv_lora = w_ukv.shape[0]
    o_rg = 2 * qk_w + v_w
    o_cq = o_rg + v_w
    o_kpe = o_cq + q_lora + kv_lora
    o_ga = o_kpe + ROPE_DIM
    wa = jnp.concatenate([w_in[:, :o_rg], w_in[:, o_cq:o_ga], jnp.zeros((d, LANES - ROPE_DIM), w_in.dtype)],
                         axis=1).astype(BF16)
    wg = jnp.concatenate([w_in[:, o_rg:o_cq], w_in[:, o_ga:]], axis=1).astype(BF16)
    head_w = NOPE_DIM + ROPE_DIM
    wuq = jnp.pad(w_uq.reshape(q_lora, MLA_HEADS, head_w), ((0, 0), (0, 0), (0, QK_PAD - head_w)))
    wuq = wuq.reshape(q_lora, MLA_HEADS * QK_PAD).astype(BF16)
    wkv = w_ukv.reshape(kv_lora, MLA_HEADS, NOPE_DIM + V_DIM)
    wuk = wkv[:, :, :NOPE_DIM].reshape(kv_lora, MLA_HEADS * NOPE_DIM).astype(BF16)
    wuv = wkv[:, :, NOPE_DIM:].reshape(kv_lora, MLA_HEADS * V_DIM).astype(BF16)
    d_ff = w_down.shape[0]
    nch = d_ff // FF_CHUNK
    wgu = w_gate_up.reshape(d, 2, nch, FF_CHUNK).transpose(0, 2, 1, 3).reshape(d, 2 * d_ff).astype(BF16)
    return dict(wa=wa, wg=wg, wuq=wuq, wuk=wuk, wuv=wuv, wro=w_ret_o.astype(BF16), wmo=w_mla_o.astype(BF16),
                wo=w_out.astype(BF16), wgu=wgu, wd=w_down.astype(BF16))


def _layer(x, tabs, tab_tiles, lw, norms, state0, cache, fw, final_norm):
    b, s, d = x.shape
    n = b * s
    x2 = x.reshape(n, d)
    nmw, qnw, kvnw, gnw, nfw = norms
    rq, rk, rv, q, ckv, kpe = _inproj(x2, tabs, nmw, lw["wa"], qnw, kvnw, lw["wuq"], tab_tiles=tab_tiles)
    kv_lora = ckv.shape[1]
    ckv3 = ckv.reshape(b, s, kv_lora)
    kpe3 = kpe.reshape(b, s, ROPE_DIM)
    if cache is None:
        keys_ckv, keys_kpe = ckv3, kpe3
    else:
        keys_ckv = jnp.concatenate([cache[0].astype(F32), ckv3], axis=1)
        keys_kpe = jnp.concatenate([cache[1].astype(F32), kpe3], axis=1)
    t = keys_ckv.shape[1]
    kk, vv = _decompress(keys_ckv.reshape(b * t, kv_lora), keys_kpe.reshape(b * t, ROPE_DIM), lw["wuk"], lw["wuv"])
    kk = kk.reshape(b, t, MLA_HEADS * QK_PAD)
    vv = vv.reshape(b, t, MLA_HEADS * V_DIM)
    q3 = q.reshape(b, s, MLA_HEADS * QK_PAD)
    mla = _attn_prompt(q3, kk, vv) if cache is None else _attn_sample(q3, kk, vv)
    blk = RET_BLOCK if s % RET_BLOCK == 0 else CHUNK
    ret, s_fin = _retention(rq.reshape(b, s, -1), rk.reshape(b, s, -1), rv.reshape(b, s, -1), state0, gnw, blk)
    h2 = _mix(x2, ret.reshape(n, -1), mla.reshape(n, -1), nmw, lw["wg"], lw["wro"], lw["wmo"], lw["wo"])
    y2 = _ffn(h2, nfw, lw["wgu"], lw["wd"], fw, final_norm)
    return y2.reshape(b, s, d), (ckv3, kpe3, s_fin)


def kernel(x_prompt, x_sample, cache_ckv, cache_kpe, state_ret, norm_mix_w, w_in, q_norm_w, w_uq, kv_norm_w, w_ukv, ret_gn_w, w_ret_o, w_mla_o, w_out, norm_ffn_w, w_gate_up, w_down, norm_final_w):
    depth = w_in.shape[0]
    bp, sp, _ = x_prompt.shape
    bs, ss, _ = x_sample.shape
    past = cache_ckv.shape[2]
    assert sp % TOKEN_TILE == 0 and sp % ATTN_Q_TILE == 0 and (bs * ss) % min(TOKEN_TILE, bs * ss) == 0
    assert ss == CHUNK, "the sample group is one streaming chunk"

    tabs_p = _rope_tables(jnp.arange(sp))
    tabs_s = tuple(jnp.tile(t, (bs, 1)) for t in _rope_tables(past + jnp.arange(ss)))
    fw = norm_final_w.reshape(1, -1)
    state0_p = jnp.zeros((bp, RET_HEADS, RET_DK, RET_DV), F32)

    hp, hs = x_prompt, x_sample
    outs = [[] for _ in range(6)]
    for l in range(depth):
        lw = _prep_layer_weights(w_in[l], w_uq[l], w_ukv[l], w_ret_o[l], w_mla_o[l], w_out[l], w_gate_up[l],
                                 w_down[l])
        norms = (norm_mix_w[l].reshape(1, -1), q_norm_w[l].reshape(1, -1), kv_norm_w[l].reshape(1, -1),
                 ret_gn_w[l].reshape(1, -1), norm_ffn_w[l].reshape(1, -1))
        final = l == depth - 1
        hp, (a, b_, c) = _layer(hp, tabs_p, sp // TOKEN_TILE, lw, norms, state0_p, None, fw, final)
        hs, (d_, e, f) = _layer(hs, tabs_s, 1, lw, norms, state_ret[l].astype(F32),
                                (cache_ckv[l], cache_kpe[l]), fw, final)
        for lst, val in zip(outs, (a, b_, c, d_, e, f)):
            lst.append(val)
    return (hp, hs) + tuple(jnp.stack(o) for o in outs)
```

```python
import functools
import math

import jax
import jax.numpy as jnp
import numpy as np
from jax import lax
from jax.experimental import pallas as pl
from jax.experimental.pallas import tpu as pltpu

F32 = jnp.float32
BF16 = jnp.bfloat16

CHUNK = 64
RET_HEADS = 4
RET_DK = 128
RET_DV = 256
MLA_HEADS = 8
NOPE_DIM = 128
ROPE_DIM = 64
V_DIM = 128
ROPE_BASE = 10000.0
EPS = 1e-6

LANES = 128
QK_PAD = 2 * LANES
V7X_VMEM_BYTES = 64 * 2**20
VMEM_LIMIT = V7X_VMEM_BYTES * 7 // 8
TOKEN_TILE = 512
RET_BLOCK = 256
ATTN_Q_TILE = 1024
FF_CHUNK = 256
NEG_BIG = -1e30

_dot = functools.partial(jnp.dot, preferred_element_type=F32)


def _dot_nt(a, b):
    return lax.dot_general(a, b, (((1,), (1,)), ((), ())), preferred_element_type=F32)


def _dot_tn(a, b):
    return lax.dot_general(a, b, (((0,), (0,)), ((), ())), preferred_element_type=F32)


def _rms(x, g):
    ms = jnp.mean(x * x, axis=-1, keepdims=True)
    return x * lax.rsqrt(ms + EPS) * g


def _sigmoid(x):
    return 1.0 / (1.0 + jnp.exp(-x))


def _silu(x):
    return x * _sigmoid(x)


def _params(*sem):
    return pltpu.CompilerParams(dimension_semantics=sem, vmem_limit_bytes=VMEM_LIMIT)


def _const_spec(shape):
    return pl.BlockSpec(shape, lambda *_: (0,) * len(shape), pipeline_mode=pl.Buffered(1))


def _row_spec(tm, width):
    return pl.BlockSpec((tm, width), lambda i: (i, 0))


def _rope_full(x, cos2, sin2):
    return x * cos2 + pltpu.roll(x, LANES // 2, 1) * sin2


def _rope_half(x, c, sa, sb):
    q = ROPE_DIM // 2
    return x * c + pltpu.roll(x, LANES - q, 1) * sa + pltpu.roll(x, q, 1) * sb


def _inproj_kernel(x_ref, nw_ref, wa_ref, qnw_ref, kvnw_ref, wuq_ref,
                   cr_ref, sr_ref, cm_ref, sm_ref,
                   rq_ref, rk_ref, rv_ref, q_ref, ckv_ref, kpe_ref):
    qk_w = RET_HEADS * RET_DK
    v_w = RET_HEADS * RET_DV
    q_lora = wuq_ref.shape[0]
    kv_lora = ckv_ref.shape[1]
    o_rk, o_rv, o_cq = qk_w, 2 * qk_w, 2 * qk_w + v_w
    o_ckv = o_cq + q_lora
    o_kpe = o_ckv + kv_lora

    xn = _rms(x_ref[...], nw_ref[...]).astype(BF16)
    c, s = cr_ref[...], sr_ref[...]
    cr = jnp.concatenate([c, c], axis=1)
    sr = jnp.concatenate([-s, s], axis=1)
    c, s = cm_ref[...], sm_ref[...]
    z = jnp.zeros_like(s)
    cm = jnp.concatenate([c, c, c, c], axis=1)
    sa = jnp.concatenate([-s, z, -s, z], axis=1)
    sb = jnp.concatenate([z, s, z, s], axis=1)

    zq = _dot(xn, wa_ref[:, 0:qk_w])
    zk = _dot(xn, wa_ref[:, o_rk:o_rk + qk_w])
    k_scale = RET_DK ** -0.5
    for h in range(RET_HEADS):
        sl = slice(h * RET_DK, (h + 1) * RET_DK)
        rq_ref[:, sl] = _rope_full(zq[:, sl], cr, sr).astype(BF16)
        rk_ref[:, sl] = (_rope_full(zk[:, sl], cr, sr) * k_scale).astype(BF16)
    rv_ref[...] = _dot(xn, wa_ref[:, o_rv:o_rv + v_w]).astype(BF16)

    cq = _dot(xn, wa_ref[:, o_cq:o_cq + q_lora])
    cqn = _rms(cq, qnw_ref[...]).astype(BF16)
    for h in range(MLA_HEADS):
        qh = _dot(cqn, wuq_ref[:, h * QK_PAD:(h + 1) * QK_PAD])
        q_ref[:, h * QK_PAD:h * QK_PAD + LANES] = qh[:, :LANES].astype(BF16)
        q_ref[:, h * QK_PAD + LANES:(h + 1) * QK_PAD] = _rope_half(qh[:, LANES:], cm, sa, sb).astype(BF16)

    ckv = _dot(xn, wa_ref[:, o_ckv:o_ckv + kv_lora])
    ckv_ref[...] = _rms(ckv, kvnw_ref[...])
    kp = _dot(xn, wa_ref[:, o_kpe:o_kpe + LANES])
    kpe_ref[...] = _rope_half(kp, cm, sa, sb)[:, :ROPE_DIM]


def _inproj(x2, tabs, nw, wa, qnw, kvnw, wuq, *, tab_tiles):
    n, d = x2.shape
    tm = min(TOKEN_TILE, n)
    qk_w, v_w = RET_HEADS * RET_DK, RET_HEADS * RET_DV
    kv_lora = kvnw.shape[1]
    tab_specs = [pl.BlockSpec((tm, t.shape[1]), lambda i: (i % tab_tiles, 0)) for t in tabs]
    return pl.pallas_call(
        _inproj_kernel,
        grid=(n // tm,),
        in_specs=[_row_spec(tm, d), _const_spec(nw.shape), _const_spec(wa.shape), _const_spec(qnw.shape),
                  _const_spec(kvnw.shape), _const_spec(wuq.shape)] + tab_specs,
        out_specs=[_row_spec(tm, qk_w), _row_spec(tm, qk_w), _row_spec(tm, v_w),
                   _row_spec(tm, MLA_HEADS * QK_PAD), _row_spec(tm, kv_lora), _row_spec(tm, ROPE_DIM)],
        out_shape=[jax.ShapeDtypeStruct((n, qk_w), BF16), jax.ShapeDtypeStruct((n, qk_w), BF16),
                   jax.ShapeDtypeStruct((n, v_w), BF16), jax.ShapeDtypeStruct((n, MLA_HEADS * QK_PAD), BF16),
                   jax.ShapeDtypeStruct((n, kv_lora), F32), jax.ShapeDtypeStruct((n, ROPE_DIM), F32)],
        compiler_params=_params("parallel"),
        name="inproj",
    )(x2, nw, wa, qnw, kvnw, wuq, *tabs)


def _decompress_kernel(ckv_ref, kpe_ref, wuk_ref, wuv_ref, k_ref, v_ref):
    c = ckv_ref[...].astype(BF16)
    kn = _dot(c, wuk_ref[...])
    v_ref[...] = _dot(c, wuv_ref[...]).astype(BF16)
    kpe = kpe_ref[...]
    kpe_pad = jnp.concatenate([kpe, jnp.zeros_like(kpe)], axis=-1).astype(BF16)
    for h in range(MLA_HEADS):
        k_ref[:, h * QK_PAD:h * QK_PAD + LANES] = kn[:, h * NOPE_DIM:(h + 1) * NOPE_DIM].astype(BF16)
        k_ref[:, h * QK_PAD + LANES:(h + 1) * QK_PAD] = kpe_pad


def _decompress(ckv2, kpe2, wuk, wuv):
    n = ckv2.shape[0]
    tm = min(TOKEN_TILE, n)
    return pl.pallas_call(
        _decompress_kernel,
        grid=(n // tm,),
        in_specs=[_row_spec(tm, ckv2.shape[1]), _row_spec(tm, ROPE_DIM), _const_spec(wuk.shape),
                  _const_spec(wuv.shape)],
        out_specs=[_row_spec(tm, MLA_HEADS * QK_PAD), _row_spec(tm, MLA_HEADS * V_DIM)],
        out_shape=[jax.ShapeDtypeStruct((n, MLA_HEADS * QK_PAD), BF16),
                   jax.ShapeDtypeStruct((n, MLA_HEADS * V_DIM), BF16)],
        compiler_params=_params("parallel"),
        name="decompress",
    )(ckv2, kpe2, wuk, wuv)


def _retention_kernel(q_ref, k_ref, v_ref, s0_ref, gnw_ref, y_ref, sfin_ref, state_ref, *, blk, nblk):
    c = pl.program_id(1)

    @pl.when(c == 0)
    def _():
        state_ref[...] = s0_ref[...]

    ri = lax.broadcasted_iota(jnp.int32, (blk, blk), 0)
    ci = lax.broadcasted_iota(jnp.int32, (blk, blk), 1)
    diff = (ri - ci).astype(F32)
    n = lax.broadcasted_iota(jnp.int32, (blk, 1), 0).astype(F32)
    for h in range(RET_HEADS):
        lg = math.log(1.0 - 2.0 ** (-5.0 - h))
        q = q_ref[:, h * RET_DK:(h + 1) * RET_DK]
        k = k_ref[:, h * RET_DK:(h + 1) * RET_DK]
        v = v_ref[:, h * RET_DV:(h + 1) * RET_DV]
        decay = jnp.where(diff >= 0, jnp.exp(lg * jnp.maximum(diff, 0.0)), 0.0)
        scores = _dot_nt(q, k) * decay
        inner = _dot(scores.astype(BF16), v)
        st = state_ref[h]
        qd = (q.astype(F32) * jnp.exp(lg * (n + 1.0))).astype(BF16)
        cross = _dot(qd, st.astype(BF16))
        kd = (k.astype(F32) * jnp.exp(lg * (blk - 1.0 - n))).astype(BF16)
        state_ref[h] = math.exp(lg * blk) * st + _dot_tn(kd, v)
        y = inner + cross
        mu = jnp.mean(y, axis=-1, keepdims=True)
        yc = y - mu
        yn = yc * lax.rsqrt(jnp.mean(yc * yc, axis=-1, keepdims=True) + EPS)
        sl = slice(h * RET_DV, (h + 1) * RET_DV)
        y_ref[:, sl] = (yn * gnw_ref[:, sl]).astype(BF16)

    @pl.when(c == nblk - 1)
    def _():
        sfin_ref[...] = state_ref[...]


def _retention(rq, rk, rv, state0, gnw, blk):
    b, s, _ = rq.shape
    nblk = s // blk
    qk_w, v_w = RET_HEADS * RET_DK, RET_HEADS * RET_DV
    st_spec = pl.BlockSpec((None, RET_HEADS, RET_DK, RET_DV), lambda i, c: (i, 0, 0, 0))
    return pl.pallas_call(
        functools.partial(_retention_kernel, blk=blk, nblk=nblk),
        grid=(b, nblk),
        in_specs=[pl.BlockSpec((None, blk, qk_w), lambda i, c: (i, c, 0)),
                  pl.BlockSpec((None, blk, qk_w), lambda i, c: (i, c, 0)),
                  pl.BlockSpec((None, blk, v_w), lambda i, c: (i, c, 0)),
                  st_spec,
                  pl.BlockSpec((1, v_w), lambda i, c: (0, 0))],
        out_specs=[pl.BlockSpec((None, blk, v_w), lambda i, c: (i, c, 0)), st_spec],
        out_shape=[jax.ShapeDtypeStruct((b, s, v_w), BF16),
                   jax.ShapeDtypeStruct((b, RET_HEADS, RET_DK, RET_DV), F32)],
        scratch_shapes=[pltpu.VMEM((RET_HEADS, RET_DK, RET_DV), F32)],
        compiler_params=_params("parallel", "arbitrary"),
        name="retention",
    )(rq, rk, rv, state0, gnw)


_SOFTMAX_C = (NOPE_DIM + ROPE_DIM) ** -0.5 * math.log2(math.e)


def _attn_prompt_kernel(q_ref, k_ref, v_ref, o_ref, s_scr, m_scr, acc_scr, *, tq, tk):
    nq = q_ref.shape[0] // tq
    ones = jnp.ones((tk, V_DIM), BF16)

    def rows(ref, t, n):
        return ref[pl.ds(pl.multiple_of(t * n, n), n), :]

    def scores(qi, t):
        return _dot_nt(rows(q_ref, qi, tq), rows(k_ref, t, tk))

    def consume(s, t, mask=None):
        if mask is not None:
            s = jnp.where(mask, s, NEG_BIG)
        m_old = m_scr[...]
        m_new = jnp.max(jnp.concatenate([s, m_old], axis=1), axis=-1, keepdims=True)
        alpha = jnp.exp2((m_old - m_new) * _SOFTMAX_C)
        p = jnp.exp2((s - m_new) * _SOFTMAX_C).astype(BF16)
        v_ext = jnp.concatenate([rows(v_ref, t, tk), ones], axis=1)
        m_scr[...] = jnp.broadcast_to(m_new, m_scr.shape)
        acc_scr[...] = jnp.concatenate([alpha, alpha], axis=1) * acc_scr[...] + _dot(p, v_ext)

    def restart():
        m_scr[...] = jnp.full_like(m_scr, NEG_BIG)
        acc_scr[...] = jnp.zeros_like(acc_scr)

    row = lax.broadcasted_iota(jnp.int32, (tq, tk), 0) // CHUNK
    col = lax.broadcasted_iota(jnp.int32, (tq, tk), 1) // CHUNK
    diag_masks = (col <= row, col + tk // CHUNK <= row)

    restart()
    s_scr[...] = scores(0, 0)

    def q_tile(qi, _):
        def pair(j, _):
            s_odd = scores(qi, 2 * j + 1)
            consume(s_scr[...], 2 * j)
            s_scr[...] = scores(qi, 2 * j + 2)
            consume(s_odd, 2 * j + 1)
            return 0

        lax.fori_loop(0, qi, pair, 0)
        s_odd = scores(qi, 2 * qi + 1)
        consume(s_scr[...], 2 * qi, diag_masks[0])
        s_scr[...] = scores(jnp.minimum(qi + 1, nq - 1), 0)
        consume(s_odd, 2 * qi + 1, diag_masks[1])
        acc = acc_scr[...]
        o_ref[pl.ds(pl.multiple_of(qi * tq, tq), tq), :] = (acc[:, :V_DIM] / acc[:, V_DIM:]).astype(BF16)
        restart()
        return 0

    lax.fori_loop(0, nq, q_tile, 0)


def _attn_prompt(q, k, v):
    b, s, _ = q.shape
    tq, tk = ATTN_Q_TILE, ATTN_Q_TILE // 2
    return pl.pallas_call(
        functools.partial(_attn_prompt_kernel, tq=tq, tk=tk),
        grid=(b, MLA_HEADS),
        in_specs=[pl.BlockSpec((None, s, QK_PAD), lambda i, h: (i, 0, h)),
                  pl.BlockSpec((None, s, QK_PAD), lambda i, h: (i, 0, h)),
                  pl.BlockSpec((None, s, V_DIM), lambda i, h: (i, 0, h))],
        out_specs=pl.BlockSpec((None, s, V_DIM), lambda i, h: (i, 0, h)),
        out_shape=jax.ShapeDtypeStruct((b, s, MLA_HEADS * V_DIM), BF16),
        scratch_shapes=[pltpu.VMEM((tq, tk), F32), pltpu.VMEM((tq, LANES), F32),
                        pltpu.VMEM((tq, 2 * V_DIM), F32)],
        compiler_params=_params("parallel", "parallel"),
        name="attn_prompt",
    )(q, k, v)


def _attn_sample_kernel(q_ref, k_ref, v_ref, o_ref):
    s = _dot_nt(q_ref[...], k_ref[...])
    m = jnp.max(s, axis=-1, keepdims=True)
    p = jnp.exp2((s - m) * _SOFTMAX_C)
    l = jnp.sum(p, axis=-1, keepdims=True)
    o_ref[...] = (_dot(p.astype(BF16), v_ref[...]) / l).astype(BF16)


def _attn_sample(q, k, v):
    b, s, _ = q.shape
    t = k.shape[1]
    return pl.pallas_call(
        _attn_sample_kernel,
        grid=(b, MLA_HEADS),
        in_specs=[pl.BlockSpec((None, s, QK_PAD), lambda i, h: (i, 0, h)),
                  pl.BlockSpec((None, t, QK_PAD), lambda i, h: (i, 0, h)),
                  pl.BlockSpec((None, t, V_DIM), lambda i, h: (i, 0, h))],
        out_specs=pl.BlockSpec((None, s, V_DIM), lambda i, h: (i, 0, h)),
        out_shape=jax.ShapeDtypeStruct((b, s, MLA_HEADS * V_DIM), BF16),
        compiler_params=_params("parallel", "parallel"),
        name="attn_sample",
    )(q, k, v)


def _mix_kernel(x_ref, ret_ref, mla_ref, nw_ref, wg_ref, wro_ref, wmo_ref, wo_ref, h_ref):
    d = x_ref.shape[1]
    v_w = ret_ref.shape[1]
    x = x_ref[...]
    xn = _rms(x, nw_ref[...]).astype(BF16)
    rg = _dot(xn, wg_ref[:, 0:v_w])
    ret_b = _dot((ret_ref[...].astype(F32) * _silu(rg)).astype(BF16), wro_ref[...])
    ga = _dot(xn, wg_ref[:, v_w:v_w + d])
    merged = _sigmoid(ga) * ret_b
    mla_b = _dot(mla_ref[...], wmo_ref[...])
    gb = _dot(xn, wg_ref[:, v_w + d:v_w + 2 * d])
    merged = merged + _sigmoid(gb) * mla_b
    h_ref[...] = x + _dot(merged.astype(BF16), wo_ref[...])


def _mix(x2, ret2, mla2, nw, wg, wro, wmo, wo):
    n, d = x2.shape
    tm = min(TOKEN_TILE, n)
    return pl.pallas_call(
        _mix_kernel,
        grid=(n // tm,),
        in_specs=[_row_spec(tm, d), _row_spec(tm, ret2.shape[1]), _row_spec(tm, mla2.shape[1]),
                  _const_spec(nw.shape), _const_spec(wg.shape), _const_spec(wro.shape),
                  _const_spec(wmo.shape), _const_spec(wo.shape)],
        out_specs=_row_spec(tm, d),
        out_shape=jax.ShapeDtypeStruct((n, d), F32),
        compiler_params=_params("parallel"),
        name="mix",
    )(x2, ret2, mla2, nw, wg, wro, wmo, wo)


def _ffn_kernel(h_ref, nw_ref, wgu_ref, wd_ref, fw_ref, y_ref, *, final_norm):
    h = h_ref[...]
    hn = _rms(h, nw_ref[...]).astype(BF16)
    acc = h
    d_ff = wd_ref.shape[0]
    for c in range(d_ff // FF_CHUNK):
        sl = slice(c * FF_CHUNK, (c + 1) * FF_CHUNK)
        g = _dot(hn, wgu_ref[:, sl])
        u = _dot(hn, wgu_ref[:, d_ff + c * FF_CHUNK:d_ff + (c + 1) * FF_CHUNK])
        acc = acc + _dot((_silu(g) * u).astype(BF16), wd_ref[sl, :])
    y_ref[...] = _rms(acc, fw_ref[...]) if final_norm else acc


def _ffn(h2, nw, wgu, wd, fw, final_norm):
    n, d = h2.shape
    tm = min(TOKEN_TILE, n)
    return pl.pallas_call(
        functools.partial(_ffn_kernel, final_norm=final_norm),
        grid=(n // tm,),
        in_specs=[_row_spec(tm, d), _const_spec(nw.shape), _const_spec(wgu.shape), _const_spec(wd.shape),
                  _const_spec(fw.shape)],
        out_specs=_row_spec(tm, d),
        out_shape=jax.ShapeDtypeStruct((n, d), F32),
        compiler_params=_params("parallel"),
        name="ffn",
    )(h2, nw, wgu, wd, fw)


def _rope_tables(start, n, reps):
    pos = np.arange(start, start + n, dtype=np.float64)[:, None]

    def cs(d):
        inv = ROPE_BASE ** (-np.arange(0, d, 2, dtype=np.float64) / d)
        ang = pos * inv[None, :]
        return [np.tile(f(ang).astype(np.float32), (reps, 1)) for f in (np.cos, np.sin)]

    return tuple(jnp.asarray(t) for t in cs(RET_DK) + cs(ROPE_DIM))


def _prep_layer_weights(w_in, w_uq, w_ukv, w_ret_o, w_mla_o, w_out, w_gate_up, w_down):
    d = w_in.shape[0]
    qk_w, v_w = RET_HEADS * RET_DK, RET_HEADS * RET_DV
    q_lora = w_uq.shape[0]
    kv_lora = w_ukv.shape[0]
    o_rg = 2 * qk_w + v_w
    o_cq = o_rg + v_w
    o_kpe = o_cq + q_lora + kv_lora
    o_ga = o_kpe + ROPE_DIM
    wa = jnp.concatenate([w_in[:, :o_rg], w_in[:, o_cq:o_ga], jnp.zeros((d, LANES - ROPE_DIM), w_in.dtype)],
                         axis=1).astype(BF16)
    wg = jnp.concatenate([w_in[:, o_rg:o_cq], w_in[:, o_ga:]], axis=1).astype(BF16)
    head_w = NOPE_DIM + ROPE_DIM
    wuq = jnp.pad(w_uq.reshape(q_lora, MLA_HEADS, head_w), ((0, 0), (0, 0), (0, QK_PAD - head_w)))
    wuq = wuq.reshape(q_lora, MLA_HEADS * QK_PAD).astype(BF16)
    wkv = w_ukv.reshape(kv_lora, MLA_HEADS, NOPE_DIM + V_DIM)
    wuk = wkv[:, :, :NOPE_DIM].reshape(kv_lora, MLA_HEADS * NOPE_DIM).astype(BF16)
    wuv = wkv[:, :, NOPE_DIM:].reshape(kv_lora, MLA_HEADS * V_DIM).astype(BF16)
    assert w_down.shape[0] % FF_CHUNK == 0
    return dict(wa=wa, wg=wg, wuq=wuq, wuk=wuk, wuv=wuv, wro=w_ret_o.astype(BF16), wmo=w_mla_o.astype(BF16),
                wo=w_out.astype(BF16), wgu=w_gate_up.astype(BF16), wd=w_down.astype(BF16))


def _layer(x, tabs, tab_tiles, lw, norms, state0, cache, fw, final_norm):
    b, s, d = x.shape
    n = b * s
    x2 = x.reshape(n, d)
    nmw, qnw, kvnw, gnw, nfw = norms
    rq, rk, rv, q, ckv, kpe = _inproj(x2, tabs, nmw, lw["wa"], qnw, kvnw, lw["wuq"], tab_tiles=tab_tiles)
    kv_lora = ckv.shape[1]
    ckv3 = ckv.reshape(b, s, kv_lora)
    kpe3 = kpe.reshape(b, s, ROPE_DIM)
    if cache is None:
        keys_ckv, keys_kpe = ckv3, kpe3
    else:
        keys_ckv = jnp.concatenate([cache[0].astype(F32), ckv3], axis=1)
        keys_kpe = jnp.concatenate([cache[1].astype(F32), kpe3], axis=1)
    t = keys_ckv.shape[1]
    kk, vv = _decompress(keys_ckv.reshape(b * t, kv_lora), keys_kpe.reshape(b * t, ROPE_DIM), lw["wuk"], lw["wuv"])
    kk = kk.reshape(b, t, MLA_HEADS * QK_PAD)
    vv = vv.reshape(b, t, MLA_HEADS * V_DIM)
    q3 = q.reshape(b, s, MLA_HEADS * QK_PAD)
    mla = _attn_prompt(q3, kk, vv) if cache is None else _attn_sample(q3, kk, vv)
    blk = RET_BLOCK if s % RET_BLOCK == 0 else CHUNK
    ret, s_fin = _retention(rq.reshape(b, s, -1), rk.reshape(b, s, -1), rv.reshape(b, s, -1), state0, gnw, blk)
    h2 = _mix(x2, ret.reshape(n, -1), mla.reshape(n, -1), nmw, lw["wg"], lw["wro"], lw["wmo"], lw["wo"])
    y2 = _ffn(h2, nfw, lw["wgu"], lw["wd"], fw, final_norm)
    return y2.reshape(b, s, d), (ckv3, kpe3, s_fin)


def kernel(x_prompt, x_sample, cache_ckv, cache_kpe, state_ret, norm_mix_w, w_in, q_norm_w, w_uq, kv_norm_w, w_ukv, ret_gn_w, w_ret_o, w_mla_o, w_out, norm_ffn_w, w_gate_up, w_down, norm_final_w):
    depth = w_in.shape[0]
    bp, sp, _ = x_prompt.shape
    bs, ss, _ = x_sample.shape
    past = cache_ckv.shape[2]
    assert sp % TOKEN_TILE == 0 and sp % ATTN_Q_TILE == 0 and (bs * ss) % min(TOKEN_TILE, bs * ss) == 0
    assert ss == CHUNK, "the sample group is one streaming chunk"

    tabs_p = _rope_tables(0, sp, 1)
    tabs_s = _rope_tables(past, ss, bs)
    fw = norm_final_w.reshape(1, -1)
    state0_p = jnp.zeros((bp, RET_HEADS, RET_DK, RET_DV), F32)

    hp, hs = x_prompt, x_sample
    outs = [[] for _ in range(6)]
    for l in range(depth):
        lw = _prep_layer_weights(w_in[l], w_uq[l], w_ukv[l], w_ret_o[l], w_mla_o[l], w_out[l], w_gate_up[l],
                                 w_down[l])
        norms = (norm_mix_w[l].reshape(1, -1), q_norm_w[l].reshape(1, -1), kv_norm_w[l].reshape(1, -1),
                 ret_gn_w[l].reshape(1, -1), norm_ffn_w[l].reshape(1, -1))
        final = l == depth - 1
        hp, (a, b_, c) = _layer(hp, tabs_p, sp // TOKEN_TILE, lw, norms, state0_p, None, fw, final)
        hs, (d_, e, f) = _layer(hs, tabs_s, 1, lw, norms, state_ret[l].astype(F32),
                                (cache_ckv[l], cache_kpe[l]), fw, final)
        for lst, val in zip(outs, (a, b_, c, d_, e, f)):
            lst.append(val)
    return (hp, hs) + tuple(o[0][None] if depth == 1 else jnp.stack(o) for o in outs)
```

```python
import functools
import math

import jax
import jax.numpy as jnp
import numpy as np
from jax import lax
from jax.experimental import pallas as pl
from jax.experimental.pallas import tpu as pltpu

F32 = jnp.float32
BF16 = jnp.bfloat16

CHUNK = 64
RET_HEADS = 4
RET_DK = 128
RET_DV = 256
MLA_HEADS = 8
NOPE_DIM = 128
ROPE_DIM = 64
V_DIM = 128
ROPE_BASE = 10000.0
EPS = 1e-6

LANES = 128
QK_PAD = 2 * LANES
V7X_VMEM_BYTES = 64 * 2**20
VMEM_LIMIT = V7X_VMEM_BYTES * 7 // 8
TOKEN_TILE = 512
RET_BLOCK = 256
ATTN_Q_TILE = 1024
FF_CHUNK = 256
NEG_BIG = -1e30

_dot = functools.partial(jnp.dot, preferred_element_type=F32)


def _dot_nt(a, b):
    return lax.dot_general(a, b, (((1,), (1,)), ((), ())), preferred_element_type=F32)


def _dot_tn(a, b):
    return lax.dot_general(a, b, (((0,), (0,)), ((), ())), preferred_element_type=F32)


def _rms(x, g):
    ms = jnp.mean(x * x, axis=-1, keepdims=True)
    return x * lax.rsqrt(ms + EPS) * g


def _sigmoid(x):
    return 1.0 / (1.0 + jnp.exp(-x))


def _silu(x):
    return x * _sigmoid(x)


def _params(*sem):
    return pltpu.CompilerParams(dimension_semantics=sem, vmem_limit_bytes=VMEM_LIMIT)


def _const_spec(shape):
    return pl.BlockSpec(shape, lambda *_: (0,) * len(shape), pipeline_mode=pl.Buffered(1))


def _row_spec(tm, width):
    return pl.BlockSpec((tm, width), lambda i: (i, 0))


def _rope_full(x, cos2, sin2):
    return x * cos2 + pltpu.roll(x, LANES // 2, 1) * sin2


def _rope_half(x, c, sa, sb):
    q = ROPE_DIM // 2
    return x * c + pltpu.roll(x, LANES - q, 1) * sa + pltpu.roll(x, q, 1) * sb


def _inproj_kernel(x_ref, nw_ref, wa_ref, qnw_ref, kvnw_ref, wuq_ref, cr_ref, sr_ref, cm_ref, sm_ref, *rest,
                   with_kv):
    if with_kv:
        wuk_ref, wuv_ref, rq_ref, rk_ref, rv_ref, q_ref, ckv_ref, kpe_ref, kn_ref, kp_ref, v_ref = rest
    else:
        rq_ref, rk_ref, rv_ref, q_ref, ckv_ref, kpe_ref = rest
    qk_w = RET_HEADS * RET_DK
    v_w = RET_HEADS * RET_DV
    q_lora = wuq_ref.shape[0]
    kv_lora = ckv_ref.shape[1]
    o_rk, o_rv, o_cq = qk_w, 2 * qk_w, 2 * qk_w + v_w
    o_ckv = o_cq + q_lora
    o_kpe = o_ckv + kv_lora

    xn = _rms(x_ref[...], nw_ref[...]).astype(BF16)
    c, s = cr_ref[...], sr_ref[...]
    cr = jnp.concatenate([c, c], axis=1)
    sr = jnp.concatenate([-s, s], axis=1)
    c, s = cm_ref[...], sm_ref[...]
    z = jnp.zeros_like(s)
    cm = jnp.concatenate([c, c, c, c], axis=1)
    sa = jnp.concatenate([-s, z, -s, z], axis=1)
    sb = jnp.concatenate([z, s, z, s], axis=1)

    zq = _dot(xn, wa_ref[:, 0:qk_w])
    zk = _dot(xn, wa_ref[:, o_rk:o_rk + qk_w])
    k_scale = RET_DK ** -0.5
    for h in range(RET_HEADS):
        sl = slice(h * RET_DK, (h + 1) * RET_DK)
        rq_ref[:, sl] = _rope_full(zq[:, sl], cr, sr).astype(BF16)
        rk_ref[:, sl] = (_rope_full(zk[:, sl], cr, sr) * k_scale).astype(BF16)
    rv_ref[...] = _dot(xn, wa_ref[:, o_rv:o_rv + v_w]).astype(BF16)

    cq = _dot(xn, wa_ref[:, o_cq:o_cq + q_lora])
    cqn = _rms(cq, qnw_ref[...]).astype(BF16)
    for h in range(MLA_HEADS):
        qh = _dot(cqn, wuq_ref[:, h * QK_PAD:(h + 1) * QK_PAD])
        q_ref[:, h * QK_PAD:h * QK_PAD + LANES] = qh[:, :LANES].astype(BF16)
        q_ref[:, h * QK_PAD + LANES:(h + 1) * QK_PAD] = _rope_half(qh[:, LANES:], cm, sa, sb).astype(BF16)

    ckv = _rms(_dot(xn, wa_ref[:, o_ckv:o_ckv + kv_lora]), kvnw_ref[...])
    ckv_ref[...] = ckv
    kpe_pad = _rope_half(_dot(xn, wa_ref[:, o_kpe:o_kpe + LANES]), cm, sa, sb)
    kpe_ref[...] = kpe_pad[:, :ROPE_DIM]
    if with_kv:
        c16 = ckv.astype(BF16)
        kn_ref[...] = _dot(c16, wuk_ref[...]).astype(BF16)
        v_ref[...] = _dot(c16, wuv_ref[...]).astype(BF16)
        kp_ref[...] = kpe_pad.astype(BF16)


def _inproj(x2, tabs, nw, wa, qnw, kvnw, wuq, wukv, *, tab_tiles):
    n, d = x2.shape
    tm = min(TOKEN_TILE, n)
    qk_w, v_w = RET_HEADS * RET_DK, RET_HEADS * RET_DV
    kv_lora = kvnw.shape[1]
    tab_specs = [pl.BlockSpec((tm, t.shape[1]), lambda i: (i % tab_tiles, 0)) for t in tabs]
    out_w = [(qk_w, BF16), (qk_w, BF16), (v_w, BF16), (MLA_HEADS * QK_PAD, BF16), (kv_lora, F32), (ROPE_DIM, F32)]
    extra = ()
    if wukv is not None:
        extra = tuple(wukv)
        out_w += [(MLA_HEADS * NOPE_DIM, BF16), (LANES, BF16), (MLA_HEADS * V_DIM, BF16)]
    return pl.pallas_call(
        functools.partial(_inproj_kernel, with_kv=wukv is not None),
        grid=(n // tm,),
        in_specs=[_row_spec(tm, d), _const_spec(nw.shape), _const_spec(wa.shape), _const_spec(qnw.shape),
                  _const_spec(kvnw.shape), _const_spec(wuq.shape)] + tab_specs + [_const_spec(w.shape) for w in extra],
        out_specs=[_row_spec(tm, w) for w, _ in out_w],
        out_shape=[jax.ShapeDtypeStruct((n, w), dt) for w, dt in out_w],
        compiler_params=_params("parallel"),
        name="inproj",
    )(x2, nw, wa, qnw, kvnw, wuq, *tabs, *extra)


def _retention_kernel(q_ref, k_ref, v_ref, s0_ref, gnw_ref, y_ref, sfin_ref, state_ref, *, blk, nblk):
    c = pl.program_id(1)

    @pl.when(c == 0)
    def _():
        state_ref[...] = s0_ref[...]

    ri = lax.broadcasted_iota(jnp.int32, (blk, blk), 0)
    ci = lax.broadcasted_iota(jnp.int32, (blk, blk), 1)
    diff = (ri - ci).astype(F32)
    n = lax.broadcasted_iota(jnp.int32, (blk, 1), 0).astype(F32)
    for h in range(RET_HEADS):
        lg = math.log(1.0 - 2.0 ** (-5.0 - h))
        q = q_ref[:, h * RET_DK:(h + 1) * RET_DK]
        k = k_ref[:, h * RET_DK:(h + 1) * RET_DK]
        v = v_ref[:, h * RET_DV:(h + 1) * RET_DV]
        decay = jnp.where(diff >= 0, jnp.exp(lg * jnp.maximum(diff, 0.0)), 0.0)
        scores = _dot_nt(q, k) * decay
        inner = _dot(scores.astype(BF16), v)
        st = state_ref[h]
        qd = (q.astype(F32) * jnp.exp(lg * (n + 1.0))).astype(BF16)
        cross = _dot(qd, st.astype(BF16))
        kd = (k.astype(F32) * jnp.exp(lg * (blk - 1.0 - n))).astype(BF16)
        state_ref[h] = math.exp(lg * blk) * st + _dot_tn(kd, v)
        y = inner + cross
        mu = jnp.mean(y, axis=-1, keepdims=True)
        yc = y - mu
        yn = yc * lax.rsqrt(jnp.mean(yc * yc, axis=-1, keepdims=True) + EPS)
        sl = slice(h * RET_DV, (h + 1) * RET_DV)
        y_ref[:, sl] = (yn * gnw_ref[:, sl]).astype(BF16)

    @pl.when(c == nblk - 1)
    def _():
        sfin_ref[...] = state_ref[...]


def _retention(rq, rk, rv, state0, gnw, blk):
    b, s, _ = rq.shape
    nblk = s // blk
    qk_w, v_w = RET_HEADS * RET_DK, RET_HEADS * RET_DV
    st_spec = pl.BlockSpec((None, RET_HEADS, RET_DK, RET_DV), lambda i, c: (i, 0, 0, 0))
    return pl.pallas_call(
        functools.partial(_retention_kernel, blk=blk, nblk=nblk),
        grid=(b, nblk),
        in_specs=[pl.BlockSpec((None, blk, qk_w), lambda i, c: (i, c, 0)),
                  pl.BlockSpec((None, blk, qk_w), lambda i, c: (i, c, 0)),
                  pl.BlockSpec((None, blk, v_w), lambda i, c: (i, c, 0)),
                  st_spec,
                  pl.BlockSpec((1, v_w), lambda i, c: (0, 0))],
        out_specs=[pl.BlockSpec((None, blk, v_w), lambda i, c: (i, c, 0)), st_spec],
        out_shape=[jax.ShapeDtypeStruct((b, s, v_w), BF16),
                   jax.ShapeDtypeStruct((b, RET_HEADS, RET_DK, RET_DV), F32)],
        scratch_shapes=[pltpu.VMEM((RET_HEADS, RET_DK, RET_DV), F32)],
        compiler_params=_params("parallel", "arbitrary"),
        name="retention",
    )(rq, rk, rv, state0, gnw)


_SOFTMAX_C = (NOPE_DIM + ROPE_DIM) ** -0.5 * math.log2(math.e)


def _attn_prompt_kernel(q_ref, kn_ref, kp_ref, v_ref, o_ref, s_scr, m_scr, acc_scr, *, tq, tk):
    nq = q_ref.shape[0] // tq
    ones = jnp.ones((tk, V_DIM), BF16)

    def rows(ref, t, n):
        return ref[pl.ds(pl.multiple_of(t * n, n), n), :]

    def scores(qi, t):
        k = jnp.concatenate([rows(kn_ref, t, tk), rows(kp_ref, t, tk)], axis=1)
        return _dot_nt(rows(q_ref, qi, tq), k)

    def consume(s, t, mask=None):
        if mask is not None:
            s = jnp.where(mask, s, NEG_BIG)
        m_old = m_scr[...]
        m_new = jnp.max(jnp.concatenate([s, m_old], axis=1), axis=-1, keepdims=True)
        alpha = jnp.exp2((m_old - m_new) * _SOFTMAX_C)
        p = jnp.exp2((s - m_new) * _SOFTMAX_C).astype(BF16)
        v_ext = jnp.concatenate([rows(v_ref, t, tk), ones], axis=1)
        m_scr[...] = jnp.broadcast_to(m_new, m_scr.shape)
        acc_scr[...] = jnp.concatenate([alpha, alpha], axis=1) * acc_scr[...] + _dot(p, v_ext)

    def restart():
        m_scr[...] = jnp.full_like(m_scr, NEG_BIG)
        acc_scr[...] = jnp.zeros_like(acc_scr)

    row = lax.broadcasted_iota(jnp.int32, (tq, tk), 0) // CHUNK
    col = lax.broadcasted_iota(jnp.int32, (tq, tk), 1) // CHUNK
    diag_masks = (col <= row, col + tk // CHUNK <= row)

    restart()
    s_scr[...] = scores(0, 0)

    def q_tile(qi, _):
        def pair(j, _):
            s_odd = scores(qi, 2 * j + 1)
            consume(s_scr[...], 2 * j)
            s_scr[...] = scores(qi, 2 * j + 2)
            consume(s_odd, 2 * j + 1)
            return 0

        lax.fori_loop(0, qi, pair, 0)
        s_odd = scores(qi, 2 * qi + 1)
        consume(s_scr[...], 2 * qi, diag_masks[0])
        s_scr[...] = scores(jnp.minimum(qi + 1, nq - 1), 0)
        consume(s_odd, 2 * qi + 1, diag_masks[1])
        acc = acc_scr[...]
        o_ref[pl.ds(pl.multiple_of(qi * tq, tq), tq), :] = (acc[:, :V_DIM] / acc[:, V_DIM:]).astype(BF16)
        restart()
        return 0

    lax.fori_loop(0, nq, q_tile, 0)


def _attn_prompt(q, kn, kp, v):
    b, s, _ = q.shape
    tq, tk = ATTN_Q_TILE, ATTN_Q_TILE // 2
    return pl.pallas_call(
        functools.partial(_attn_prompt_kernel, tq=tq, tk=tk),
        grid=(b, MLA_HEADS),
        in_specs=[pl.BlockSpec((None, s, QK_PAD), lambda i, h: (i, 0, h)),
                  pl.BlockSpec((None, s, NOPE_DIM), lambda i, h: (i, 0, h)),
                  pl.BlockSpec((None, s, LANES), lambda i, h: (i, 0, 0)),
                  pl.BlockSpec((None, s, V_DIM), lambda i, h: (i, 0, h))],
        out_specs=pl.BlockSpec((None, s, V_DIM), lambda i, h: (i, 0, h)),
        out_shape=jax.ShapeDtypeStruct((b, s, MLA_HEADS * V_DIM), BF16),
        scratch_shapes=[pltpu.VMEM((tq, tk), F32), pltpu.VMEM((tq, LANES), F32),
                        pltpu.VMEM((tq, 2 * V_DIM), F32)],
        compiler_params=_params("parallel", "parallel"),
        name="attn_prompt",
    )(q, kn, kp, v)


def _attn_sample_kernel(q_ref, cckv_ref, ckpe_ref, nckv_ref, nkpe_ref, wuk_ref, wuv_ref, o_ref):
    n = q_ref.shape[0]
    q_lat, q_rope = [], []
    for h in range(MLA_HEADS):
        q_nope = q_ref[:, h * QK_PAD:h * QK_PAD + NOPE_DIM]
        q_lat.append(_dot_nt(q_nope, wuk_ref[:, h * NOPE_DIM:(h + 1) * NOPE_DIM]).astype(BF16))
        q_rope.append(q_ref[:, h * QK_PAD + NOPE_DIM:(h + 1) * QK_PAD])
    q_all = jnp.concatenate([jnp.concatenate(q_lat, axis=0), jnp.concatenate(q_rope, axis=0)], axis=1)

    def keys(ckv_ref, kpe_ref):
        ckv = ckv_ref[...].astype(BF16)
        kpe = kpe_ref[...]
        kpe_pad = jnp.concatenate([kpe, jnp.zeros_like(kpe)], axis=1).astype(BF16)
        return ckv, jnp.concatenate([ckv, kpe_pad], axis=1)

    vc, kc = keys(cckv_ref, ckpe_ref)
    vn, kn = keys(nckv_ref, nkpe_ref)
    sc = _dot_nt(q_all, kc)
    sn = _dot_nt(q_all, kn)
    m = jnp.maximum(jnp.max(sc, axis=-1, keepdims=True), jnp.max(sn, axis=-1, keepdims=True))
    pc = jnp.exp2((sc - m) * _SOFTMAX_C)
    pn = jnp.exp2((sn - m) * _SOFTMAX_C)
    l = jnp.sum(pc, axis=-1, keepdims=True) + jnp.sum(pn, axis=-1, keepdims=True)
    o_lat = ((_dot(pc.astype(BF16), vc) + _dot(pn.astype(BF16), vn)) / l).astype(BF16)
    for h in range(MLA_HEADS):
        sl = slice(h * V_DIM, (h + 1) * V_DIM)
        o_ref[:, sl] = _dot(o_lat[h * n:(h + 1) * n], wuv_ref[:, sl]).astype(BF16)


def _attn_sample(q, cache_ckv, cache_kpe, new_ckv, new_kpe, wuk, wuv):
    b, s, _ = q.shape

    def per_stream(a):
        return pl.BlockSpec((None,) + a.shape[1:], lambda i: (i, 0, 0))

    return pl.pallas_call(
        _attn_sample_kernel,
        grid=(b,),
        in_specs=[per_stream(q), per_stream(cache_ckv), per_stream(cache_kpe), per_stream(new_ckv),
                  per_stream(new_kpe), _const_spec(wuk.shape), _const_spec(wuv.shape)],
        out_specs=pl.BlockSpec((None, s, MLA_HEADS * V_DIM), lambda i: (i, 0, 0)),
        out_shape=jax.ShapeDtypeStruct((b, s, MLA_HEADS * V_DIM), BF16),
        compiler_params=_params("parallel"),
        name="attn_sample",
    )(q, cache_ckv, cache_kpe, new_ckv, new_kpe, wuk, wuv)


def _mix_kernel(x_ref, ret_ref, mla_ref, nw_ref, wg_ref, wro_ref, wmo_ref, wo_ref, h_ref):
    d = x_ref.shape[1]
    v_w = ret_ref.shape[1]
    x = x_ref[...]
    xn = _rms(x, nw_ref[...]).astype(BF16)
    rg = _dot(xn, wg_ref[:, 0:v_w])
    ret_b = _dot((ret_ref[...].astype(F32) * _silu(rg)).astype(BF16), wro_ref[...])
    ga = _dot(xn, wg_ref[:, v_w:v_w + d])
    merged = _sigmoid(ga) * ret_b
    mla_b = _dot(mla_ref[...], wmo_ref[...])
    gb = _dot(xn, wg_ref[:, v_w + d:v_w + 2 * d])
    merged = merged + _sigmoid(gb) * mla_b
    h_ref[...] = x + _dot(merged.astype(BF16), wo_ref[...])


def _mix(x2, ret2, mla2, nw, wg, wro, wmo, wo):
    n, d = x2.shape
    tm = min(TOKEN_TILE, n)
    return pl.pallas_call(
        _mix_kernel,
        grid=(n // tm,),
        in_specs=[_row_spec(tm, d), _row_spec(tm, ret2.shape[1]), _row_spec(tm, mla2.shape[1]),
                  _const_spec(nw.shape), _const_spec(wg.shape), _const_spec(wro.shape),
                  _const_spec(wmo.shape), _const_spec(wo.shape)],
        out_specs=_row_spec(tm, d),
        out_shape=jax.ShapeDtypeStruct((n, d), F32),
        compiler_params=_params("parallel"),
        name="mix",
    )(x2, ret2, mla2, nw, wg, wro, wmo, wo)


def _ffn_kernel(h_ref, nw_ref, wgu_ref, wd_ref, fw_ref, y_ref, *, final_norm):
    h = h_ref[...]
    hn = _rms(h, nw_ref[...]).astype(BF16)
    acc = h
    d_ff = wd_ref.shape[0]
    for c in range(d_ff // FF_CHUNK):
        sl = slice(c * FF_CHUNK, (c + 1) * FF_CHUNK)
        g = _dot(hn, wgu_ref[:, sl])
        u = _dot(hn, wgu_ref[:, d_ff + c * FF_CHUNK:d_ff + (c + 1) * FF_CHUNK])
        acc = acc + _dot((_silu(g) * u).astype(BF16), wd_ref[sl, :])
    y_ref[...] = _rms(acc, fw_ref[...]) if final_norm else acc


def _ffn(h2, nw, wgu, wd, fw, final_norm):
    n, d = h2.shape
    tm = min(TOKEN_TILE, n)
    return pl.pallas_call(
        functools.partial(_ffn_kernel, final_norm=final_norm),
        grid=(n // tm,),
        in_specs=[_row_spec(tm, d), _const_spec(nw.shape), _const_spec(wgu.shape), _const_spec(wd.shape),
                  _const_spec(fw.shape)],
        out_specs=_row_spec(tm, d),
        out_shape=jax.ShapeDtypeStruct((n, d), F32),
        compiler_params=_params("parallel"),
        name="ffn",
    )(h2, nw, wgu, wd, fw)


def _rope_tables(start, n, reps):
    pos = np.arange(start, start + n, dtype=np.float64)[:, None]

    def cs(d):
        inv = ROPE_BASE ** (-np.arange(0, d, 2, dtype=np.float64) / d)
        ang = pos * inv[None, :]
        return [np.tile(f(ang).astype(np.float32), (reps, 1)) for f in (np.cos, np.sin)]

    return tuple(jnp.asarray(t) for t in cs(RET_DK) + cs(ROPE_DIM))


def _prep_layer_weights(w_in, w_uq, w_ukv, w_ret_o, w_mla_o, w_out, w_gate_up, w_down):
    d = w_in.shape[0]
    qk_w, v_w = RET_HEADS * RET_DK, RET_HEADS * RET_DV
    q_lora = w_uq.shape[0]
    kv_lora = w_ukv.shape[0]
    o_rg = 2 * qk_w + v_w
    o_cq = o_rg + v_w
    o_kpe = o_cq + q_lora + kv_lora
    o_ga = o_kpe + ROPE_DIM
    wa = jnp.concatenate([w_in[:, :o_rg], w_in[:, o_cq:o_ga], jnp.zeros((d, LANES - ROPE_DIM), w_in.dtype)],
                         axis=1).astype(BF16)
    wg = jnp.concatenate([w_in[:, o_rg:o_cq], w_in[:, o_ga:]], axis=1).astype(BF16)
    head_w = NOPE_DIM + ROPE_DIM
    wuq = jnp.pad(w_uq.reshape(q_lora, MLA_HEADS, head_w), ((0, 0), (0, 0), (0, QK_PAD - head_w)))
    wuq = wuq.reshape(q_lora, MLA_HEADS * QK_PAD).astype(BF16)
    wkv = w_ukv.reshape(kv_lora, MLA_HEADS, NOPE_DIM + V_DIM)
    wuk = wkv[:, :, :NOPE_DIM].reshape(kv_lora, MLA_HEADS * NOPE_DIM).astype(BF16)
    wuv = wkv[:, :, NOPE_DIM:].reshape(kv_lora, MLA_HEADS * V_DIM).astype(BF16)
    assert w_down.shape[0] % FF_CHUNK == 0
    return dict(wa=wa, wg=wg, wuq=wuq, wuk=wuk, wuv=wuv, wro=w_ret_o.astype(BF16), wmo=w_mla_o.astype(BF16),
                wo=w_out.astype(BF16), wgu=w_gate_up.astype(BF16), wd=w_down.astype(BF16))


def _layer(x, tabs, tab_tiles, lw, norms, state0, cache, fw, final_norm):
    b, s, d = x.shape
    n = b * s
    x2 = x.reshape(n, d)
    nmw, qnw, kvnw, gnw, nfw = norms
    wukv = (lw["wuk"], lw["wuv"])
    rq, rk, rv, q, ckv, kpe, *kv = _inproj(x2, tabs, nmw, lw["wa"], qnw, kvnw, lw["wuq"],
                                           wukv if cache is None else None, tab_tiles=tab_tiles)
    ckv3 = ckv.reshape(b, s, -1)
    kpe3 = kpe.reshape(b, s, ROPE_DIM)
    q3 = q.reshape(b, s, MLA_HEADS * QK_PAD)
    if cache is None:
        mla = _attn_prompt(q3, *(a.reshape(b, s, -1) for a in kv))
    else:
        mla = _attn_sample(q3, cache[0].astype(F32), cache[1].astype(F32), ckv3, kpe3, *wukv)
    blk = RET_BLOCK if s % RET_BLOCK == 0 else CHUNK
    ret, s_fin = _retention(rq.reshape(b, s, -1), rk.reshape(b, s, -1), rv.reshape(b, s, -1), state0, gnw, blk)
    h2 = _mix(x2, ret.reshape(n, -1), mla.reshape(n, -1), nmw, lw["wg"], lw["wro"], lw["wmo"], lw["wo"])
    y2 = _ffn(h2, nfw, lw["wgu"], lw["wd"], fw, final_norm)
    return y2.reshape(b, s, d), (ckv3, kpe3, s_fin)


def kernel(x_prompt, x_sample, cache_ckv, cache_kpe, state_ret, norm_mix_w, w_in, q_norm_w, w_uq, kv_norm_w, w_ukv, ret_gn_w, w_ret_o, w_mla_o, w_out, norm_ffn_w, w_gate_up, w_down, norm_final_w):
    depth = w_in.shape[0]
    bp, sp, _ = x_prompt.shape
    bs, ss, _ = x_sample.shape
    past = cache_ckv.shape[2]
    assert sp % TOKEN_TILE == 0 and sp % ATTN_Q_TILE == 0 and (bs * ss) % min(TOKEN_TILE, bs * ss) == 0
    assert ss == CHUNK, "the sample group is one streaming chunk"

    tabs_p = _rope_tables(0, sp, 1)
    tabs_s = _rope_tables(past, ss, bs)
    fw = norm_final_w.reshape(1, -1)
    state0_p = jnp.zeros((bp, RET_HEADS, RET_DK, RET_DV), F32)

    hp, hs = x_prompt, x_sample
    outs = [[] for _ in range(6)]
    for l in range(depth):
        lw = _prep_layer_weights(w_in[l], w_uq[l], w_ukv[l], w_ret_o[l], w_mla_o[l], w_out[l], w_gate_up[l],
                                 w_down[l])
        norms = (norm_mix_w[l].reshape(1, -1), q_norm_w[l].reshape(1, -1), kv_norm_w[l].reshape(1, -1),
                 ret_gn_w[l].reshape(1, -1), norm_ffn_w[l].reshape(1, -1))
        final = l == depth - 1
        hp, (a, b_, c) = _layer(hp, tabs_p, sp // TOKEN_TILE, lw, norms, state0_p, None, fw, final)
        hs, (d_, e, f) = _layer(hs, tabs_s, 1, lw, norms, state_ret[l].astype(F32),
                                (cache_ckv[l], cache_kpe[l]), fw, final)
        for lst, val in zip(outs, (a, b_, c, d_, e, f)):
            lst.append(val)
    return (hp, hs) + tuple(o[0][None] if depth == 1 else jnp.stack(o) for o in outs)
```

```python
import functools
import math

import jax
import jax.numpy as jnp
import numpy as np
from jax import lax
from jax.experimental import pallas as pl
from jax.experimental.pallas import tpu as pltpu

F32 = jnp.float32
BF16 = jnp.bfloat16

CHUNK = 64
RET_HEADS = 4
RET_DK = 128
RET_DV = 256
MLA_HEADS = 8
NOPE_DIM = 128
ROPE_DIM = 64
V_DIM = 128
ROPE_BASE = 10000.0
EPS = 1e-6

LANES = 128
QK_PAD = 2 * LANES
V7X_VMEM_BYTES = 64 * 2**20
VMEM_LIMIT = V7X_VMEM_BYTES * 7 // 8
TOKEN_TILE = 512
RET_BLOCK = 256
ATTN_Q_TILE = 2048
ATTN_KV_TILE = 512
FF_CHUNK = 256
NEG_BIG = -1e30
SOFTMAX_C = (NOPE_DIM + ROPE_DIM) ** -0.5 * math.log2(math.e)

_dot = functools.partial(jnp.dot, preferred_element_type=F32)


def _dot_nt(a, b):
    return lax.dot_general(a, b, (((1,), (1,)), ((), ())), preferred_element_type=F32)


def _dot_tn(a, b):
    return lax.dot_general(a, b, (((0,), (0,)), ((), ())), preferred_element_type=F32)


def _rms(x, g):
    ms = jnp.mean(x * x, axis=-1, keepdims=True)
    return x * lax.rsqrt(ms + EPS) * g


def _sigmoid(x):
    return 1.0 / (1.0 + jnp.exp(-x))


def _silu(x):
    return x * _sigmoid(x)


def _params(*sem):
    return pltpu.CompilerParams(dimension_semantics=sem, vmem_limit_bytes=VMEM_LIMIT)


def _const_spec(shape):
    return pl.BlockSpec(shape, lambda *_: (0,) * len(shape), pipeline_mode=pl.Buffered(1))


def _row_spec(tm, width):
    return pl.BlockSpec((tm, width), lambda i: (i, 0))


def _rope_full(x, cos2, sin2):
    return x * cos2 + pltpu.roll(x, LANES // 2, 1) * sin2


def _rope_half(x, c, sa, sb):
    q = ROPE_DIM // 2
    return x * c + pltpu.roll(x, LANES - q, 1) * sa + pltpu.roll(x, q, 1) * sb


def _inproj_kernel(x_ref, nw_ref, wa_ref, qnw_ref, kvnw_ref, wuq_ref, cr_ref, sr_ref, cm_ref, sm_ref, *rest,
                   with_kv):
    if with_kv:
        wuk_ref, wuv_ref, rq_ref, rk_ref, rv_ref, q_ref, ckv_ref, kpe_ref, kn_ref, kp_ref, v_ref = rest
    else:
        rq_ref, rk_ref, rv_ref, q_ref, ckv_ref, kpe_ref = rest
    qk_w = RET_HEADS * RET_DK
    v_w = RET_HEADS * RET_DV
    q_lora = wuq_ref.shape[0]
    kv_lora = ckv_ref.shape[1]
    o_rk, o_rv, o_cq = qk_w, 2 * qk_w, 2 * qk_w + v_w
    o_ckv = o_cq + q_lora
    o_kpe = o_ckv + kv_lora

    xn = _rms(x_ref[...], nw_ref[...]).astype(BF16)
    c, s = cr_ref[...], sr_ref[...]
    cr = jnp.concatenate([c, c], axis=1)
    sr = jnp.concatenate([-s, s], axis=1)
    c, s = cm_ref[...], sm_ref[...]
    z = jnp.zeros_like(s)
    cm = jnp.concatenate([c, c, c, c], axis=1)
    sa = jnp.concatenate([-s, z, -s, z], axis=1)
    sb = jnp.concatenate([z, s, z, s], axis=1)

    zq = _dot(xn, wa_ref[:, 0:qk_w])
    zk = _dot(xn, wa_ref[:, o_rk:o_rk + qk_w])
    k_scale = RET_DK ** -0.5
    for h in range(RET_HEADS):
        sl = slice(h * RET_DK, (h + 1) * RET_DK)
        rq_ref[:, sl] = _rope_full(zq[:, sl], cr, sr).astype(BF16)
        rk_ref[:, sl] = (_rope_full(zk[:, sl], cr, sr) * k_scale).astype(BF16)
    rv_ref[...] = _dot(xn, wa_ref[:, o_rv:o_rv + v_w]).astype(BF16)

    cq = _dot(xn, wa_ref[:, o_cq:o_cq + q_lora])
    cqn = _rms(cq, qnw_ref[...]).astype(BF16)
    for h in range(MLA_HEADS):
        qh = _dot(cqn, wuq_ref[:, h * QK_PAD:(h + 1) * QK_PAD]) * SOFTMAX_C
        q_ref[:, h * QK_PAD:h * QK_PAD + LANES] = qh[:, :LANES].astype(BF16)
        q_ref[:, h * QK_PAD + LANES:(h + 1) * QK_PAD] = _rope_half(qh[:, LANES:], cm, sa, sb).astype(BF16)

    ckv = _rms(_dot(xn, wa_ref[:, o_ckv:o_ckv + kv_lora]), kvnw_ref[...])
    ckv_ref[...] = ckv
    kpe_pad = _rope_half(_dot(xn, wa_ref[:, o_kpe:o_kpe + LANES]), cm, sa, sb)
    kpe_ref[...] = kpe_pad[:, :ROPE_DIM]
    if with_kv:
        c16 = ckv.astype(BF16)
        kn_ref[...] = _dot(c16, wuk_ref[...]).astype(BF16)
        v_ref[...] = _dot(c16, wuv_ref[...]).astype(BF16)
        kp_ref[...] = kpe_pad.astype(BF16)


def _inproj(x2, tabs, nw, wa, qnw, kvnw, wuq, wukv, *, tab_tiles):
    n, d = x2.shape
    tm = min(TOKEN_TILE, n)
    qk_w, v_w = RET_HEADS * RET_DK, RET_HEADS * RET_DV
    kv_lora = kvnw.shape[1]
    tab_specs = [pl.BlockSpec((tm, t.shape[1]), lambda i: (i % tab_tiles, 0)) for t in tabs]
    out_w = [(qk_w, BF16), (qk_w, BF16), (v_w, BF16), (MLA_HEADS * QK_PAD, BF16), (kv_lora, F32), (ROPE_DIM, F32)]
    extra = ()
    if wukv is not None:
        extra = tuple(wukv)
        out_w += [(MLA_HEADS * NOPE_DIM, BF16), (LANES, BF16), (MLA_HEADS * V_DIM, BF16)]
    return pl.pallas_call(
        functools.partial(_inproj_kernel, with_kv=wukv is not None),
        grid=(n // tm,),
        in_specs=[_row_spec(tm, d), _const_spec(nw.shape), _const_spec(wa.shape), _const_spec(qnw.shape),
                  _const_spec(kvnw.shape), _const_spec(wuq.shape)] + tab_specs + [_const_spec(w.shape) for w in extra],
        out_specs=[_row_spec(tm, w) for w, _ in out_w],
        out_shape=[jax.ShapeDtypeStruct((n, w), dt) for w, dt in out_w],
        compiler_params=_params("parallel"),
        name="inproj",
    )(x2, nw, wa, qnw, kvnw, wuq, *tabs, *extra)


def _retention_kernel(q_ref, k_ref, v_ref, s0_ref, gnw_ref, y_ref, sfin_ref, state_ref, *, blk, nblk):
    c = pl.program_id(1)

    @pl.when(c == 0)
    def _():
        state_ref[...] = s0_ref[...]

    ri = lax.broadcasted_iota(jnp.int32, (blk, blk), 0)
    ci = lax.broadcasted_iota(jnp.int32, (blk, blk), 1)
    diff = (ri - ci).astype(F32)
    n = lax.broadcasted_iota(jnp.int32, (blk, 1), 0).astype(F32)
    for h in range(RET_HEADS):
        lg = math.log(1.0 - 2.0 ** (-5.0 - h))
        q = q_ref[:, h * RET_DK:(h + 1) * RET_DK]
        k = k_ref[:, h * RET_DK:(h + 1) * RET_DK]
        v = v_ref[:, h * RET_DV:(h + 1) * RET_DV]
        decay = jnp.where(diff >= 0, jnp.exp(lg * jnp.maximum(diff, 0.0)), 0.0)
        scores = _dot_nt(q, k) * decay
        inner = _dot(scores.astype(BF16), v)
        st = state_ref[h]
        qd = (q.astype(F32) * jnp.exp(lg * (n + 1.0))).astype(BF16)
        cross = _dot(qd, st.astype(BF16))
        kd = (k.astype(F32) * jnp.exp(lg * (blk - 1.0 - n))).astype(BF16)
        state_ref[h] = math.exp(lg * blk) * st + _dot_tn(kd, v)
        y = inner + cross
        mu = jnp.mean(y, axis=-1, keepdims=True)
        yc = y - mu
        yn = yc * lax.rsqrt(jnp.mean(yc * yc, axis=-1, keepdims=True) + EPS)
        sl = slice(h * RET_DV, (h + 1) * RET_DV)
        y_ref[:, sl] = (yn * gnw_ref[:, sl]).astype(BF16)

    @pl.when(c == nblk - 1)
    def _():
        sfin_ref[...] = state_ref[...]


def _retention(rq, rk, rv, state0, gnw, blk):
    b, s, _ = rq.shape
    nblk = s // blk
    qk_w, v_w = RET_HEADS * RET_DK, RET_HEADS * RET_DV
    st_spec = pl.BlockSpec((None, RET_HEADS, RET_DK, RET_DV), lambda i, c: (i, 0, 0, 0))
    return pl.pallas_call(
        functools.partial(_retention_kernel, blk=blk, nblk=nblk),
        grid=(b, nblk),
        in_specs=[pl.BlockSpec((None, blk, qk_w), lambda i, c: (i, c, 0)),
                  pl.BlockSpec((None, blk, qk_w), lambda i, c: (i, c, 0)),
                  pl.BlockSpec((None, blk, v_w), lambda i, c: (i, c, 0)),
                  st_spec,
                  pl.BlockSpec((1, v_w), lambda i, c: (0, 0))],
        out_specs=[pl.BlockSpec((None, blk, v_w), lambda i, c: (i, c, 0)), st_spec],
        out_shape=[jax.ShapeDtypeStruct((b, s, v_w), BF16),
                   jax.ShapeDtypeStruct((b, RET_HEADS, RET_DK, RET_DV), F32)],
        scratch_shapes=[pltpu.VMEM((RET_HEADS, RET_DK, RET_DV), F32)],
        compiler_params=_params("parallel", "arbitrary"),
        name="retention",
    )(rq, rk, rv, state0, gnw)


def _attn_prompt_kernel(q_ref, kn_ref, kp_ref, v_ref, o_ref, s_scr, mx_scr, m_scr, acc_scr, *, tq, tk):
    nq = q_ref.shape[0] // tq
    r = tq // tk
    ones = jnp.ones((tk, V_DIM), BF16)

    def rows(ref, t, n):
        return ref[pl.ds(pl.multiple_of(t * n, n), n), :]

    def lane_max(s):
        blocks = [s[:, c * LANES:(c + 1) * LANES] for c in range(tk // LANES)]
        while len(blocks) > 1:
            blocks = [jnp.maximum(a, b) for a, b in zip(blocks[::2], blocks[1::2])]
        return blocks[0]

    def scores(qi, t, row0=0):
        k = jnp.concatenate([rows(kn_ref, t, tk), rows(kp_ref, t, tk)], axis=1)
        q = q_ref[pl.ds(pl.multiple_of(qi * tq + row0, tk), tq - row0), :]
        s = _dot_nt(q, k)
        return s, lane_max(s)

    def park(sm, row0=0):
        s_scr[row0:, :], mx_scr[row0:, :] = sm

    def parked(row0=0):
        return s_scr[row0:, :], mx_scr[row0:, :]

    def consume(sm, t, row0=0, masked=False):
        s, mx = sm
        if masked:
            row = lax.broadcasted_iota(jnp.int32, s.shape, 0) // CHUNK
            col = lax.broadcasted_iota(jnp.int32, s.shape, 1) // CHUNK
            s = jnp.where(col <= row, s, NEG_BIG)
            mx = lane_max(s)
        m_old = m_scr[row0:, :]
        m_new = jnp.max(jnp.concatenate([mx, m_old], axis=1), axis=-1, keepdims=True)
        alpha = jnp.exp2(m_old - m_new)
        p = jnp.exp2(s - m_new).astype(BF16)
        v_ext = jnp.concatenate([rows(v_ref, t, tk), ones], axis=1)
        m_scr[row0:, :] = jnp.broadcast_to(m_new, m_old.shape)
        acc_scr[row0:, :] = jnp.concatenate([alpha, alpha], axis=1) * acc_scr[row0:, :] + _dot(p, v_ext)

    def restart():
        m_scr[...] = jnp.full_like(m_scr, NEG_BIG)
        acc_scr[...] = jnp.zeros_like(acc_scr)

    restart()
    park(scores(0, 0))

    def q_tile(qi, _):
        def trip(j, _):
            for u in range(0, r, 2):
                t = r * j + u
                odd = scores(qi, t + 1)
                consume(parked(), t)
                park(scores(qi, t + 2))
                consume(odd, t + 1)
            return 0

        lax.fori_loop(0, qi, trip, 0)
        for d in range(0, r, 2):
            t = r * qi + d
            odd = scores(qi, t + 1, (d + 1) * tk)
            consume(parked(d * tk), t, d * tk, masked=True)
            if d + 2 < r:
                park(scores(qi, t + 2, (d + 2) * tk), (d + 2) * tk)
            else:
                park(scores(jnp.minimum(qi + 1, nq - 1), 0))
            consume(odd, t + 1, (d + 1) * tk, masked=True)
        acc = acc_scr[...]
        o_ref[pl.ds(pl.multiple_of(qi * tq, tq), tq), :] = (acc[:, :V_DIM] / acc[:, V_DIM:]).astype(BF16)
        restart()
        return 0

    lax.fori_loop(0, nq, q_tile, 0)


def _attn_prompt(q, kn, kp, v):
    b, s, _ = q.shape
    tq, tk = ATTN_Q_TILE, ATTN_KV_TILE
    assert tq % (2 * tk) == 0 and s % tq == 0
    return pl.pallas_call(
        functools.partial(_attn_prompt_kernel, tq=tq, tk=tk),
        grid=(b, MLA_HEADS),
        in_specs=[pl.BlockSpec((None, s, QK_PAD), lambda i, h: (i, 0, h)),
                  pl.BlockSpec((None, s, NOPE_DIM), lambda i, h: (i, 0, h)),
                  pl.BlockSpec((None, s, LANES), lambda i, h: (i, 0, 0)),
                  pl.BlockSpec((None, s, V_DIM), lambda i, h: (i, 0, h))],
        out_specs=pl.BlockSpec((None, s, V_DIM), lambda i, h: (i, 0, h)),
        out_shape=jax.ShapeDtypeStruct((b, s, MLA_HEADS * V_DIM), BF16),
        scratch_shapes=[pltpu.VMEM((tq, tk), F32), pltpu.VMEM((tq, LANES), F32), pltpu.VMEM((tq, LANES), F32),
                        pltpu.VMEM((tq, 2 * V_DIM), F32)],
        compiler_params=_params("parallel", "parallel"),
        name="attn_prompt",
    )(q, kn, kp, v)


def _attn_sample_kernel(q_ref, cckv_ref, ckpe_ref, nckv_ref, nkpe_ref, wuk_ref, wuv_ref, o_ref):
    n = q_ref.shape[0]
    q_lat, q_rope = [], []
    for h in range(MLA_HEADS):
        q_nope = q_ref[:, h * QK_PAD:h * QK_PAD + NOPE_DIM]
        q_lat.append(_dot_nt(q_nope, wuk_ref[:, h * NOPE_DIM:(h + 1) * NOPE_DIM]).astype(BF16))
        q_rope.append(q_ref[:, h * QK_PAD + NOPE_DIM:(h + 1) * QK_PAD])
    q_all = jnp.concatenate([jnp.concatenate(q_lat, axis=0), jnp.concatenate(q_rope, axis=0)], axis=1)

    def keys(ckv_ref, kpe_ref):
        ckv = ckv_ref[...].astype(BF16)
        kpe = kpe_ref[...]
        kpe_pad = jnp.concatenate([kpe, jnp.zeros_like(kpe)], axis=1).astype(BF16)
        return ckv, jnp.concatenate([ckv, kpe_pad], axis=1)

    vc, kc = keys(cckv_ref, ckpe_ref)
    vn, kn = keys(nckv_ref, nkpe_ref)
    sc = _dot_nt(q_all, kc)
    sn = _dot_nt(q_all, kn)
    m = jnp.maximum(jnp.max(sc, axis=-1, keepdims=True), jnp.max(sn, axis=-1, keepdims=True))
    pc = jnp.exp2(sc - m)
    pn = jnp.exp2(sn - m)
    l = jnp.sum(pc, axis=-1, keepdims=True) + jnp.sum(pn, axis=-1, keepdims=True)
    o_lat = ((_dot(pc.astype(BF16), vc) + _dot(pn.astype(BF16), vn)) / l).astype(BF16)
    for h in range(MLA_HEADS):
        sl = slice(h * V_DIM, (h + 1) * V_DIM)
        o_ref[:, sl] = _dot(o_lat[h * n:(h + 1) * n], wuv_ref[:, sl]).astype(BF16)


def _attn_sample(q, cache_ckv, cache_kpe, new_ckv, new_kpe, wuk, wuv):
    b, s, _ = q.shape

    def per_stream(a):
        return pl.BlockSpec((None,) + a.shape[1:], lambda i: (i, 0, 0))

    return pl.pallas_call(
        _attn_sample_kernel,
        grid=(b,),
        in_specs=[per_stream(q), per_stream(cache_ckv), per_stream(cache_kpe), per_stream(new_ckv),
                  per_stream(new_kpe), _const_spec(wuk.shape), _const_spec(wuv.shape)],
        out_specs=pl.BlockSpec((None, s, MLA_HEADS * V_DIM), lambda i: (i, 0, 0)),
        out_shape=jax.ShapeDtypeStruct((b, s, MLA_HEADS * V_DIM), BF16),
        compiler_params=_params("parallel"),
        name="attn_sample",
    )(q, cache_ckv, cache_kpe, new_ckv, new_kpe, wuk, wuv)


def _mix_kernel(x_ref, ret_ref, mla_ref, nw_ref, wg_ref, wro_ref, wmo_ref, wo_ref, h_ref):
    d = x_ref.shape[1]
    v_w = ret_ref.shape[1]
    x = x_ref[...]
    xn = _rms(x, nw_ref[...]).astype(BF16)
    rg = _dot(xn, wg_ref[:, 0:v_w])
    ret_b = _dot((ret_ref[...].astype(F32) * _silu(rg)).astype(BF16), wro_ref[...])
    ga = _dot(xn, wg_ref[:, v_w:v_w + d])
    merged = _sigmoid(ga) * ret_b
    mla_b = _dot(mla_ref[...], wmo_ref[...])
    gb = _dot(xn, wg_ref[:, v_w + d:v_w + 2 * d])
    merged = merged + _sigmoid(gb) * mla_b
    h_ref[...] = x + _dot(merged.astype(BF16), wo_ref[...])


def _mix(x2, ret2, mla2, nw, wg, wro, wmo, wo):
    n, d = x2.shape
    tm = min(TOKEN_TILE, n)
    return pl.pallas_call(
        _mix_kernel,
        grid=(n // tm,),
        in_specs=[_row_spec(tm, d), _row_spec(tm, ret2.shape[1]), _row_spec(tm, mla2.shape[1]),
                  _const_spec(nw.shape), _const_spec(wg.shape), _const_spec(wro.shape),
                  _const_spec(wmo.shape), _const_spec(wo.shape)],
        out_specs=_row_spec(tm, d),
        out_shape=jax.ShapeDtypeStruct((n, d), F32),
        compiler_params=_params("parallel"),
        name="mix",
    )(x2, ret2, mla2, nw, wg, wro, wmo, wo)


def _ffn_kernel(h_ref, nw_ref, wgu_ref, wd_ref, fw_ref, y_ref, *, final_norm):
    h = h_ref[...]
    hn = _rms(h, nw_ref[...]).astype(BF16)
    acc = h
    d_ff = wd_ref.shape[0]
    for c in range(d_ff // FF_CHUNK):
        sl = slice(c * FF_CHUNK, (c + 1) * FF_CHUNK)
        g = _dot(hn, wgu_ref[:, sl])
        u = _dot(hn, wgu_ref[:, d_ff + c * FF_CHUNK:d_ff + (c + 1) * FF_CHUNK])
        acc = acc + _dot((_silu(g) * u).astype(BF16), wd_ref[sl, :])
    y_ref[...] = _rms(acc, fw_ref[...]) if final_norm else acc


def _ffn(h2, nw, wgu, wd, fw, final_norm):
    n, d = h2.shape
    tm = min(TOKEN_TILE, n)
    return pl.pallas_call(
        functools.partial(_ffn_kernel, final_norm=final_norm),
        grid=(n // tm,),
        in_specs=[_row_spec(tm, d), _const_spec(nw.shape), _const_spec(wgu.shape), _const_spec(wd.shape),
                  _const_spec(fw.shape)],
        out_specs=_row_spec(tm, d),
        out_shape=jax.ShapeDtypeStruct((n, d), F32),
        compiler_params=_params("parallel"),
        name="ffn",
    )(h2, nw, wgu, wd, fw)


def _rope_tables(start, n, reps):
    pos = np.arange(start, start + n, dtype=np.float64)[:, None]

    def cs(d):
        inv = ROPE_BASE ** (-np.arange(0, d, 2, dtype=np.float64) / d)
        ang = pos * inv[None, :]
        return [np.tile(f(ang).astype(np.float32), (reps, 1)) for f in (np.cos, np.sin)]

    return tuple(jnp.asarray(t) for t in cs(RET_DK) + cs(ROPE_DIM))


def _prep_layer_weights(w_in, w_uq, w_ukv, w_ret_o, w_mla_o, w_out, w_gate_up, w_down):
    d = w_in.shape[0]
    qk_w, v_w = RET_HEADS * RET_DK, RET_HEADS * RET_DV
    q_lora = w_uq.shape[0]
    kv_lora = w_ukv.shape[0]
    o_rg = 2 * qk_w + v_w
    o_cq = o_rg + v_w
    o_kpe = o_cq + q_lora + kv_lora
    o_ga = o_kpe + ROPE_DIM
    wa = jnp.concatenate([w_in[:, :o_rg], w_in[:, o_cq:o_ga], jnp.zeros((d, LANES - ROPE_DIM), w_in.dtype)],
                         axis=1).astype(BF16)
    wg = jnp.concatenate([w_in[:, o_rg:o_cq], w_in[:, o_ga:]], axis=1).astype(BF16)
    head_w = NOPE_DIM + ROPE_DIM
    wuq = jnp.pad(w_uq.reshape(q_lora, MLA_HEADS, head_w), ((0, 0), (0, 0), (0, QK_PAD - head_w)))
    wuq = wuq.reshape(q_lora, MLA_HEADS * QK_PAD).astype(BF16)
    wkv = w_ukv.reshape(kv_lora, MLA_HEADS, NOPE_DIM + V_DIM)
    wuk = wkv[:, :, :NOPE_DIM].reshape(kv_lora, MLA_HEADS * NOPE_DIM).astype(BF16)
    wuv = wkv[:, :, NOPE_DIM:].reshape(kv_lora, MLA_HEADS * V_DIM).astype(BF16)
    assert w_down.shape[0] % FF_CHUNK == 0
    return dict(wa=wa, wg=wg, wuq=wuq, wuk=wuk, wuv=wuv, wro=w_ret_o.astype(BF16), wmo=w_mla_o.astype(BF16),
                wo=w_out.astype(BF16), wgu=w_gate_up.astype(BF16), wd=w_down.astype(BF16))


def _layer(x, tabs, tab_tiles, lw, norms, state0, cache, fw, final_norm):
    b, s, d = x.shape
    n = b * s
    x2 = x.reshape(n, d)
    nmw, qnw, kvnw, gnw, nfw = norms
    wukv = (lw["wuk"], lw["wuv"])
    rq, rk, rv, q, ckv, kpe, *kv = _inproj(x2, tabs, nmw, lw["wa"], qnw, kvnw, lw["wuq"],
                                           wukv if cache is None else None, tab_tiles=tab_tiles)
    ckv3 = ckv.reshape(b, s, -1)
    kpe3 = kpe.reshape(b, s, ROPE_DIM)
    q3 = q.reshape(b, s, MLA_HEADS * QK_PAD)
    if cache is None:
        mla = _attn_prompt(q3, *(a.reshape(b, s, -1) for a in kv))
    else:
        mla = _attn_sample(q3, cache[0].astype(F32), cache[1].astype(F32), ckv3, kpe3, *wukv)
    blk = RET_BLOCK if s % RET_BLOCK == 0 else CHUNK
    ret, s_fin = _retention(rq.reshape(b, s, -1), rk.reshape(b, s, -1), rv.reshape(b, s, -1), state0, gnw, blk)
    h2 = _mix(x2, ret.reshape(n, -1), mla.reshape(n, -1), nmw, lw["wg"], lw["wro"], lw["wmo"], lw["wo"])
    y2 = _ffn(h2, nfw, lw["wgu"], lw["wd"], fw, final_norm)
    return y2.reshape(b, s, d), (ckv3, kpe3, s_fin)


def kernel(x_prompt, x_sample, cache_ckv, cache_kpe, state_ret, norm_mix_w, w_in, q_norm_w, w_uq, kv_norm_w, w_ukv, ret_gn_w, w_ret_o, w_mla_o, w_out, norm_ffn_w, w_gate_up, w_down, norm_final_w):
    depth = w_in.shape[0]
    bp, sp, _ = x_prompt.shape
    bs, ss, _ = x_sample.shape
    past = cache_ckv.shape[2]
    assert sp % TOKEN_TILE == 0 and sp % ATTN_Q_TILE == 0 and (bs * ss) % min(TOKEN_TILE, bs * ss) == 0
    assert ss == CHUNK, "the sample group is one streaming chunk"

    tabs_p = _rope_tables(0, sp, 1)
    tabs_s = _rope_tables(past, ss, bs)
    fw = norm_final_w.reshape(1, -1)
    state0_p = jnp.zeros((bp, RET_HEADS, RET_DK, RET_DV), F32)

    hp, hs = x_prompt, x_sample
    outs = [[] for _ in range(6)]
    for l in range(depth):
        lw = _prep_layer_weights(w_in[l], w_uq[l], w_ukv[l], w_ret_o[l], w_mla_o[l], w_out[l], w_gate_up[l],
                                 w_down[l])
        norms = (norm_mix_w[l].reshape(1, -1), q_norm_w[l].reshape(1, -1), kv_norm_w[l].reshape(1, -1),
                 ret_gn_w[l].reshape(1, -1), norm_ffn_w[l].reshape(1, -1))
        final = l == depth - 1
        hp, (a, b_, c) = _layer(hp, tabs_p, sp // TOKEN_TILE, lw, norms, state0_p, None, fw, final)
        hs, (d_, e, f) = _layer(hs, tabs_s, 1, lw, norms, state_ret[l].astype(F32),
                                (cache_ckv[l], cache_kpe[l]), fw, final)
        for lst, val in zip(outs, (a, b_, c, d_, e, f)):
            lst.append(val)
    return (hp, hs) + tuple(o[0][None] if depth == 1 else jnp.stack(o) for o in outs)
```

```python
import functools
import math

import jax
import jax.numpy as jnp
import numpy as np
from jax import lax
from jax.experimental import pallas as pl
from jax.experimental.pallas import tpu as pltpu

F32 = jnp.float32
BF16 = jnp.bfloat16

CHUNK = 64
RET_HEADS = 4
RET_DK = 128
RET_DV = 256
MLA_HEADS = 8
NOPE_DIM = 128
ROPE_DIM = 64
V_DIM = 128
ROPE_BASE = 10000.0
EPS = 1e-6

LANES = 128
QK_PAD = 2 * LANES
V7X_VMEM_BYTES = 64 * 2**20
VMEM_LIMIT = V7X_VMEM_BYTES * 7 // 8
TOKEN_TILE = 512
RET_BLOCK = 256
ATTN_Q_TILE = 2048
ATTN_KV_TILE = 512
FF_CHUNK = 256
NEG_BIG = -1e30
SOFTMAX_C = (NOPE_DIM + ROPE_DIM) ** -0.5 * math.log2(math.e)

_dot = functools.partial(jnp.dot, preferred_element_type=F32)


def _dot_nt(a, b):
    return lax.dot_general(a, b, (((1,), (1,)), ((), ())), preferred_element_type=F32)


def _dot_tn(a, b):
    return lax.dot_general(a, b, (((0,), (0,)), ((), ())), preferred_element_type=F32)


def _rms(x, g):
    ms = jnp.mean(x * x, axis=-1, keepdims=True)
    return x * lax.rsqrt(ms + EPS) * g


def _sigmoid(x):
    return 1.0 / (1.0 + jnp.exp(-x))


def _silu(x):
    return x * _sigmoid(x)


def _params(*sem):
    return pltpu.CompilerParams(dimension_semantics=sem, vmem_limit_bytes=VMEM_LIMIT)


def _const_spec(shape):
    return pl.BlockSpec(shape, lambda *_: (0,) * len(shape), pipeline_mode=pl.Buffered(1))


def _row_spec(tm, width):
    return pl.BlockSpec((tm, width), lambda i: (i, 0))


def _rope_full(x, cos2, sin2):
    return x * cos2 + pltpu.roll(x, LANES // 2, 1) * sin2


def _rope_half(x, c, sa, sb):
    q = ROPE_DIM // 2
    return x * c + pltpu.roll(x, LANES - q, 1) * sa + pltpu.roll(x, q, 1) * sb


def _ret_decay(blk, h):
    lg = math.log(1.0 - 2.0 ** (-5.0 - h))
    ri = lax.broadcasted_iota(jnp.int32, (blk, blk), 0)
    ci = lax.broadcasted_iota(jnp.int32, (blk, blk), 1)
    diff = (ri - ci).astype(F32)
    n = lax.broadcasted_iota(jnp.int32, (blk, RET_DK), 0).astype(F32)
    decay = jnp.where(diff >= 0, jnp.exp(lg * jnp.maximum(diff, 0.0)), 0.0)
    return decay, jnp.exp(lg * (n + 1.0)), jnp.exp(lg * (blk - 1.0 - n)), math.exp(lg * blk)


def _ret_block(q, k, v, state_ref, h, dec, gnw):
    decay, q_dec, k_dec, s_dec = dec
    v = v.astype(BF16)
    scores = _dot_nt(q.astype(BF16), k.astype(BF16)) * decay
    st = state_ref[h]
    y = _dot(scores.astype(BF16), v) + _dot((q * q_dec).astype(BF16), st.astype(BF16))
    state_ref[h] = s_dec * st + _dot_tn((k * k_dec).astype(BF16), v)
    yc = y - jnp.mean(y, axis=-1, keepdims=True)
    return yc * lax.rsqrt(jnp.mean(yc * yc, axis=-1, keepdims=True) + EPS) * gnw


def _inproj_kernel(x_ref, nw_ref, wa_ref, qnw_ref, kvnw_ref, wuq_ref, cr_ref, sr_ref, cm_ref, sm_ref, *rest,
                   prompt, seq_tiles):
    if prompt:
        (wuk_ref, wuv_ref, s0_ref, gnw_ref, ret_ref, sfin_ref, q_ref, ckv_ref, kpe_ref, kn_ref, kp_ref, v_ref,
         state_scr, dec_scr, qdec_scr, kdec_scr) = rest
    else:
        rq_ref, rk_ref, rv_ref, q_ref, ckv_ref, kpe_ref = rest
    qk_w = RET_HEADS * RET_DK
    v_w = RET_HEADS * RET_DV
    q_lora = wuq_ref.shape[0]
    kv_lora = ckv_ref.shape[1]
    o_rk, o_rv, o_cq = qk_w, 2 * qk_w, 2 * qk_w + v_w
    o_ckv = o_cq + q_lora
    o_kpe = o_ckv + kv_lora

    xn = _rms(x_ref[...], nw_ref[...]).astype(BF16)
    c, s = cr_ref[...], sr_ref[...]
    cr = jnp.concatenate([c, c], axis=1)
    sr = jnp.concatenate([-s, s], axis=1)
    c, s = cm_ref[...], sm_ref[...]
    z = jnp.zeros_like(s)
    cm = jnp.concatenate([c, c, c, c], axis=1)
    sa = jnp.concatenate([-s, z, -s, z], axis=1)
    sb = jnp.concatenate([z, s, z, s], axis=1)

    zq = _dot(xn, wa_ref[:, 0:qk_w])
    zk = _dot(xn, wa_ref[:, o_rk:o_rk + qk_w])
    rv = _dot(xn, wa_ref[:, o_rv:o_rv + v_w])
    k_scale = RET_DK ** -0.5
    rq = [_rope_full(zq[:, h * RET_DK:(h + 1) * RET_DK], cr, sr) for h in range(RET_HEADS)]
    rk = [_rope_full(zk[:, h * RET_DK:(h + 1) * RET_DK], cr, sr) * k_scale for h in range(RET_HEADS)]
    if prompt:
        i = pl.program_id(0)
        blk = dec_scr.shape[1]

        @pl.when(i == 0)
        def _():
            for h in range(RET_HEADS):
                dec_scr[h], qdec_scr[h], kdec_scr[h], _ = _ret_decay(blk, h)

        @pl.when(i % seq_tiles == 0)
        def _():
            state_scr[...] = s0_ref[...]

        for b in range(x_ref.shape[0] // blk):
            r = slice(b * blk, (b + 1) * blk)
            for h in range(RET_HEADS):
                sl = slice(h * RET_DV, (h + 1) * RET_DV)
                dec = (dec_scr[h], qdec_scr[h], kdec_scr[h], math.exp(math.log(1.0 - 2.0 ** (-5.0 - h)) * blk))
                ret_ref[r, sl] = _ret_block(rq[h][r], rk[h][r], rv[r, sl], state_scr, h, dec,
                                            gnw_ref[:, sl]).astype(BF16)

        @pl.when(i % seq_tiles == seq_tiles - 1)
        def _():
            sfin_ref[...] = state_scr[...]
    else:
        for h in range(RET_HEADS):
            sl = slice(h * RET_DK, (h + 1) * RET_DK)
            rq_ref[:, sl] = rq[h].astype(BF16)
            rk_ref[:, sl] = rk[h].astype(BF16)
        rv_ref[...] = rv.astype(BF16)

    cq = _dot(xn, wa_ref[:, o_cq:o_cq + q_lora])
    cqn = _rms(cq, qnw_ref[...]).astype(BF16)
    for h in range(MLA_HEADS):
        qh = _dot(cqn, wuq_ref[:, h * QK_PAD:(h + 1) * QK_PAD]) * SOFTMAX_C
        q_ref[:, h * QK_PAD:h * QK_PAD + LANES] = qh[:, :LANES].astype(BF16)
        q_ref[:, h * QK_PAD + LANES:(h + 1) * QK_PAD] = _rope_half(qh[:, LANES:], cm, sa, sb).astype(BF16)

    ckv = _rms(_dot(xn, wa_ref[:, o_ckv:o_ckv + kv_lora]), kvnw_ref[...])
    ckv_ref[...] = ckv
    kpe_pad = _rope_half(_dot(xn, wa_ref[:, o_kpe:o_kpe + LANES]), cm, sa, sb)
    kpe_ref[...] = kpe_pad[:, :ROPE_DIM]
    if prompt:
        c16 = ckv.astype(BF16)
        kn_ref[...] = _dot(c16, wuk_ref[...]).astype(BF16)
        v_ref[...] = _dot(c16, wuv_ref[...]).astype(BF16)
        kp_ref[...] = kpe_pad.astype(BF16)


def _inproj(x2, tabs, nw, wa, qnw, kvnw, wuq, prompt_args, *, tab_tiles):
    n, d = x2.shape
    tm = min(TOKEN_TILE, n)
    qk_w, v_w = RET_HEADS * RET_DK, RET_HEADS * RET_DV
    kv_lora = kvnw.shape[1]
    prompt = prompt_args is not None
    tab_specs = [pl.BlockSpec((tm, t.shape[1]), lambda i: (i % tab_tiles, 0)) for t in tabs]
    in_specs = [_row_spec(tm, d), _const_spec(nw.shape), _const_spec(wa.shape), _const_spec(qnw.shape),
                _const_spec(kvnw.shape), _const_spec(wuq.shape)] + tab_specs
    mla_outs = [(MLA_HEADS * QK_PAD, BF16), (kv_lora, F32), (ROPE_DIM, F32)]
    scratch = []
    if prompt:
        wuk, wuv, state0, gnw = prompt_args
        assert tm % RET_BLOCK == 0
        st_spec = pl.BlockSpec((None,) + state0.shape[1:], lambda i: (i // tab_tiles, 0, 0, 0))
        in_specs += [_const_spec(wuk.shape), _const_spec(wuv.shape), st_spec, _const_spec(gnw.shape)]
        row_outs = [(v_w, BF16)] + mla_outs + [(MLA_HEADS * NOPE_DIM, BF16), (LANES, BF16), (MLA_HEADS * V_DIM, BF16)]
        out_specs = [_row_spec(tm, w) for w, _ in row_outs]
        out_shape = [jax.ShapeDtypeStruct((n, w), dt) for w, dt in row_outs]
        out_specs.insert(1, st_spec)
        out_shape.insert(1, jax.ShapeDtypeStruct(state0.shape, F32))
        scratch = [pltpu.VMEM(state0.shape[1:], F32), pltpu.VMEM((RET_HEADS, RET_BLOCK, RET_BLOCK), F32),
                   pltpu.VMEM((RET_HEADS, RET_BLOCK, RET_DK), F32), pltpu.VMEM((RET_HEADS, RET_BLOCK, RET_DK), F32)]
    else:
        prompt_args = ()
        row_outs = [(qk_w, BF16), (qk_w, BF16), (v_w, BF16)] + mla_outs
        out_specs = [_row_spec(tm, w) for w, _ in row_outs]
        out_shape = [jax.ShapeDtypeStruct((n, w), dt) for w, dt in row_outs]
    return pl.pallas_call(
        functools.partial(_inproj_kernel, prompt=prompt, seq_tiles=tab_tiles),
        grid=(n // tm,),
        in_specs=in_specs,
        out_specs=out_specs,
        out_shape=out_shape,
        scratch_shapes=scratch,
        compiler_params=_params("arbitrary"),
        name="inproj",
    )(x2, nw, wa, qnw, kvnw, wuq, *tabs, *prompt_args)


def _retention_kernel(q_ref, k_ref, v_ref, s0_ref, gnw_ref, y_ref, sfin_ref, state_ref, *, blk, nblk):
    c = pl.program_id(1)

    @pl.when(c == 0)
    def _():
        state_ref[...] = s0_ref[...]

    for h in range(RET_HEADS):
        sl = slice(h * RET_DV, (h + 1) * RET_DV)
        q = q_ref[:, h * RET_DK:(h + 1) * RET_DK].astype(F32)
        k = k_ref[:, h * RET_DK:(h + 1) * RET_DK].astype(F32)
        y_ref[:, sl] = _ret_block(q, k, v_ref[:, sl], state_ref, h, _ret_decay(blk, h), gnw_ref[:, sl]).astype(BF16)

    @pl.when(c == nblk - 1)
    def _():
        sfin_ref[...] = state_ref[...]


def _retention(rq, rk, rv, state0, gnw, blk):
    b, s, _ = rq.shape
    nblk = s // blk
    qk_w, v_w = RET_HEADS * RET_DK, RET_HEADS * RET_DV
    st_spec = pl.BlockSpec((None, RET_HEADS, RET_DK, RET_DV), lambda i, c: (i, 0, 0, 0))
    return pl.pallas_call(
        functools.partial(_retention_kernel, blk=blk, nblk=nblk),
        grid=(b, nblk),
        in_specs=[pl.BlockSpec((None, blk, qk_w), lambda i, c: (i, c, 0)),
                  pl.BlockSpec((None, blk, qk_w), lambda i, c: (i, c, 0)),
                  pl.BlockSpec((None, blk, v_w), lambda i, c: (i, c, 0)),
                  st_spec,
                  pl.BlockSpec((1, v_w), lambda i, c: (0, 0))],
        out_specs=[pl.BlockSpec((None, blk, v_w), lambda i, c: (i, c, 0)), st_spec],
        out_shape=[jax.ShapeDtypeStruct((b, s, v_w), BF16),
                   jax.ShapeDtypeStruct((b, RET_HEADS, RET_DK, RET_DV), F32)],
        scratch_shapes=[pltpu.VMEM((RET_HEADS, RET_DK, RET_DV), F32)],
        compiler_params=_params("parallel", "arbitrary"),
        name="retention",
    )(rq, rk, rv, state0, gnw)


def _attn_prompt_kernel(q_ref, kn_ref, kp_ref, v_ref, o_ref, s_scr, mx_scr, m_scr, acc_scr, *, tq, tk):
    nq = q_ref.shape[0] // tq
    r = tq // tk
    ones = jnp.ones((tk, V_DIM), BF16)

    def rows(ref, t, n):
        return ref[pl.ds(pl.multiple_of(t * n, n), n), :]

    def lane_max(s):
        blocks = [s[:, c * LANES:(c + 1) * LANES] for c in range(tk // LANES)]
        while len(blocks) > 1:
            blocks = [jnp.maximum(a, b) for a, b in zip(blocks[::2], blocks[1::2])]
        return blocks[0]

    def scores(qi, t, row0=0):
        k = jnp.concatenate([rows(kn_ref, t, tk), rows(kp_ref, t, tk)], axis=1)
        q = q_ref[pl.ds(pl.multiple_of(qi * tq + row0, tk), tq - row0), :]
        s = _dot_nt(q, k)
        return s, lane_max(s)

    def park(sm, row0=0):
        s_scr[row0:, :], mx_scr[row0:, :] = sm

    def parked(row0=0):
        return s_scr[row0:, :], mx_scr[row0:, :]

    def consume(sm, t, row0=0, masked=False):
        s, mx = sm
        if masked:
            row = lax.broadcasted_iota(jnp.int32, s.shape, 0) // CHUNK
            col = lax.broadcasted_iota(jnp.int32, s.shape, 1) // CHUNK
            s = jnp.where(col <= row, s, NEG_BIG)
            mx = lane_max(s)
        m_old = m_scr[row0:, :]
        m_new = jnp.max(jnp.concatenate([mx, m_old], axis=1), axis=-1, keepdims=True)
        alpha = jnp.exp2(m_old - m_new)
        p = jnp.exp2(s - m_new).astype(BF16)
        v_ext = jnp.concatenate([rows(v_ref, t, tk), ones], axis=1)
        m_scr[row0:, :] = jnp.broadcast_to(m_new, m_old.shape)
        acc_scr[row0:, :] = jnp.concatenate([alpha, alpha], axis=1) * acc_scr[row0:, :] + _dot(p, v_ext)

    def restart():
        m_scr[...] = jnp.full_like(m_scr, NEG_BIG)
        acc_scr[...] = jnp.zeros_like(acc_scr)

    restart()
    park(scores(0, 0))

    def q_tile(qi, _):
        def trip(j, _):
            for u in range(0, r, 2):
                t = r * j + u
                odd = scores(qi, t + 1)
                consume(parked(), t)
                park(scores(qi, t + 2))
                consume(odd, t + 1)
            return 0

        lax.fori_loop(0, qi, trip, 0)
        for d in range(0, r, 2):
            t = r * qi + d
            odd = scores(qi, t + 1, (d + 1) * tk)
            consume(parked(d * tk), t, d * tk, masked=True)
            if d + 2 < r:
                park(scores(qi, t + 2, (d + 2) * tk), (d + 2) * tk)
            else:
                park(scores(jnp.minimum(qi + 1, nq - 1), 0))
            consume(odd, t + 1, (d + 1) * tk, masked=True)
        acc = acc_scr[...]
        o_ref[pl.ds(pl.multiple_of(qi * tq, tq), tq), :] = (acc[:, :V_DIM] / acc[:, V_DIM:]).astype(BF16)
        restart()
        return 0

    lax.fori_loop(0, nq, q_tile, 0)


def _attn_prompt(q, kn, kp, v):
    b, s, _ = q.shape
    tq, tk = ATTN_Q_TILE, ATTN_KV_TILE
    assert tq % (2 * tk) == 0 and s % tq == 0
    return pl.pallas_call(
        functools.partial(_attn_prompt_kernel, tq=tq, tk=tk),
        grid=(b, MLA_HEADS),
        in_specs=[pl.BlockSpec((None, s, QK_PAD), lambda i, h: (i, 0, h)),
                  pl.BlockSpec((None, s, NOPE_DIM), lambda i, h: (i, 0, h)),
                  pl.BlockSpec((None, s, LANES), lambda i, h: (i, 0, 0)),
                  pl.BlockSpec((None, s, V_DIM), lambda i, h: (i, 0, h))],
        out_specs=pl.BlockSpec((None, s, V_DIM), lambda i, h: (i, 0, h)),
        out_shape=jax.ShapeDtypeStruct((b, s, MLA_HEADS * V_DIM), BF16),
        scratch_shapes=[pltpu.VMEM((tq, tk), F32), pltpu.VMEM((tq, LANES), F32), pltpu.VMEM((tq, LANES), F32),
                        pltpu.VMEM((tq, 2 * V_DIM), F32)],
        compiler_params=_params("parallel", "parallel"),
        name="attn_prompt",
    )(q, kn, kp, v)


def _attn_sample_kernel(q_ref, cckv_ref, ckpe_ref, nckv_ref, nkpe_ref, wuk_ref, wuv_ref, o_ref):
    n = q_ref.shape[0]
    q_lat, q_rope = [], []
    for h in range(MLA_HEADS):
        q_nope = q_ref[:, h * QK_PAD:h * QK_PAD + NOPE_DIM]
        q_lat.append(_dot_nt(q_nope, wuk_ref[:, h * NOPE_DIM:(h + 1) * NOPE_DIM]).astype(BF16))
        q_rope.append(q_ref[:, h * QK_PAD + NOPE_DIM:(h + 1) * QK_PAD])
    q_all = jnp.concatenate([jnp.concatenate(q_lat, axis=0), jnp.concatenate(q_rope, axis=0)], axis=1)

    def keys(ckv_ref, kpe_ref):
        ckv = ckv_ref[...].astype(BF16)
        kpe = kpe_ref[...]
        kpe_pad = jnp.concatenate([kpe, jnp.zeros_like(kpe)], axis=1).astype(BF16)
        return ckv, jnp.concatenate([ckv, kpe_pad], axis=1)

    vc, kc = keys(cckv_ref, ckpe_ref)
    vn, kn = keys(nckv_ref, nkpe_ref)
    sc = _dot_nt(q_all, kc)
    sn = _dot_nt(q_all, kn)
    m = jnp.maximum(jnp.max(sc, axis=-1, keepdims=True), jnp.max(sn, axis=-1, keepdims=True))
    pc = jnp.exp2(sc - m)
    pn = jnp.exp2(sn - m)
    l = jnp.sum(pc, axis=-1, keepdims=True) + jnp.sum(pn, axis=-1, keepdims=True)
    o_lat = ((_dot(pc.astype(BF16), vc) + _dot(pn.astype(BF16), vn)) / l).astype(BF16)
    for h in range(MLA_HEADS):
        sl = slice(h * V_DIM, (h + 1) * V_DIM)
        o_ref[:, sl] = _dot(o_lat[h * n:(h + 1) * n], wuv_ref[:, sl]).astype(BF16)


def _attn_sample(q, cache_ckv, cache_kpe, new_ckv, new_kpe, wuk, wuv):
    b, s, _ = q.shape

    def per_stream(a):
        return pl.BlockSpec((None,) + a.shape[1:], lambda i: (i, 0, 0))

    return pl.pallas_call(
        _attn_sample_kernel,
        grid=(b,),
        in_specs=[per_stream(q), per_stream(cache_ckv), per_stream(cache_kpe), per_stream(new_ckv),
                  per_stream(new_kpe), _const_spec(wuk.shape), _const_spec(wuv.shape)],
        out_specs=pl.BlockSpec((None, s, MLA_HEADS * V_DIM), lambda i: (i, 0, 0)),
        out_shape=jax.ShapeDtypeStruct((b, s, MLA_HEADS * V_DIM), BF16),
        compiler_params=_params("parallel"),
        name="attn_sample",
    )(q, cache_ckv, cache_kpe, new_ckv, new_kpe, wuk, wuv)


def _mix_kernel(x_ref, ret_ref, mla_ref, nw_ref, wg_ref, wro_ref, wmo_ref, wo_ref, h_ref):
    d = x_ref.shape[1]
    v_w = ret_ref.shape[1]
    x = x_ref[...]
    xn = _rms(x, nw_ref[...]).astype(BF16)
    rg = _dot(xn, wg_ref[:, 0:v_w])
    ret_b = _dot((ret_ref[...].astype(F32) * _silu(rg)).astype(BF16), wro_ref[...])
    ga = _dot(xn, wg_ref[:, v_w:v_w + d])
    merged = _sigmoid(ga) * ret_b
    mla_b = _dot(mla_ref[...], wmo_ref[...])
    gb = _dot(xn, wg_ref[:, v_w + d:v_w + 2 * d])
    merged = merged + _sigmoid(gb) * mla_b
    h_ref[...] = x + _dot(merged.astype(BF16), wo_ref[...])


def _mix(x2, ret2, mla2, nw, wg, wro, wmo, wo):
    n, d = x2.shape
    tm = min(TOKEN_TILE, n)
    return pl.pallas_call(
        _mix_kernel,
        grid=(n // tm,),
        in_specs=[_row_spec(tm, d), _row_spec(tm, ret2.shape[1]), _row_spec(tm, mla2.shape[1]),
                  _const_spec(nw.shape), _const_spec(wg.shape), _const_spec(wro.shape),
                  _const_spec(wmo.shape), _const_spec(wo.shape)],
        out_specs=_row_spec(tm, d),
        out_shape=jax.ShapeDtypeStruct((n, d), F32),
        compiler_params=_params("parallel"),
        name="mix",
    )(x2, ret2, mla2, nw, wg, wro, wmo, wo)


def _ffn_kernel(h_ref, nw_ref, wgu_ref, wd_ref, fw_ref, y_ref, *, final_norm):
    h = h_ref[...]
    hn = _rms(h, nw_ref[...]).astype(BF16)
    acc = h
    d_ff = wd_ref.shape[0]
    for c in range(d_ff // FF_CHUNK):
        sl = slice(c * FF_CHUNK, (c + 1) * FF_CHUNK)
        g = _dot(hn, wgu_ref[:, sl])
        u = _dot(hn, wgu_ref[:, d_ff + c * FF_CHUNK:d_ff + (c + 1) * FF_CHUNK])
        acc = acc + _dot((_silu(g) * u).astype(BF16), wd_ref[sl, :])
    y_ref[...] = _rms(acc, fw_ref[...]) if final_norm else acc


def _ffn(h2, nw, wgu, wd, fw, final_norm):
    n, d = h2.shape
    tm = min(TOKEN_TILE, n)
    return pl.pallas_call(
        functools.partial(_ffn_kernel, final_norm=final_norm),
        grid=(n // tm,),
        in_specs=[_row_spec(tm, d), _const_spec(nw.shape), _const_spec(wgu.shape), _const_spec(wd.shape),
                  _const_spec(fw.shape)],
        out_specs=_row_spec(tm, d),
        out_shape=jax.ShapeDtypeStruct((n, d), F32),
        compiler_params=_params("parallel"),
        name="ffn",
    )(h2, nw, wgu, wd, fw)


def _rope_tables(start, n, reps):
    pos = np.arange(start, start + n, dtype=np.float64)[:, None]

    def cs(d):
        inv = ROPE_BASE ** (-np.arange(0, d, 2, dtype=np.float64) / d)
        ang = pos * inv[None, :]
        return [np.tile(f(ang).astype(np.float32), (reps, 1)) for f in (np.cos, np.sin)]

    return tuple(jnp.asarray(t) for t in cs(RET_DK) + cs(ROPE_DIM))


def _prep_layer_weights(w_in, w_uq, w_ukv, w_ret_o, w_mla_o, w_out, w_gate_up, w_down):
    d = w_in.shape[0]
    qk_w, v_w = RET_HEADS * RET_DK, RET_HEADS * RET_DV
    q_lora = w_uq.shape[0]
    kv_lora = w_ukv.shape[0]
    o_rg = 2 * qk_w + v_w
    o_cq = o_rg + v_w
    o_kpe = o_cq + q_lora + kv_lora
    o_ga = o_kpe + ROPE_DIM
    wa = jnp.concatenate([w_in[:, :o_rg], w_in[:, o_cq:o_ga], jnp.zeros((d, LANES - ROPE_DIM), w_in.dtype)],
                         axis=1).astype(BF16)
    wg = jnp.concatenate([w_in[:, o_rg:o_cq], w_in[:, o_ga:]], axis=1).astype(BF16)
    head_w = NOPE_DIM + ROPE_DIM
    wuq = jnp.pad(w_uq.reshape(q_lora, MLA_HEADS, head_w), ((0, 0), (0, 0), (0, QK_PAD - head_w)))
    wuq = wuq.reshape(q_lora, MLA_HEADS * QK_PAD).astype(BF16)
    wkv = w_ukv.reshape(kv_lora, MLA_HEADS, NOPE_DIM + V_DIM)
    wuk = wkv[:, :, :NOPE_DIM].reshape(kv_lora, MLA_HEADS * NOPE_DIM).astype(BF16)
    wuv = wkv[:, :, NOPE_DIM:].reshape(kv_lora, MLA_HEADS * V_DIM).astype(BF16)
    assert w_down.shape[0] % FF_CHUNK == 0
    return dict(wa=wa, wg=wg, wuq=wuq, wuk=wuk, wuv=wuv, wro=w_ret_o.astype(BF16), wmo=w_mla_o.astype(BF16),
                wo=w_out.astype(BF16), wgu=w_gate_up.astype(BF16), wd=w_down.astype(BF16))


def _layer(x, tabs, tab_tiles, lw, norms, state0, cache, fw, final_norm):
    b, s, d = x.shape
    n = b * s
    x2 = x.reshape(n, d)
    nmw, qnw, kvnw, gnw, nfw = norms
    wukv = (lw["wuk"], lw["wuv"])
    if cache is None:
        ret, s_fin, q, ckv, kpe, *kv = _inproj(x2, tabs, nmw, lw["wa"], qnw, kvnw, lw["wuq"],
                                               wukv + (state0, gnw), tab_tiles=tab_tiles)
    else:
        rq, rk, rv, q, ckv, kpe = _inproj(x2, tabs, nmw, lw["wa"], qnw, kvnw, lw["wuq"], None, tab_tiles=tab_tiles)
        ret, s_fin = _retention(rq.reshape(b, s, -1), rk.reshape(b, s, -1), rv.reshape(b, s, -1), state0, gnw, s)
    ckv3 = ckv.reshape(b, s, -1)
    kpe3 = kpe.reshape(b, s, ROPE_DIM)
    q3 = q.reshape(b, s, MLA_HEADS * QK_PAD)
    if cache is None:
        mla = _attn_prompt(q3, *(a.reshape(b, s, -1) for a in kv))
    else:
        mla = _attn_sample(q3, cache[0].astype(F32), cache[1].astype(F32), ckv3, kpe3, *wukv)
    h2 = _mix(x2, ret.reshape(n, -1), mla.reshape(n, -1), nmw, lw["wg"], lw["wro"], lw["wmo"], lw["wo"])
    y2 = _ffn(h2, nfw, lw["wgu"], lw["wd"], fw, final_norm)
    return y2.reshape(b, s, d), (ckv3, kpe3, s_fin)


def kernel(x_prompt, x_sample, cache_ckv, cache_kpe, state_ret, norm_mix_w, w_in, q_norm_w, w_uq, kv_norm_w, w_ukv, ret_gn_w, w_ret_o, w_mla_o, w_out, norm_ffn_w, w_gate_up, w_down, norm_final_w):
    depth = w_in.shape[0]
    bp, sp, _ = x_prompt.shape
    bs, ss, _ = x_sample.shape
    past = cache_ckv.shape[2]
    assert sp % TOKEN_TILE == 0 and sp % ATTN_Q_TILE == 0 and (bs * ss) % min(TOKEN_TILE, bs * ss) == 0
    assert ss == CHUNK, "the sample group is one streaming chunk"

    tabs_p = _rope_tables(0, sp, 1)
    tabs_s = _rope_tables(past, ss, bs)
    fw = norm_final_w.reshape(1, -1)
    state0_p = jnp.zeros((bp, RET_HEADS, RET_DK, RET_DV), F32)

    hp, hs = x_prompt, x_sample
    outs = [[] for _ in range(6)]
    for l in range(depth):
        lw = _prep_layer_weights(w_in[l], w_uq[l], w_ukv[l], w_ret_o[l], w_mla_o[l], w_out[l], w_gate_up[l],
                                 w_down[l])
        norms = (norm_mix_w[l].reshape(1, -1), q_norm_w[l].reshape(1, -1), kv_norm_w[l].reshape(1, -1),
                 ret_gn_w[l].reshape(1, -1), norm_ffn_w[l].reshape(1, -1))
        final = l == depth - 1
        hp, (a, b_, c) = _layer(hp, tabs_p, sp // TOKEN_TILE, lw, norms, state0_p, None, fw, final)
        hs, (d_, e, f) = _layer(hs, tabs_s, 1, lw, norms, state_ret[l].astype(F32),
                                (cache_ckv[l], cache_kpe[l]), fw, final)
        for lst, val in zip(outs, (a, b_, c, d_, e, f)):
            lst.append(val)
    return (hp, hs) + tuple(o[0][None] if depth == 1 else jnp.stack(o) for o in outs)
```

```python
import functools
import math

import jax
import jax.numpy as jnp
import numpy as np
from jax import lax
from jax.experimental import pallas as pl
from jax.experimental.pallas import tpu as pltpu

F32 = jnp.float32
BF16 = jnp.bfloat16

CHUNK = 64
RET_HEADS = 4
RET_DK = 128
RET_DV = 256
MLA_HEADS = 8
NOPE_DIM = 128
ROPE_DIM = 64
V_DIM = 128
ROPE_BASE = 10000.0
EPS = 1e-6

LANES = 128
QK_PAD = 2 * LANES
V7X_VMEM_BYTES = 64 * 2**20
VMEM_LIMIT = V7X_VMEM_BYTES * 7 // 8
TOKEN_TILE = 512
WIDE_TOKEN_TILE = 1024
WEIGHT_ROW_BLOCK = 128
RET_BLOCK = 256
ATTN_Q_TILE = 2048
ATTN_KV_TILE = 512
FF_CHUNK = 256
NEG_BIG = -1e30
SOFTMAX_C = (NOPE_DIM + ROPE_DIM) ** -0.5 * math.log2(math.e)

_dot = functools.partial(jnp.dot, preferred_element_type=F32)


def _dot_nt(a, b):
    return lax.dot_general(a, b, (((1,), (1,)), ((), ())), preferred_element_type=F32)


def _dot_tn(a, b):
    return lax.dot_general(a, b, (((0,), (0,)), ((), ())), preferred_element_type=F32)


def _rms(x, g):
    ms = jnp.mean(x * x, axis=-1, keepdims=True)
    return x * lax.rsqrt(ms + EPS) * g


def _sigmoid(x):
    return 1.0 / (1.0 + jnp.exp(-x))


def _silu(x):
    return x * _sigmoid(x)


def _params(*sem):
    return pltpu.CompilerParams(dimension_semantics=sem, vmem_limit_bytes=VMEM_LIMIT)


def _const_spec(shape):
    return pl.BlockSpec(shape, lambda *_: (0,) * len(shape), pipeline_mode=pl.Buffered(1))


def _row_spec(tm, width):
    return pl.BlockSpec((tm, width), lambda i: (i, 0))


def _rope_full(x, cos2, sin2):
    return x * cos2 + pltpu.roll(x, LANES // 2, 1) * sin2


def _rope_half(x, c, sa, sb):
    q = ROPE_DIM // 2
    return x * c + pltpu.roll(x, LANES - q, 1) * sa + pltpu.roll(x, q, 1) * sb


def _ret_decay(blk, h):
    lg = math.log(1.0 - 2.0 ** (-5.0 - h))
    ri = lax.broadcasted_iota(jnp.int32, (blk, blk), 0)
    ci = lax.broadcasted_iota(jnp.int32, (blk, blk), 1)
    diff = (ri - ci).astype(F32)
    n = lax.broadcasted_iota(jnp.int32, (blk, RET_DK), 0).astype(F32)
    decay = jnp.where(diff >= 0, jnp.exp(lg * jnp.maximum(diff, 0.0)), 0.0)
    return decay, jnp.exp(lg * (n + 1.0)), jnp.exp(lg * (blk - 1.0 - n)), math.exp(lg * blk)


def _ret_block(q, k, v, state_ref, h, dec, gnw):
    decay, q_dec, k_dec, s_dec = dec
    v = v.astype(BF16)
    scores = _dot_nt(q.astype(BF16), k.astype(BF16)) * decay
    st = state_ref[h]
    y = _dot(scores.astype(BF16), v) + _dot((q * q_dec).astype(BF16), st.astype(BF16))
    state_ref[h] = s_dec * st + _dot_tn((k * k_dec).astype(BF16), v)
    yc = y - jnp.mean(y, axis=-1, keepdims=True)
    return yc * lax.rsqrt(jnp.mean(yc * yc, axis=-1, keepdims=True) + EPS) * gnw


def _inproj_kernel(x_ref, nw_ref, wa_ref, qnw_ref, kvnw_ref, wuq_ref, cr_ref, sr_ref, cm_ref, sm_ref, *rest,
                   prompt, seq_tiles):
    if prompt:
        (wuk_ref, wuv_ref, s0_ref, gnw_ref, ret_ref, sfin_ref, q_ref, ckv_ref, kpe_ref, kn_ref, kp_ref, v_ref,
         state_scr, dec_scr, qdec_scr, kdec_scr) = rest
    else:
        rq_ref, rk_ref, rv_ref, q_ref, ckv_ref, kpe_ref = rest
    qk_w = RET_HEADS * RET_DK
    v_w = RET_HEADS * RET_DV
    q_lora = wuq_ref.shape[0]
    kv_lora = ckv_ref.shape[1]
    o_rk, o_rv, o_cq = qk_w, 2 * qk_w, 2 * qk_w + v_w
    o_ckv = o_cq + q_lora
    o_kpe = o_ckv + kv_lora

    xn = _rms(x_ref[...], nw_ref[...]).astype(BF16)
    c, s = cr_ref[...], sr_ref[...]
    cr = jnp.concatenate([c, c], axis=1)
    sr = jnp.concatenate([-s, s], axis=1)
    c, s = cm_ref[...], sm_ref[...]
    z = jnp.zeros_like(s)
    cm = jnp.concatenate([c, c, c, c], axis=1)
    sa = jnp.concatenate([-s, z, -s, z], axis=1)
    sb = jnp.concatenate([z, s, z, s], axis=1)

    zq = _dot(xn, wa_ref[:, 0:qk_w])
    zk = _dot(xn, wa_ref[:, o_rk:o_rk + qk_w])
    rv = _dot(xn, wa_ref[:, o_rv:o_rv + v_w])
    k_scale = RET_DK ** -0.5
    rq = [_rope_full(zq[:, h * RET_DK:(h + 1) * RET_DK], cr, sr) for h in range(RET_HEADS)]
    rk = [_rope_full(zk[:, h * RET_DK:(h + 1) * RET_DK], cr, sr) * k_scale for h in range(RET_HEADS)]
    if prompt:
        i = pl.program_id(0)
        blk = dec_scr.shape[1]

        @pl.when(i == 0)
        def _():
            for h in range(RET_HEADS):
                dec_scr[h], qdec_scr[h], kdec_scr[h], _ = _ret_decay(blk, h)

        @pl.when(i % seq_tiles == 0)
        def _():
            state_scr[...] = s0_ref[...]

        for b in range(x_ref.shape[0] // blk):
            r = slice(b * blk, (b + 1) * blk)
            for h in range(RET_HEADS):
                sl = slice(h * RET_DV, (h + 1) * RET_DV)
                dec = (dec_scr[h], qdec_scr[h], kdec_scr[h], math.exp(math.log(1.0 - 2.0 ** (-5.0 - h)) * blk))
                ret_ref[r, sl] = _ret_block(rq[h][r], rk[h][r], rv[r, sl], state_scr, h, dec,
                                            gnw_ref[:, sl]).astype(BF16)

        @pl.when(i % seq_tiles == seq_tiles - 1)
        def _():
            sfin_ref[...] = state_scr[...]
    else:
        for h in range(RET_HEADS):
            sl = slice(h * RET_DK, (h + 1) * RET_DK)
            rq_ref[:, sl] = rq[h].astype(BF16)
            rk_ref[:, sl] = rk[h].astype(BF16)
        rv_ref[...] = rv.astype(BF16)

    cq = _dot(xn, wa_ref[:, o_cq:o_cq + q_lora])
    cqn = _rms(cq, qnw_ref[...]).astype(BF16)
    for h in range(MLA_HEADS):
        qh = _dot(cqn, wuq_ref[:, h * QK_PAD:(h + 1) * QK_PAD]) * SOFTMAX_C
        q_ref[:, h * QK_PAD:h * QK_PAD + LANES] = qh[:, :LANES].astype(BF16)
        q_ref[:, h * QK_PAD + LANES:(h + 1) * QK_PAD] = _rope_half(qh[:, LANES:], cm, sa, sb).astype(BF16)

    ckv = _rms(_dot(xn, wa_ref[:, o_ckv:o_ckv + kv_lora]), kvnw_ref[...])
    ckv_ref[...] = ckv
    kpe_pad = _rope_half(_dot(xn, wa_ref[:, o_kpe:o_kpe + LANES]), cm, sa, sb)
    kpe_ref[...] = kpe_pad[:, :ROPE_DIM]
    if prompt:
        c16 = ckv.astype(BF16)
        kn_ref[...] = _dot(c16, wuk_ref[...]).astype(BF16)
        v_ref[...] = _dot(c16, wuv_ref[...]).astype(BF16)
        kp_ref[...] = kpe_pad.astype(BF16)


def _inproj(x2, tabs, nw, wa, qnw, kvnw, wuq, prompt_args, *, tab_tiles):
    n, d = x2.shape
    tm = min(TOKEN_TILE, n)
    qk_w, v_w = RET_HEADS * RET_DK, RET_HEADS * RET_DV
    kv_lora = kvnw.shape[1]
    prompt = prompt_args is not None
    tab_specs = [pl.BlockSpec((tm, t.shape[1]), lambda i: (i % tab_tiles, 0)) for t in tabs]
    in_specs = [_row_spec(tm, d), _const_spec(nw.shape), _const_spec(wa.shape), _const_spec(qnw.shape),
                _const_spec(kvnw.shape), _const_spec(wuq.shape)] + tab_specs
    mla_outs = [(MLA_HEADS * QK_PAD, BF16), (kv_lora, F32), (ROPE_DIM, F32)]
    scratch = []
    if prompt:
        wuk, wuv, state0, gnw = prompt_args
        assert tm % RET_BLOCK == 0
        st_spec = pl.BlockSpec((None,) + state0.shape[1:], lambda i: (i // tab_tiles, 0, 0, 0))
        in_specs += [_const_spec(wuk.shape), _const_spec(wuv.shape), st_spec, _const_spec(gnw.shape)]
        row_outs = [(v_w, BF16)] + mla_outs + [(MLA_HEADS * NOPE_DIM, BF16), (LANES, BF16), (MLA_HEADS * V_DIM, BF16)]
        out_specs = [_row_spec(tm, w) for w, _ in row_outs]
        out_shape = [jax.ShapeDtypeStruct((n, w), dt) for w, dt in row_outs]
        out_specs.insert(1, st_spec)
        out_shape.insert(1, jax.ShapeDtypeStruct(state0.shape, F32))
        scratch = [pltpu.VMEM(state0.shape[1:], F32), pltpu.VMEM((RET_HEADS, RET_BLOCK, RET_BLOCK), F32),
                   pltpu.VMEM((RET_HEADS, RET_BLOCK, RET_DK), F32), pltpu.VMEM((RET_HEADS, RET_BLOCK, RET_DK), F32)]
    else:
        prompt_args = ()
        row_outs = [(qk_w, BF16), (qk_w, BF16), (v_w, BF16)] + mla_outs
        out_specs = [_row_spec(tm, w) for w, _ in row_outs]
        out_shape = [jax.ShapeDtypeStruct((n, w), dt) for w, dt in row_outs]
    return pl.pallas_call(
        functools.partial(_inproj_kernel, prompt=prompt, seq_tiles=tab_tiles),
        grid=(n // tm,),
        in_specs=in_specs,
        out_specs=out_specs,
        out_shape=out_shape,
        scratch_shapes=scratch,
        compiler_params=_params("arbitrary"),
        name="inproj",
    )(x2, nw, wa, qnw, kvnw, wuq, *tabs, *prompt_args)


def _retention_kernel(q_ref, k_ref, v_ref, s0_ref, gnw_ref, y_ref, sfin_ref, state_ref, *, blk, nblk):
    c = pl.program_id(1)

    @pl.when(c == 0)
    def _():
        state_ref[...] = s0_ref[...]

    for h in range(RET_HEADS):
        sl = slice(h * RET_DV, (h + 1) * RET_DV)
        q = q_ref[:, h * RET_DK:(h + 1) * RET_DK].astype(F32)
        k = k_ref[:, h * RET_DK:(h + 1) * RET_DK].astype(F32)
        y_ref[:, sl] = _ret_block(q, k, v_ref[:, sl], state_ref, h, _ret_decay(blk, h), gnw_ref[:, sl]).astype(BF16)

    @pl.when(c == nblk - 1)
    def _():
        sfin_ref[...] = state_ref[...]


def _retention(rq, rk, rv, state0, gnw, blk):
    b, s, _ = rq.shape
    nblk = s // blk
    qk_w, v_w = RET_HEADS * RET_DK, RET_HEADS * RET_DV
    st_spec = pl.BlockSpec((None, RET_HEADS, RET_DK, RET_DV), lambda i, c: (i, 0, 0, 0))
    return pl.pallas_call(
        functools.partial(_retention_kernel, blk=blk, nblk=nblk),
        grid=(b, nblk),
        in_specs=[pl.BlockSpec((None, blk, qk_w), lambda i, c: (i, c, 0)),
                  pl.BlockSpec((None, blk, qk_w), lambda i, c: (i, c, 0)),
                  pl.BlockSpec((None, blk, v_w), lambda i, c: (i, c, 0)),
                  st_spec,
                  pl.BlockSpec((1, v_w), lambda i, c: (0, 0))],
        out_specs=[pl.BlockSpec((None, blk, v_w), lambda i, c: (i, c, 0)), st_spec],
        out_shape=[jax.ShapeDtypeStruct((b, s, v_w), BF16),
                   jax.ShapeDtypeStruct((b, RET_HEADS, RET_DK, RET_DV), F32)],
        scratch_shapes=[pltpu.VMEM((RET_HEADS, RET_DK, RET_DV), F32)],
        compiler_params=_params("parallel", "arbitrary"),
        name="retention",
    )(rq, rk, rv, state0, gnw)


def _attn_prompt_kernel(q_ref, kn_ref, kp_ref, v_ref, o_ref, s_scr, mx_scr, m_scr, acc_scr, *, tq, tk):
    nq = q_ref.shape[0] // tq
    r = tq // tk
    ones = jnp.ones((tk, V_DIM), BF16)

    def rows(ref, t, n):
        return ref[pl.ds(pl.multiple_of(t * n, n), n), :]

    def lane_max(s):
        blocks = [s[:, c * LANES:(c + 1) * LANES] for c in range(tk // LANES)]
        while len(blocks) > 1:
            blocks = [jnp.maximum(a, b) for a, b in zip(blocks[::2], blocks[1::2])]
        return blocks[0]

    def scores(qi, t, row0=0):
        k = jnp.concatenate([rows(kn_ref, t, tk), rows(kp_ref, t, tk)], axis=1)
        q = q_ref[pl.ds(pl.multiple_of(qi * tq + row0, tk), tq - row0), :]
        s = _dot_nt(q, k)
        return s, lane_max(s)

    def park(sm, row0=0):
        s_scr[row0:, :], mx_scr[row0:, :] = sm

    def parked(row0=0):
        return s_scr[row0:, :], mx_scr[row0:, :]

    def consume(sm, t, row0=0, masked=False):
        s, mx = sm
        if masked:
            row = lax.broadcasted_iota(jnp.int32, s.shape, 0) // CHUNK
            col = lax.broadcasted_iota(jnp.int32, s.shape, 1) // CHUNK
            s = jnp.where(col <= row, s, NEG_BIG)
            mx = lane_max(s)
        m_old = m_scr[row0:, :]
        m_new = jnp.max(jnp.concatenate([mx, m_old], axis=1), axis=-1, keepdims=True)
        alpha = jnp.exp2(m_old - m_new)
        p = jnp.exp2(s - m_new).astype(BF16)
        v_ext = jnp.concatenate([rows(v_ref, t, tk), ones], axis=1)
        m_scr[row0:, :] = jnp.broadcast_to(m_new, m_old.shape)
        acc_scr[row0:, :] = jnp.concatenate([alpha, alpha], axis=1) * acc_scr[row0:, :] + _dot(p, v_ext)

    def restart():
        m_scr[...] = jnp.full_like(m_scr, NEG_BIG)
        acc_scr[...] = jnp.zeros_like(acc_scr)

    restart()
    park(scores(0, 0))

    def q_tile(qi, _):
        def trip(j, _):
            for u in range(0, r, 2):
                t = r * j + u
                odd = scores(qi, t + 1)
                consume(parked(), t)
                park(scores(qi, t + 2))
                consume(odd, t + 1)
            return 0

        lax.fori_loop(0, qi, trip, 0)
        for d in range(0, r, 2):
            t = r * qi + d
            odd = scores(qi, t + 1, (d + 1) * tk)
            consume(parked(d * tk), t, d * tk, masked=True)
            if d + 2 < r:
                park(scores(qi, t + 2, (d + 2) * tk), (d + 2) * tk)
            else:
                park(scores(jnp.minimum(qi + 1, nq - 1), 0))
            consume(odd, t + 1, (d + 1) * tk, masked=True)
        acc = acc_scr[...]
        o_ref[pl.ds(pl.multiple_of(qi * tq, tq), tq), :] = (acc[:, :V_DIM] / acc[:, V_DIM:]).astype(BF16)
        restart()
        return 0

    lax.fori_loop(0, nq, q_tile, 0)


def _attn_prompt(q, kn, kp, v):
    b, s, _ = q.shape
    tq, tk = ATTN_Q_TILE, ATTN_KV_TILE
    assert tq % (2 * tk) == 0 and s % tq == 0
    return pl.pallas_call(
        functools.partial(_attn_prompt_kernel, tq=tq, tk=tk),
        grid=(b, MLA_HEADS),
        in_specs=[pl.BlockSpec((None, s, QK_PAD), lambda i, h: (i, 0, h)),
                  pl.BlockSpec((None, s, NOPE_DIM), lambda i, h: (i, 0, h)),
                  pl.BlockSpec((None, s, LANES), lambda i, h: (i, 0, 0)),
                  pl.BlockSpec((None, s, V_DIM), lambda i, h: (i, 0, h))],
        out_specs=pl.BlockSpec((None, s, V_DIM), lambda i, h: (i, 0, h)),
        out_shape=jax.ShapeDtypeStruct((b, s, MLA_HEADS * V_DIM), BF16),
        scratch_shapes=[pltpu.VMEM((tq, tk), F32), pltpu.VMEM((tq, LANES), F32), pltpu.VMEM((tq, LANES), F32),
                        pltpu.VMEM((tq, 2 * V_DIM), F32)],
        compiler_params=_params("parallel", "parallel"),
        name="attn_prompt",
    )(q, kn, kp, v)


def _attn_sample_kernel(q_ref, cckv_ref, ckpe_ref, nckv_ref, nkpe_ref, wuk_ref, wuv_ref, o_ref):
    n = q_ref.shape[0]
    q_lat, q_rope = [], []
    for h in range(MLA_HEADS):
        q_nope = q_ref[:, h * QK_PAD:h * QK_PAD + NOPE_DIM]
        q_lat.append(_dot_nt(q_nope, wuk_ref[:, h * NOPE_DIM:(h + 1) * NOPE_DIM]).astype(BF16))
        q_rope.append(q_ref[:, h * QK_PAD + NOPE_DIM:(h + 1) * QK_PAD])
    q_all = jnp.concatenate([jnp.concatenate(q_lat, axis=0), jnp.concatenate(q_rope, axis=0)], axis=1)

    def keys(ckv_ref, kpe_ref):
        ckv = ckv_ref[...].astype(BF16)
        kpe = kpe_ref[...]
        kpe_pad = jnp.concatenate([kpe, jnp.zeros_like(kpe)], axis=1).astype(BF16)
        return ckv, jnp.concatenate([ckv, kpe_pad], axis=1)

    vc, kc = keys(cckv_ref, ckpe_ref)
    vn, kn = keys(nckv_ref, nkpe_ref)
    sc = _dot_nt(q_all, kc)
    sn = _dot_nt(q_all, kn)
    m = jnp.maximum(jnp.max(sc, axis=-1, keepdims=True), jnp.max(sn, axis=-1, keepdims=True))
    pc = jnp.exp2(sc - m)
    pn = jnp.exp2(sn - m)
    l = jnp.sum(pc, axis=-1, keepdims=True) + jnp.sum(pn, axis=-1, keepdims=True)
    o_lat = ((_dot(pc.astype(BF16), vc) + _dot(pn.astype(BF16), vn)) / l).astype(BF16)
    for h in range(MLA_HEADS):
        sl = slice(h * V_DIM, (h + 1) * V_DIM)
        o_ref[:, sl] = _dot(o_lat[h * n:(h + 1) * n], wuv_ref[:, sl]).astype(BF16)


def _attn_sample(q, cache_ckv, cache_kpe, new_ckv, new_kpe, wuk, wuv):
    b, s, _ = q.shape

    def per_stream(a):
        return pl.BlockSpec((None,) + a.shape[1:], lambda i: (i, 0, 0))

    return pl.pallas_call(
        _attn_sample_kernel,
        grid=(b,),
        in_specs=[per_stream(q), per_stream(cache_ckv), per_stream(cache_kpe), per_stream(new_ckv),
                  per_stream(new_kpe), _const_spec(wuk.shape), _const_spec(wuv.shape)],
        out_specs=pl.BlockSpec((None, s, MLA_HEADS * V_DIM), lambda i: (i, 0, 0)),
        out_shape=jax.ShapeDtypeStruct((b, s, MLA_HEADS * V_DIM), BF16),
        compiler_params=_params("parallel"),
        name="attn_sample",
    )(q, cache_ckv, cache_kpe, new_ckv, new_kpe, wuk, wuv)


def _mix_kernel(x_ref, ret_ref, mla_ref, nw_ref, wg_ref, wro_ref, wmo_ref, wo_ref, h_ref):
    d = x_ref.shape[1]
    v_w = ret_ref.shape[1]
    x = x_ref[...]
    xn = _rms(x, nw_ref[...]).astype(BF16)
    rg = _dot(xn, wg_ref[:, 0:v_w])
    ret_b = _dot((ret_ref[...].astype(F32) * _silu(rg)).astype(BF16), wro_ref[...])
    ga = _dot(xn, wg_ref[:, v_w:v_w + d])
    merged = _sigmoid(ga) * ret_b
    mla_b = _dot(mla_ref[...], wmo_ref[...])
    gb = _dot(xn, wg_ref[:, v_w + d:v_w + 2 * d])
    merged = merged + _sigmoid(gb) * mla_b
    h_ref[...] = x + _dot(merged.astype(BF16), wo_ref[...])


def _mix(x2, ret2, mla2, nw, wg, wro, wmo, wo):
    n, d = x2.shape
    tm = min(WIDE_TOKEN_TILE, n)
    return pl.pallas_call(
        _mix_kernel,
        grid=(n // tm,),
        in_specs=[_row_spec(tm, d), _row_spec(tm, ret2.shape[1]), _row_spec(tm, mla2.shape[1]),
                  _const_spec(nw.shape), _const_spec(wg.shape), _const_spec(wro.shape),
                  _const_spec(wmo.shape), _const_spec(wo.shape)],
        out_specs=_row_spec(tm, d),
        out_shape=jax.ShapeDtypeStruct((n, d), F32),
        compiler_params=_params("parallel"),
        name="mix",
    )(x2, ret2, mla2, nw, wg, wro, wmo, wo)


def _ffn_kernel(h_ref, nw_ref, wgu_ref, wd_ref, fw_ref, y_ref, *, final_norm):
    h = h_ref[...]
    hn = _rms(h, nw_ref[...]).astype(BF16)
    acc = h
    d_ff = wd_ref.shape[0]
    for c in range(d_ff // FF_CHUNK):
        sl = slice(c * FF_CHUNK, (c + 1) * FF_CHUNK)
        g = _dot(hn, wgu_ref[:, sl])
        u = _dot(hn, wgu_ref[:, d_ff + c * FF_CHUNK:d_ff + (c + 1) * FF_CHUNK])
        acc = acc + _dot((_silu(g) * u).astype(BF16), wd_ref[sl, :])
    y_ref[...] = _rms(acc, fw_ref[...]) if final_norm else acc


def _ffn(h2, nw, wgu, wd, fw, final_norm):
    n, d = h2.shape
    tm = min(WIDE_TOKEN_TILE, n)
    return pl.pallas_call(
        functools.partial(_ffn_kernel, final_norm=final_norm),
        grid=(n // tm,),
        in_specs=[_row_spec(tm, d), _const_spec(nw.shape), _const_spec(wgu.shape), _const_spec(wd.shape),
                  _const_spec(fw.shape)],
        out_specs=_row_spec(tm, d),
        out_shape=jax.ShapeDtypeStruct((n, d), F32),
        compiler_params=_params("parallel"),
        name="ffn",
    )(h2, nw, wgu, wd, fw)


def _rope_tables(start, n, reps):
    pos = np.arange(start, start + n, dtype=np.float64)[:, None]

    def cs(d):
        inv = ROPE_BASE ** (-np.arange(0, d, 2, dtype=np.float64) / d)
        ang = pos * inv[None, :]
        return [np.tile(f(ang).astype(np.float32), (reps, 1)) for f in (np.cos, np.sin)]

    return tuple(jnp.asarray(t) for t in cs(RET_DK) + cs(ROPE_DIM))


def _regroup_kernel(w_ref, *out_refs, plans):
    for o_ref, plan in zip(out_refs, plans):
        off = 0
        for start, width in plan:
            if start is None:
                o_ref[:, off:off + width] = jnp.zeros((o_ref.shape[0], width), o_ref.dtype)
            else:
                o_ref[:, off:off + width] = w_ref[:, start:start + width].astype(o_ref.dtype)
            off += width


def _regroup_bf16(w, plans):
    k, n = w.shape
    rb = min(k, WEIGHT_ROW_BLOCK)
    widths = [sum(width for _, width in plan) for plan in plans]
    return pl.pallas_call(
        functools.partial(_regroup_kernel, plans=plans),
        grid=(k // rb,),
        in_specs=[pl.BlockSpec((rb, n), lambda i: (i, 0))],
        out_specs=[pl.BlockSpec((rb, wd), lambda i: (i, 0)) for wd in widths],
        out_shape=[jax.ShapeDtypeStruct((k, wd), BF16) for wd in widths],
        compiler_params=_params("parallel"),
        name="regroup",
    )(w)


def _prep_layer_weights(w_in, w_uq, w_ukv, w_ret_o, w_mla_o, w_out, w_gate_up, w_down):
    qk_w, v_w = RET_HEADS * RET_DK, RET_HEADS * RET_DV
    q_lora = w_uq.shape[0]
    kv_lora = w_ukv.shape[0]
    o_rg = 2 * qk_w + v_w
    o_cq = o_rg + v_w
    o_kpe = o_cq + q_lora + kv_lora
    o_ga = o_kpe + ROPE_DIM
    wa, wg = _regroup_bf16(w_in, [[(0, o_rg), (o_cq, o_ga - o_cq), (None, LANES - ROPE_DIM)],
                                  [(o_rg, v_w), (o_ga, w_in.shape[1] - o_ga)]])
    head_w = NOPE_DIM + ROPE_DIM
    wuq, = _regroup_bf16(w_uq, [[piece for h in range(MLA_HEADS)
                                 for piece in ((h * head_w, head_w), (None, QK_PAD - head_w))]])
    kv_w = NOPE_DIM + V_DIM
    wuk, wuv = _regroup_bf16(w_ukv, [[(h * kv_w, NOPE_DIM) for h in range(MLA_HEADS)],
                                     [(h * kv_w + NOPE_DIM, V_DIM) for h in range(MLA_HEADS)]])
    assert w_down.shape[0] % FF_CHUNK == 0
    return dict(wa=wa, wg=wg, wuq=wuq, wuk=wuk, wuv=wuv, wro=w_ret_o.astype(BF16), wmo=w_mla_o.astype(BF16),
                wo=w_out.astype(BF16), wgu=w_gate_up.astype(BF16), wd=w_down.astype(BF16))


def _layer(x, tabs, tab_tiles, lw, norms, state0, cache, fw, final_norm):
    b, s, d = x.shape
    n = b * s
    x2 = x.reshape(n, d)
    nmw, qnw, kvnw, gnw, nfw = norms
    wukv = (lw["wuk"], lw["wuv"])
    if cache is None:
        ret, s_fin, q, ckv, kpe, *kv = _inproj(x2, tabs, nmw, lw["wa"], qnw, kvnw, lw["wuq"],
                                               wukv + (state0, gnw), tab_tiles=tab_tiles)
    else:
        rq, rk, rv, q, ckv, kpe = _inproj(x2, tabs, nmw, lw["wa"], qnw, kvnw, lw["wuq"], None, tab_tiles=tab_tiles)
        ret, s_fin = _retention(rq.reshape(b, s, -1), rk.reshape(b, s, -1), rv.reshape(b, s, -1), state0, gnw, s)
    ckv3 = ckv.reshape(b, s, -1)
    kpe3 = kpe.reshape(b, s, ROPE_DIM)
    q3 = q.reshape(b, s, MLA_HEADS * QK_PAD)
    if cache is None:
        mla = _attn_prompt(q3, *(a.reshape(b, s, -1) for a in kv))
    else:
        mla = _attn_sample(q3, cache[0].astype(F32), cache[1].astype(F32), ckv3, kpe3, *wukv)
    h2 = _mix(x2, ret.reshape(n, -1), mla.reshape(n, -1), nmw, lw["wg"], lw["wro"], lw["wmo"], lw["wo"])
    y2 = _ffn(h2, nfw, lw["wgu"], lw["wd"], fw, final_norm)
    return y2.reshape(b, s, d), (ckv3, kpe3, s_fin)


def kernel(x_prompt, x_sample, cache_ckv, cache_kpe, state_ret, norm_mix_w, w_in, q_norm_w, w_uq, kv_norm_w, w_ukv, ret_gn_w, w_ret_o, w_mla_o, w_out, norm_ffn_w, w_gate_up, w_down, norm_final_w):
    depth = w_in.shape[0]
    bp, sp, _ = x_prompt.shape
    bs, ss, _ = x_sample.shape
    past = cache_ckv.shape[2]
    assert sp % TOKEN_TILE == 0 and sp % ATTN_Q_TILE == 0 and (bs * ss) % min(TOKEN_TILE, bs * ss) == 0
    assert ss == CHUNK, "the sample group is one streaming chunk"

    tabs_p = _rope_tables(0, sp, 1)
    tabs_s = _rope_tables(past, ss, bs)
    fw = norm_final_w.reshape(1, -1)
    state0_p = jnp.zeros((bp, RET_HEADS, RET_DK, RET_DV), F32)

    hp, hs = x_prompt, x_sample
    outs = [[] for _ in range(6)]
    for l in range(depth):
        lw = _prep_layer_weights(w_in[l], w_uq[l], w_ukv[l], w_ret_o[l], w_mla_o[l], w_out[l], w_gate_up[l],
                                 w_down[l])
        norms = (norm_mix_w[l].reshape(1, -1), q_norm_w[l].reshape(1, -1), kv_norm_w[l].reshape(1, -1),
                 ret_gn_w[l].reshape(1, -1), norm_ffn_w[l].reshape(1, -1))
        final = l == depth - 1
        hp, (a, b_, c) = _layer(hp, tabs_p, sp // TOKEN_TILE, lw, norms, state0_p, None, fw, final)
        hs, (d_, e, f) = _layer(hs, tabs_s, 1, lw, norms, state_ret[l].astype(F32),
                                (cache_ckv[l], cache_kpe[l]), fw, final)
        for lst, val in zip(outs, (a, b_, c, d_, e, f)):
            lst.append(val)
    return (hp, hs) + tuple(o[0][None] if depth == 1 else jnp.stack(o) for o in outs)
```

```python
import functools
import math

import jax
import jax.numpy as jnp
import numpy as np
from jax import lax
from jax.experimental import pallas as pl
from jax.experimental.pallas import tpu as pltpu

F32 = jnp.float32
BF16 = jnp.bfloat16

CHUNK = 64
RET_HEADS = 4
RET_DK = 128
RET_DV = 256
MLA_HEADS = 8
NOPE_DIM = 128
ROPE_DIM = 64
V_DIM = 128
ROPE_BASE = 10000.0
EPS = 1e-6

LANES = 128
QK_PAD = 2 * LANES
V7X_VMEM_BYTES = 64 * 2**20
VMEM_LIMIT = V7X_VMEM_BYTES * 7 // 8
TOKEN_TILE = 512
WIDE_TOKEN_TILE = 1024
WEIGHT_STREAM_BLOCK = 128
RET_BLOCK = 256
ATTN_Q_TILE = 2048
ATTN_KV_TILE = 512
FF_CHUNK = 256
NEG_BIG = -1e30
SOFTMAX_C = (NOPE_DIM + ROPE_DIM) ** -0.5 * math.log2(math.e)

_dot = functools.partial(jnp.dot, preferred_element_type=F32)


def _dot_nt(a, b):
    return lax.dot_general(a, b, (((1,), (1,)), ((), ())), preferred_element_type=F32)


def _dot_tn(a, b):
    return lax.dot_general(a, b, (((0,), (0,)), ((), ())), preferred_element_type=F32)


def _rms(x, g):
    ms = jnp.mean(x * x, axis=-1, keepdims=True)
    return x * lax.rsqrt(ms + EPS) * g


def _sigmoid(x):
    return 1.0 / (1.0 + jnp.exp(-x))


def _silu(x):
    return x * _sigmoid(x)


def _params(*sem):
    return pltpu.CompilerParams(dimension_semantics=sem, vmem_limit_bytes=VMEM_LIMIT)


def _const_spec(shape):
    return pl.BlockSpec(shape, lambda *_: (0,) * len(shape), pipeline_mode=pl.Buffered(1))


def _row_spec(tm, width):
    return pl.BlockSpec((tm, width), lambda i: (i, 0))


def _rope_full(x, cos2, sin2):
    return x * cos2 + pltpu.roll(x, LANES // 2, 1) * sin2


def _rope_half(x, c, sa, sb):
    q = ROPE_DIM // 2
    return x * c + pltpu.roll(x, LANES - q, 1) * sa + pltpu.roll(x, q, 1) * sb


def _ret_decay(blk, h):
    lg = math.log(1.0 - 2.0 ** (-5.0 - h))
    ri = lax.broadcasted_iota(jnp.int32, (blk, blk), 0)
    ci = lax.broadcasted_iota(jnp.int32, (blk, blk), 1)
    diff = (ri - ci).astype(F32)
    n = lax.broadcasted_iota(jnp.int32, (blk, RET_DK), 0).astype(F32)
    decay = jnp.where(diff >= 0, jnp.exp(lg * jnp.maximum(diff, 0.0)), 0.0)
    return decay, jnp.exp(lg * (n + 1.0)), jnp.exp(lg * (blk - 1.0 - n)), math.exp(lg * blk)


def _ret_block(q, k, v, state_ref, h, dec, gnw):
    decay, q_dec, k_dec, s_dec = dec
    v = v.astype(BF16)
    scores = _dot_nt(q.astype(BF16), k.astype(BF16)) * decay
    st = state_ref[h]
    y = _dot(scores.astype(BF16), v) + _dot((q * q_dec).astype(BF16), st.astype(BF16))
    state_ref[h] = s_dec * st + _dot_tn((k * k_dec).astype(BF16), v)
    yc = y - jnp.mean(y, axis=-1, keepdims=True)
    return yc * lax.rsqrt(jnp.mean(yc * yc, axis=-1, keepdims=True) + EPS) * gnw


def _inproj_kernel(x_ref, nw_ref, wa_ref, qnw_ref, kvnw_ref, wuq_ref, cr_ref, sr_ref, cm_ref, sm_ref, *rest,
                   prompt, seq_tiles):
    if prompt:
        (wuk_ref, wuv_ref, s0_ref, gnw_ref, ret_ref, sfin_ref, q_ref, ckv_ref, kpe_ref, kn_ref, kp_ref, v_ref,
         state_scr, dec_scr, qdec_scr, kdec_scr) = rest
    else:
        rq_ref, rk_ref, rv_ref, q_ref, ckv_ref, kpe_ref = rest
    qk_w = RET_HEADS * RET_DK
    v_w = RET_HEADS * RET_DV
    q_lora = wuq_ref.shape[0]
    kv_lora = ckv_ref.shape[1]
    o_rk, o_rv, o_cq = qk_w, 2 * qk_w, 2 * qk_w + v_w
    o_ckv = o_cq + q_lora
    o_kpe = o_ckv + kv_lora

    xn = _rms(x_ref[...], nw_ref[...]).astype(BF16)
    c, s = cr_ref[...], sr_ref[...]
    cr = jnp.concatenate([c, c], axis=1)
    sr = jnp.concatenate([-s, s], axis=1)
    c, s = cm_ref[...], sm_ref[...]
    z = jnp.zeros_like(s)
    cm = jnp.concatenate([c, c, c, c], axis=1)
    sa = jnp.concatenate([-s, z, -s, z], axis=1)
    sb = jnp.concatenate([z, s, z, s], axis=1)

    zq = _dot_nt(xn, wa_ref[0:qk_w, :])
    zk = _dot_nt(xn, wa_ref[o_rk:o_rk + qk_w, :])
    rv = _dot_nt(xn, wa_ref[o_rv:o_rv + v_w, :])
    k_scale = RET_DK ** -0.5
    rq = [_rope_full(zq[:, h * RET_DK:(h + 1) * RET_DK], cr, sr) for h in range(RET_HEADS)]
    rk = [_rope_full(zk[:, h * RET_DK:(h + 1) * RET_DK], cr, sr) * k_scale for h in range(RET_HEADS)]
    if prompt:
        i = pl.program_id(0)
        blk = dec_scr.shape[1]

        @pl.when(i == 0)
        def _():
            for h in range(RET_HEADS):
                dec_scr[h], qdec_scr[h], kdec_scr[h], _ = _ret_decay(blk, h)

        @pl.when(i % seq_tiles == 0)
        def _():
            state_scr[...] = s0_ref[...]

        for b in range(x_ref.shape[0] // blk):
            r = slice(b * blk, (b + 1) * blk)
            for h in range(RET_HEADS):
                sl = slice(h * RET_DV, (h + 1) * RET_DV)
                dec = (dec_scr[h], qdec_scr[h], kdec_scr[h], math.exp(math.log(1.0 - 2.0 ** (-5.0 - h)) * blk))
                ret_ref[r, sl] = _ret_block(rq[h][r], rk[h][r], rv[r, sl], state_scr, h, dec,
                                            gnw_ref[:, sl]).astype(BF16)

        @pl.when(i % seq_tiles == seq_tiles - 1)
        def _():
            sfin_ref[...] = state_scr[...]
    else:
        for h in range(RET_HEADS):
            sl = slice(h * RET_DK, (h + 1) * RET_DK)
            rq_ref[:, sl] = rq[h].astype(BF16)
            rk_ref[:, sl] = rk[h].astype(BF16)
        rv_ref[...] = rv.astype(BF16)

    cq = _dot_nt(xn, wa_ref[o_cq:o_cq + q_lora, :])
    cqn = _rms(cq, qnw_ref[...]).astype(BF16)
    for h in range(MLA_HEADS):
        qh = _dot(cqn, wuq_ref[:, h * QK_PAD:(h + 1) * QK_PAD]) * SOFTMAX_C
        q_ref[:, h * QK_PAD:h * QK_PAD + LANES] = qh[:, :LANES].astype(BF16)
        q_ref[:, h * QK_PAD + LANES:(h + 1) * QK_PAD] = _rope_half(qh[:, LANES:], cm, sa, sb).astype(BF16)

    ckv = _rms(_dot_nt(xn, wa_ref[o_ckv:o_ckv + kv_lora, :]), kvnw_ref[...])
    ckv_ref[...] = ckv
    kpe_pad = _rope_half(_dot_nt(xn, wa_ref[o_kpe:o_kpe + LANES, :]), cm, sa, sb)
    if prompt:
        kpe_ref[...] = kpe_pad.T[:ROPE_DIM, :]
    else:
        kpe_ref[...] = kpe_pad[:, :ROPE_DIM]
    if prompt:
        c16 = ckv.astype(BF16)
        kn_ref[...] = _dot(c16, wuk_ref[...]).astype(BF16)
        v_ref[...] = _dot(c16, wuv_ref[...]).astype(BF16)
        kp_ref[...] = kpe_pad.astype(BF16)


def _inproj(x2, tabs, nw, wa, qnw, kvnw, wuq, prompt_args, *, tab_tiles):
    n, d = x2.shape
    tm = min(TOKEN_TILE, n)
    qk_w, v_w = RET_HEADS * RET_DK, RET_HEADS * RET_DV
    kv_lora = kvnw.shape[1]
    prompt = prompt_args is not None
    tab_specs = [pl.BlockSpec((tm, t.shape[1]), lambda i: (i % tab_tiles, 0)) for t in tabs]
    in_specs = [_row_spec(tm, d), _const_spec(nw.shape), _const_spec(wa.shape), _const_spec(qnw.shape),
                _const_spec(kvnw.shape), _const_spec(wuq.shape)] + tab_specs
    mla_outs = [(MLA_HEADS * QK_PAD, BF16), (kv_lora, F32), (ROPE_DIM, F32)]
    scratch = []
    if prompt:
        wuk, wuv, state0, gnw = prompt_args
        assert tm % RET_BLOCK == 0
        st_spec = pl.BlockSpec((None,) + state0.shape[1:], lambda i: (i // tab_tiles, 0, 0, 0))
        in_specs += [_const_spec(wuk.shape), _const_spec(wuv.shape), st_spec, _const_spec(gnw.shape)]
        row_outs = [(v_w, BF16)] + mla_outs + [(MLA_HEADS * NOPE_DIM, BF16), (LANES, BF16), (MLA_HEADS * V_DIM, BF16)]
        out_specs = [_row_spec(tm, w) for w, _ in row_outs]
        out_shape = [jax.ShapeDtypeStruct((n, w), dt) for w, dt in row_outs]
        out_specs.insert(1, st_spec)
        out_shape.insert(1, jax.ShapeDtypeStruct(state0.shape, F32))
        seq = tab_tiles * tm
        out_specs[4] = pl.BlockSpec((None, ROPE_DIM, tm), lambda i: (i // tab_tiles, 0, i % tab_tiles))
        out_shape[4] = jax.ShapeDtypeStruct((n // seq, ROPE_DIM, seq), F32)
        scratch = [pltpu.VMEM(state0.shape[1:], F32), pltpu.VMEM((RET_HEADS, RET_BLOCK, RET_BLOCK), F32),
                   pltpu.VMEM((RET_HEADS, RET_BLOCK, RET_DK), F32), pltpu.VMEM((RET_HEADS, RET_BLOCK, RET_DK), F32)]
    else:
        prompt_args = ()
        row_outs = [(qk_w, BF16), (qk_w, BF16), (v_w, BF16)] + mla_outs
        out_specs = [_row_spec(tm, w) for w, _ in row_outs]
        out_shape = [jax.ShapeDtypeStruct((n, w), dt) for w, dt in row_outs]
    return pl.pallas_call(
        functools.partial(_inproj_kernel, prompt=prompt, seq_tiles=tab_tiles),
        grid=(n // tm,),
        in_specs=in_specs,
        out_specs=out_specs,
        out_shape=out_shape,
        scratch_shapes=scratch,
        compiler_params=_params("arbitrary"),
        name="inproj",
    )(x2, nw, wa, qnw, kvnw, wuq, *tabs, *prompt_args)


def _retention_kernel(q_ref, k_ref, v_ref, s0_ref, gnw_ref, y_ref, sfin_ref, state_ref, *, blk, nblk):
    c = pl.program_id(1)

    @pl.when(c == 0)
    def _():
        state_ref[...] = s0_ref[...]

    for h in range(RET_HEADS):
        sl = slice(h * RET_DV, (h + 1) * RET_DV)
        q = q_ref[:, h * RET_DK:(h + 1) * RET_DK].astype(F32)
        k = k_ref[:, h * RET_DK:(h + 1) * RET_DK].astype(F32)
        y_ref[:, sl] = _ret_block(q, k, v_ref[:, sl], state_ref, h, _ret_decay(blk, h), gnw_ref[:, sl]).astype(BF16)

    @pl.when(c == nblk - 1)
    def _():
        sfin_ref[...] = state_ref[...]


def _retention(rq, rk, rv, state0, gnw, blk):
    b, s, _ = rq.shape
    nblk = s // blk
    qk_w, v_w = RET_HEADS * RET_DK, RET_HEADS * RET_DV
    st_spec = pl.BlockSpec((None, RET_HEADS, RET_DK, RET_DV), lambda i, c: (i, 0, 0, 0))
    return pl.pallas_call(
        functools.partial(_retention_kernel, blk=blk, nblk=nblk),
        grid=(b, nblk),
        in_specs=[pl.BlockSpec((None, blk, qk_w), lambda i, c: (i, c, 0)),
                  pl.BlockSpec((None, blk, qk_w), lambda i, c: (i, c, 0)),
                  pl.BlockSpec((None, blk, v_w), lambda i, c: (i, c, 0)),
                  st_spec,
                  pl.BlockSpec((1, v_w), lambda i, c: (0, 0))],
        out_specs=[pl.BlockSpec((None, blk, v_w), lambda i, c: (i, c, 0)), st_spec],
        out_shape=[jax.ShapeDtypeStruct((b, s, v_w), BF16),
                   jax.ShapeDtypeStruct((b, RET_HEADS, RET_DK, RET_DV), F32)],
        scratch_shapes=[pltpu.VMEM((RET_HEADS, RET_DK, RET_DV), F32)],
        compiler_params=_params("parallel", "arbitrary"),
        name="retention",
    )(rq, rk, rv, state0, gnw)


def _attn_prompt_kernel(q_ref, kn_ref, kp_ref, v_ref, o_ref, s_scr, mx_scr, m_scr, acc_scr, *, tq, tk):
    nq = q_ref.shape[0] // tq
    r = tq // tk
    ones = jnp.ones((tk, V_DIM), BF16)

    def rows(ref, t, n):
        return ref[pl.ds(pl.multiple_of(t * n, n), n), :]

    def lane_max(s):
        blocks = [s[:, c * LANES:(c + 1) * LANES] for c in range(tk // LANES)]
        while len(blocks) > 1:
            blocks = [jnp.maximum(a, b) for a, b in zip(blocks[::2], blocks[1::2])]
        return blocks[0]

    def scores(qi, t, row0=0):
        k = jnp.concatenate([rows(kn_ref, t, tk), rows(kp_ref, t, tk)], axis=1)
        q = q_ref[pl.ds(pl.multiple_of(qi * tq + row0, tk), tq - row0), :]
        s = _dot_nt(q, k)
        return s, lane_max(s)

    def park(sm, row0=0):
        s_scr[row0:, :], mx_scr[row0:, :] = sm

    def parked(row0=0):
        return s_scr[row0:, :], mx_scr[row0:, :]

    def consume(sm, t, row0=0, masked=False):
        s, mx = sm
        if masked:
            row = lax.broadcasted_iota(jnp.int32, s.shape, 0) // CHUNK
            col = lax.broadcasted_iota(jnp.int32, s.shape, 1) // CHUNK
            s = jnp.where(col <= row, s, NEG_BIG)
            mx = lane_max(s)
        m_old = m_scr[row0:, :]
        m_new = jnp.max(jnp.concatenate([mx, m_old], axis=1), axis=-1, keepdims=True)
        alpha = jnp.exp2(m_old - m_new)
        p = jnp.exp2(s - m_new).astype(BF16)
        v_ext = jnp.concatenate([rows(v_ref, t, tk), ones], axis=1)
        m_scr[row0:, :] = jnp.broadcast_to(m_new, m_old.shape)
        acc_scr[row0:, :] = jnp.concatenate([alpha, alpha], axis=1) * acc_scr[row0:, :] + _dot(p, v_ext)

    def restart():
        m_scr[...] = jnp.full_like(m_scr, NEG_BIG)
        acc_scr[...] = jnp.zeros_like(acc_scr)

    restart()
    park(scores(0, 0))

    def q_tile(qi, _):
        def trip(j, _):
            for u in range(0, r, 2):
                t = r * j + u
                odd = scores(qi, t + 1)
                consume(parked(), t)
                park(scores(qi, t + 2))
                consume(odd, t + 1)
            return 0

        lax.fori_loop(0, qi, trip, 0)
        for d in range(0, r, 2):
            t = r * qi + d
            odd = scores(qi, t + 1, (d + 1) * tk)
            consume(parked(d * tk), t, d * tk, masked=True)
            if d + 2 < r:
                park(scores(qi, t + 2, (d + 2) * tk), (d + 2) * tk)
            else:
                park(scores(jnp.minimum(qi + 1, nq - 1), 0))
            consume(odd, t + 1, (d + 1) * tk, masked=True)
        acc = acc_scr[...]
        o_ref[pl.ds(pl.multiple_of(qi * tq, tq), tq), :] = (acc[:, :V_DIM] / acc[:, V_DIM:]).astype(BF16)
        restart()
        return 0

    lax.fori_loop(0, nq, q_tile, 0)


def _attn_prompt(q, kn, kp, v):
    b, s, _ = q.shape
    tq, tk = ATTN_Q_TILE, ATTN_KV_TILE
    assert tq % (2 * tk) == 0 and s % tq == 0
    return pl.pallas_call(
        functools.partial(_attn_prompt_kernel, tq=tq, tk=tk),
        grid=(b, MLA_HEADS),
        in_specs=[pl.BlockSpec((None, s, QK_PAD), lambda i, h: (i, 0, h)),
                  pl.BlockSpec((None, s, NOPE_DIM), lambda i, h: (i, 0, h)),
                  pl.BlockSpec((None, s, LANES), lambda i, h: (i, 0, 0)),
                  pl.BlockSpec((None, s, V_DIM), lambda i, h: (i, 0, h))],
        out_specs=pl.BlockSpec((None, s, V_DIM), lambda i, h: (i, 0, h)),
        out_shape=jax.ShapeDtypeStruct((b, s, MLA_HEADS * V_DIM), BF16),
        scratch_shapes=[pltpu.VMEM((tq, tk), F32), pltpu.VMEM((tq, LANES), F32), pltpu.VMEM((tq, LANES), F32),
                        pltpu.VMEM((tq, 2 * V_DIM), F32)],
        compiler_params=_params("parallel", "parallel"),
        name="attn_prompt",
    )(q, kn, kp, v)


def _attn_sample_kernel(q_ref, cckv_ref, ckpe_t_ref, nckv_ref, nkpe_ref, wuk_ref, wuv_ref, o_ref):
    n = q_ref.shape[0]
    q_lat, q_rope = [], []
    for h in range(MLA_HEADS):
        q_nope = q_ref[:, h * QK_PAD:h * QK_PAD + NOPE_DIM]
        q_lat.append(_dot_nt(q_nope, wuk_ref[:, h * NOPE_DIM:(h + 1) * NOPE_DIM]).astype(BF16))
        q_rope.append(q_ref[:, h * QK_PAD + NOPE_DIM:h * QK_PAD + NOPE_DIM + ROPE_DIM])
    q_lat = jnp.concatenate(q_lat, axis=0)
    q_rope = jnp.concatenate(q_rope, axis=0)
    vc = cckv_ref[...].astype(BF16)
    vn = nckv_ref[...].astype(BF16)
    sc = _dot_nt(q_lat, vc) + _dot(q_rope, ckpe_t_ref[...].astype(BF16))
    sn = _dot_nt(q_lat, vn) + _dot_nt(q_rope, nkpe_ref[...].astype(BF16))
    m = jnp.maximum(jnp.max(sc, axis=-1, keepdims=True), jnp.max(sn, axis=-1, keepdims=True))
    pc = jnp.exp2(sc - m)
    pn = jnp.exp2(sn - m)
    l = jnp.sum(pc, axis=-1, keepdims=True) + jnp.sum(pn, axis=-1, keepdims=True)
    o_lat = ((_dot(pc.astype(BF16), vc) + _dot(pn.astype(BF16), vn)) / l).astype(BF16)
    for h in range(MLA_HEADS):
        sl = slice(h * V_DIM, (h + 1) * V_DIM)
        o_ref[:, sl] = _dot(o_lat[h * n:(h + 1) * n], wuv_ref[:, sl]).astype(BF16)


def _attn_sample(q, cache_ckv, cache_kpe_t, new_ckv, new_kpe, wuk, wuv):
    b, s, _ = q.shape

    def per_stream(a):
        return pl.BlockSpec((None,) + a.shape[1:], lambda i: (i, 0, 0))

    return pl.pallas_call(
        _attn_sample_kernel,
        grid=(b,),
        in_specs=[per_stream(q), per_stream(cache_ckv), per_stream(cache_kpe_t), per_stream(new_ckv),
                  per_stream(new_kpe), _const_spec(wuk.shape), _const_spec(wuv.shape)],
        out_specs=pl.BlockSpec((None, s, MLA_HEADS * V_DIM), lambda i: (i, 0, 0)),
        out_shape=jax.ShapeDtypeStruct((b, s, MLA_HEADS * V_DIM), BF16),
        compiler_params=_params("parallel"),
        name="attn_sample",
    )(q, cache_ckv, cache_kpe_t, new_ckv, new_kpe, wuk, wuv)


def _mix_kernel(x_ref, ret_ref, mla_ref, nw_ref, wg_ref, wro_ref, wmo_ref, wo_ref, h_ref):
    d = x_ref.shape[1]
    v_w = ret_ref.shape[1]
    x = x_ref[...]
    xn = _rms(x, nw_ref[...]).astype(BF16)
    rg = _dot_nt(xn, wg_ref[0:v_w, :])
    ret_b = _dot((ret_ref[...].astype(F32) * _silu(rg)).astype(BF16), wro_ref[...])
    ga = _dot_nt(xn, wg_ref[v_w:v_w + d, :])
    merged = _sigmoid(ga) * ret_b
    mla_b = _dot(mla_ref[...], wmo_ref[...])
    gb = _dot_nt(xn, wg_ref[v_w + d:v_w + 2 * d, :])
    merged = merged + _sigmoid(gb) * mla_b
    h_ref[...] = x + _dot(merged.astype(BF16), wo_ref[...])


def _mix(x2, ret2, mla2, nw, wg, wro, wmo, wo):
    n, d = x2.shape
    tm = min(WIDE_TOKEN_TILE, n)
    return pl.pallas_call(
        _mix_kernel,
        grid=(n // tm,),
        in_specs=[_row_spec(tm, d), _row_spec(tm, ret2.shape[1]), _row_spec(tm, mla2.shape[1]),
                  _const_spec(nw.shape), _const_spec(wg.shape), _const_spec(wro.shape),
                  _const_spec(wmo.shape), _const_spec(wo.shape)],
        out_specs=_row_spec(tm, d),
        out_shape=jax.ShapeDtypeStruct((n, d), F32),
        compiler_params=_params("parallel"),
        name="mix",
    )(x2, ret2, mla2, nw, wg, wro, wmo, wo)


def _ffn_kernel(h_ref, nw_ref, wgu_ref, wd_ref, fw_ref, y_ref, *, final_norm):
    h = h_ref[...]
    hn = _rms(h, nw_ref[...]).astype(BF16)
    acc = h
    d_ff = wd_ref.shape[0]
    for c in range(d_ff // FF_CHUNK):
        sl = slice(c * FF_CHUNK, (c + 1) * FF_CHUNK)
        g = _dot(hn, wgu_ref[:, sl])
        u = _dot(hn, wgu_ref[:, d_ff + c * FF_CHUNK:d_ff + (c + 1) * FF_CHUNK])
        acc = acc + _dot((_silu(g) * u).astype(BF16), wd_ref[sl, :])
    y_ref[...] = _rms(acc, fw_ref[...]) if final_norm else acc


def _ffn(h2, nw, wgu, wd, fw, final_norm):
    n, d = h2.shape
    tm = min(WIDE_TOKEN_TILE, n)
    return pl.pallas_call(
        functools.partial(_ffn_kernel, final_norm=final_norm),
        grid=(n // tm,),
        in_specs=[_row_spec(tm, d), _const_spec(nw.shape), _const_spec(wgu.shape), _const_spec(wd.shape),
                  _const_spec(fw.shape)],
        out_specs=_row_spec(tm, d),
        out_shape=jax.ShapeDtypeStruct((n, d), F32),
        compiler_params=_params("parallel"),
        name="ffn",
    )(h2, nw, wgu, wd, fw)


def _rope_tables(start, n, reps):
    pos = np.arange(start, start + n, dtype=np.float64)[:, None]

    def cs(d):
        inv = ROPE_BASE ** (-np.arange(0, d, 2, dtype=np.float64) / d)
        ang = pos * inv[None, :]
        return [np.tile(f(ang).astype(np.float32), (reps, 1)) for f in (np.cos, np.sin)]

    return tuple(jnp.asarray(t) for t in cs(RET_DK) + cs(ROPE_DIM))


def _regroup_kernel(w_ref, *out_refs, plans, axis):
    for o_ref, plan in zip(out_refs, plans):
        off = 0
        for start, width in plan:
            dst = (slice(off, off + width), slice(None))[::1 if axis == 0 else -1]
            if start is None:
                o_ref[dst] = jnp.zeros(o_ref[dst].shape, o_ref.dtype)
            else:
                src = (slice(start, start + width), slice(None))[::1 if axis == 0 else -1]
                o_ref[dst] = w_ref[src].astype(o_ref.dtype)
            off += width


def _regroup_bf16(w, plans, axis):
    other = w.shape[1 - axis]
    blk = min(other, WEIGHT_STREAM_BLOCK)
    widths = [sum(width for _, width in plan) for plan in plans]

    def spec(width):
        return (pl.BlockSpec((width, blk), lambda i: (0, i)) if axis == 0 else
                pl.BlockSpec((blk, width), lambda i: (i, 0)))

    def shape(width):
        return (width, other) if axis == 0 else (other, width)

    return pl.pallas_call(
        functools.partial(_regroup_kernel, plans=plans, axis=axis),
        grid=(other // blk,),
        in_specs=[spec(w.shape[axis])],
        out_specs=[spec(wd) for wd in widths],
        out_shape=[jax.ShapeDtypeStruct(shape(wd), BF16) for wd in widths],
        compiler_params=_params("parallel"),
        name="regroup",
    )(w)


def _prep_layer_weights(w_in, w_uq, w_ukv, w_ret_o, w_mla_o, w_out, w_gate_up, w_down):
    qk_w, v_w = RET_HEADS * RET_DK, RET_HEADS * RET_DV
    q_lora = w_uq.shape[0]
    kv_lora = w_ukv.shape[0]
    o_rg = 2 * qk_w + v_w
    o_cq = o_rg + v_w
    o_kpe = o_cq + q_lora + kv_lora
    o_ga = o_kpe + ROPE_DIM
    wa, wg = _regroup_bf16(jnp.swapaxes(w_in, 0, 1),
                           [[(0, o_rg), (o_cq, o_ga - o_cq), (None, LANES - ROPE_DIM)],
                            [(o_rg, v_w), (o_ga, w_in.shape[1] - o_ga)]], axis=0)
    head_w = NOPE_DIM + ROPE_DIM
    wuq, = _regroup_bf16(w_uq, [[piece for h in range(MLA_HEADS)
                                 for piece in ((h * head_w, head_w), (None, QK_PAD - head_w))]], axis=1)
    kv_w = NOPE_DIM + V_DIM
    wuk, wuv = _regroup_bf16(w_ukv, [[(h * kv_w, NOPE_DIM) for h in range(MLA_HEADS)],
                                     [(h * kv_w + NOPE_DIM, V_DIM) for h in range(MLA_HEADS)]], axis=1)
    assert w_down.shape[0] % FF_CHUNK == 0
    return dict(wa=wa, wg=wg, wuq=wuq, wuk=wuk, wuv=wuv, wro=w_ret_o.astype(BF16), wmo=w_mla_o.astype(BF16),
                wo=w_out.astype(BF16), wgu=w_gate_up.astype(BF16), wd=w_down.astype(BF16))


def _layer(x, tabs, tab_tiles, lw, norms, state0, cache, fw, final_norm):
    b, s, d = x.shape
    n = b * s
    x2 = x.reshape(n, d)
    nmw, qnw, kvnw, gnw, nfw = norms
    wukv = (lw["wuk"], lw["wuv"])
    if cache is None:
        ret, s_fin, q, ckv, kpe, *kv = _inproj(x2, tabs, nmw, lw["wa"], qnw, kvnw, lw["wuq"],
                                               wukv + (state0, gnw), tab_tiles=tab_tiles)
    else:
        rq, rk, rv, q, ckv, kpe = _inproj(x2, tabs, nmw, lw["wa"], qnw, kvnw, lw["wuq"], None, tab_tiles=tab_tiles)
        ret, s_fin = _retention(rq.reshape(b, s, -1), rk.reshape(b, s, -1), rv.reshape(b, s, -1), state0, gnw, s)
    ckv3 = ckv.reshape(b, s, -1)
    kpe3 = jnp.swapaxes(kpe, 1, 2) if cache is None else kpe.reshape(b, s, ROPE_DIM)
    q3 = q.reshape(b, s, MLA_HEADS * QK_PAD)
    if cache is None:
        mla = _attn_prompt(q3, *(a.reshape(b, s, -1) for a in kv))
    else:
        mla = _attn_sample(q3, cache[0].astype(F32), jnp.swapaxes(cache[1].astype(F32), 1, 2), ckv3, kpe3, *wukv)
    h2 = _mix(x2, ret.reshape(n, -1), mla.reshape(n, -1), nmw, lw["wg"], lw["wro"], lw["wmo"], lw["wo"])
    y2 = _ffn(h2, nfw, lw["wgu"], lw["wd"], fw, final_norm)
    return y2.reshape(b, s, d), (ckv3, kpe3, s_fin)


def kernel(x_prompt, x_sample, cache_ckv, cache_kpe, state_ret, norm_mix_w, w_in, q_norm_w, w_uq, kv_norm_w, w_ukv, ret_gn_w, w_ret_o, w_mla_o, w_out, norm_ffn_w, w_gate_up, w_down, norm_final_w):
    depth = w_in.shape[0]
    bp, sp, _ = x_prompt.shape
    bs, ss, _ = x_sample.shape
    past = cache_ckv.shape[2]
    assert sp % TOKEN_TILE == 0 and sp % ATTN_Q_TILE == 0 and (bs * ss) % min(TOKEN_TILE, bs * ss) == 0
    assert ss == CHUNK, "the sample group is one streaming chunk"

    tabs_p = _rope_tables(0, sp, 1)
    tabs_s = _rope_tables(past, ss, bs)
    fw = norm_final_w.reshape(1, -1)
    state0_p = jnp.zeros((bp, RET_HEADS, RET_DK, RET_DV), F32)

    hp, hs = x_prompt, x_sample
    outs = [[] for _ in range(6)]
    for l in range(depth):
        lw = _prep_layer_weights(w_in[l], w_uq[l], w_ukv[l], w_ret_o[l], w_mla_o[l], w_out[l], w_gate_up[l],
                                 w_down[l])
        norms = (norm_mix_w[l].reshape(1, -1), q_norm_w[l].reshape(1, -1), kv_norm_w[l].reshape(1, -1),
                 ret_gn_w[l].reshape(1, -1), norm_ffn_w[l].reshape(1, -1))
        final = l == depth - 1
        hp, (a, b_, c) = _layer(hp, tabs_p, sp // TOKEN_TILE, lw, norms, state0_p, None, fw, final)
        hs, (d_, e, f) = _layer(hs, tabs_s, 1, lw, norms, state_ret[l].astype(F32),
                                (cache_ckv[l], cache_kpe[l]), fw, final)
        for lst, val in zip(outs, (a, b_, c, d_, e, f)):
            lst.append(val)
    return (hp, hs) + tuple(o[0][None] if depth == 1 else jnp.stack(o) for o in outs)
```

```python
import functools
import math

import jax
import jax.numpy as jnp
import numpy as np
from jax import lax
from jax.experimental import pallas as pl
from jax.experimental.pallas import tpu as pltpu

F32 = jnp.float32
BF16 = jnp.bfloat16

CHUNK = 64
RET_HEADS = 4
RET_DK = 128
RET_DV = 256
MLA_HEADS = 8
NOPE_DIM = 128
ROPE_DIM = 64
V_DIM = 128
ROPE_BASE = 10000.0
EPS = 1e-6

LANES = 128
QK_PAD = 2 * LANES
V7X_VMEM_BYTES = 64 * 2**20
VMEM_LIMIT = V7X_VMEM_BYTES * 7 // 8
TOKEN_TILE = 512
WIDE_TOKEN_TILE = 1024
WEIGHT_STREAM_BLOCK = 128
RET_BLOCK = 256
ATTN_Q_TILE = 2048
ATTN_KV_TILE = 512
ATTN_COL_BLOCK = 256
FF_CHUNK = 256
NEG_BIG = -1e30
SOFTMAX_C = (NOPE_DIM + ROPE_DIM) ** -0.5 * math.log2(math.e)

_dot = functools.partial(jnp.dot, preferred_element_type=F32)


def _dot_nt(a, b):
    return lax.dot_general(a, b, (((1,), (1,)), ((), ())), preferred_element_type=F32)


def _dot_tn(a, b):
    return lax.dot_general(a, b, (((0,), (0,)), ((), ())), preferred_element_type=F32)


def _rms(x, g):
    ms = jnp.mean(x * x, axis=-1, keepdims=True)
    return x * lax.rsqrt(ms + EPS) * g


def _sigmoid(x):
    return 1.0 / (1.0 + jnp.exp(-x))


def _silu(x):
    return x * _sigmoid(x)


def _params(*sem):
    return pltpu.CompilerParams(dimension_semantics=sem, vmem_limit_bytes=VMEM_LIMIT)


def _const_spec(shape):
    return pl.BlockSpec(shape, lambda *_: (0,) * len(shape), pipeline_mode=pl.Buffered(1))


def _row_spec(tm, width):
    return pl.BlockSpec((tm, width), lambda i: (i, 0))


def _rope_full(x, cos2, sin2):
    return x * cos2 + pltpu.roll(x, LANES // 2, 1) * sin2


def _rope_half(x, c, sa, sb):
    q = ROPE_DIM // 2
    return x * c + pltpu.roll(x, LANES - q, 1) * sa + pltpu.roll(x, q, 1) * sb


def _ret_decay(blk, h):
    lg = math.log(1.0 - 2.0 ** (-5.0 - h))
    ri = lax.broadcasted_iota(jnp.int32, (blk, blk), 0)
    ci = lax.broadcasted_iota(jnp.int32, (blk, blk), 1)
    diff = (ri - ci).astype(F32)
    n = lax.broadcasted_iota(jnp.int32, (blk, RET_DK), 0).astype(F32)
    decay = jnp.where(diff >= 0, jnp.exp(lg * jnp.maximum(diff, 0.0)), 0.0)
    return decay, jnp.exp(lg * (n + 1.0)), jnp.exp(lg * (blk - 1.0 - n)), math.exp(lg * blk)


def _ret_block(q, k, v, state_ref, h, dec, gnw):
    decay, q_dec, k_dec, s_dec = dec
    v = v.astype(BF16)
    scores = _dot_nt(q.astype(BF16), k.astype(BF16)) * decay
    st = state_ref[h]
    y = _dot(scores.astype(BF16), v) + _dot((q * q_dec).astype(BF16), st.astype(BF16))
    state_ref[h] = s_dec * st + _dot_tn((k * k_dec).astype(BF16), v)
    yc = y - jnp.mean(y, axis=-1, keepdims=True)
    return yc * lax.rsqrt(jnp.mean(yc * yc, axis=-1, keepdims=True) + EPS) * gnw


def _inproj_kernel(x_ref, nw_ref, wa_ref, qnw_ref, kvnw_ref, wuq_ref, cr_ref, sr_ref, cm_ref, sm_ref, *rest,
                   prompt, seq_tiles):
    if prompt:
        (wuk_ref, wuvt_ref, s0_ref, gnw_ref, ret_ref, sfin_ref, q_ref, ckv_ref, kpe_ref, kn_ref, kp_ref, vt_ref,
         state_scr, dec_scr, qdec_scr, kdec_scr) = rest
    else:
        rq_ref, rk_ref, rv_ref, q_ref, ckv_ref, kpe_ref = rest
    qk_w = RET_HEADS * RET_DK
    v_w = RET_HEADS * RET_DV
    q_lora = wuq_ref.shape[0]
    kv_lora = ckv_ref.shape[1]
    o_rk, o_rv, o_cq = qk_w, 2 * qk_w, 2 * qk_w + v_w
    o_ckv = o_cq + q_lora
    o_kpe = o_ckv + kv_lora

    xn = _rms(x_ref[...], nw_ref[...]).astype(BF16)
    c, s = cr_ref[...], sr_ref[...]
    cr = jnp.concatenate([c, c], axis=1)
    sr = jnp.concatenate([-s, s], axis=1)
    c, s = cm_ref[...], sm_ref[...]
    z = jnp.zeros_like(s)
    cm = jnp.concatenate([c, c, c, c], axis=1)
    sa = jnp.concatenate([-s, z, -s, z], axis=1)
    sb = jnp.concatenate([z, s, z, s], axis=1)

    zq = _dot_nt(xn, wa_ref[0:qk_w, :])
    zk = _dot_nt(xn, wa_ref[o_rk:o_rk + qk_w, :])
    rv = _dot_nt(xn, wa_ref[o_rv:o_rv + v_w, :])
    k_scale = RET_DK ** -0.5
    rq = [_rope_full(zq[:, h * RET_DK:(h + 1) * RET_DK], cr, sr) for h in range(RET_HEADS)]
    rk = [_rope_full(zk[:, h * RET_DK:(h + 1) * RET_DK], cr, sr) * k_scale for h in range(RET_HEADS)]
    if prompt:
        i = pl.program_id(0)
        blk = dec_scr.shape[1]

        @pl.when(i == 0)
        def _():
            for h in range(RET_HEADS):
                dec_scr[h], qdec_scr[h], kdec_scr[h], _ = _ret_decay(blk, h)

        @pl.when(i % seq_tiles == 0)
        def _():
            state_scr[...] = s0_ref[...]

        for b in range(x_ref.shape[0] // blk):
            r = slice(b * blk, (b + 1) * blk)
            for h in range(RET_HEADS):
                sl = slice(h * RET_DV, (h + 1) * RET_DV)
                dec = (dec_scr[h], qdec_scr[h], kdec_scr[h], math.exp(math.log(1.0 - 2.0 ** (-5.0 - h)) * blk))
                ret_ref[r, sl] = _ret_block(rq[h][r], rk[h][r], rv[r, sl], state_scr, h, dec,
                                            gnw_ref[:, sl]).astype(BF16)

        @pl.when(i % seq_tiles == seq_tiles - 1)
        def _():
            sfin_ref[...] = state_scr[...]
    else:
        for h in range(RET_HEADS):
            sl = slice(h * RET_DK, (h + 1) * RET_DK)
            rq_ref[:, sl] = rq[h].astype(BF16)
            rk_ref[:, sl] = rk[h].astype(BF16)
        rv_ref[...] = rv.astype(BF16)

    cq = _dot_nt(xn, wa_ref[o_cq:o_cq + q_lora, :])
    cqn = _rms(cq, qnw_ref[...]).astype(BF16)
    for h in range(MLA_HEADS):
        qh = _dot(cqn, wuq_ref[:, h * QK_PAD:(h + 1) * QK_PAD]) * SOFTMAX_C
        q_ref[:, h * QK_PAD:h * QK_PAD + LANES] = qh[:, :LANES].astype(BF16)
        q_ref[:, h * QK_PAD + LANES:(h + 1) * QK_PAD] = _rope_half(qh[:, LANES:], cm, sa, sb).astype(BF16)

    ckv = _rms(_dot_nt(xn, wa_ref[o_ckv:o_ckv + kv_lora, :]), kvnw_ref[...])
    ckv_ref[...] = ckv
    kpe_pad = _rope_half(_dot_nt(xn, wa_ref[o_kpe:o_kpe + LANES, :]), cm, sa, sb)
    if prompt:
        kpe_ref[...] = kpe_pad.T[:ROPE_DIM, :]
    else:
        kpe_ref[...] = kpe_pad[:, :ROPE_DIM]
    if prompt:
        c16 = ckv.astype(BF16)
        kn_ref[...] = _dot(c16, wuk_ref[...]).astype(BF16)
        kp_ref[...] = kpe_pad.astype(BF16)
        v_t = _dot_nt(wuvt_ref[...], c16)
        for h in range(MLA_HEADS):
            vt_ref[h] = v_t[h * V_DIM:(h + 1) * V_DIM, :].astype(BF16)


def _inproj(x2, tabs, nw, wa, qnw, kvnw, wuq, prompt_args, *, tab_tiles):
    n, d = x2.shape
    tm = min(TOKEN_TILE, n)
    qk_w, v_w = RET_HEADS * RET_DK, RET_HEADS * RET_DV
    kv_lora = kvnw.shape[1]
    prompt = prompt_args is not None
    tab_specs = [pl.BlockSpec((tm, t.shape[1]), lambda i: (i % tab_tiles, 0)) for t in tabs]
    in_specs = [_row_spec(tm, d), _const_spec(nw.shape), _const_spec(wa.shape), _const_spec(qnw.shape),
                _const_spec(kvnw.shape), _const_spec(wuq.shape)] + tab_specs
    mla_outs = [(MLA_HEADS * QK_PAD, BF16), (kv_lora, F32), (ROPE_DIM, F32)]
    scratch = []
    if prompt:
        wuk, wuvt, state0, gnw = prompt_args
        assert tm % RET_BLOCK == 0 and tm == ATTN_KV_TILE
        st_spec = pl.BlockSpec((None,) + state0.shape[1:], lambda i: (i // tab_tiles, 0, 0, 0))
        in_specs += [_const_spec(wuk.shape), _const_spec(wuvt.shape), st_spec, _const_spec(gnw.shape)]
        row_outs = [(v_w, BF16)] + mla_outs + [(MLA_HEADS * NOPE_DIM, BF16), (LANES, BF16)]
        out_specs = [_row_spec(tm, w) for w, _ in row_outs]
        out_shape = [jax.ShapeDtypeStruct((n, w), dt) for w, dt in row_outs]
        out_specs.insert(1, st_spec)
        out_shape.insert(1, jax.ShapeDtypeStruct(state0.shape, F32))
        seq = tab_tiles * tm
        out_specs[4] = pl.BlockSpec((None, ROPE_DIM, tm), lambda i: (i // tab_tiles, 0, i % tab_tiles))
        out_shape[4] = jax.ShapeDtypeStruct((n // seq, ROPE_DIM, seq), F32)
        out_specs.append(pl.BlockSpec((None, MLA_HEADS, None, V_DIM, tm),
                                      lambda i: (i // tab_tiles, 0, i % tab_tiles, 0, 0)))
        out_shape.append(jax.ShapeDtypeStruct((n // seq, MLA_HEADS, tab_tiles, V_DIM, tm), BF16))
        scratch = [pltpu.VMEM(state0.shape[1:], F32), pltpu.VMEM((RET_HEADS, RET_BLOCK, RET_BLOCK), F32),
                   pltpu.VMEM((RET_HEADS, RET_BLOCK, RET_DK), F32), pltpu.VMEM((RET_HEADS, RET_BLOCK, RET_DK), F32)]
    else:
        prompt_args = ()
        row_outs = [(qk_w, BF16), (qk_w, BF16), (v_w, BF16)] + mla_outs
        out_specs = [_row_spec(tm, w) for w, _ in row_outs]
        out_shape = [jax.ShapeDtypeStruct((n, w), dt) for w, dt in row_outs]
    return pl.pallas_call(
        functools.partial(_inproj_kernel, prompt=prompt, seq_tiles=tab_tiles),
        grid=(n // tm,),
        in_specs=in_specs,
        out_specs=out_specs,
        out_shape=out_shape,
        scratch_shapes=scratch,
        compiler_params=_params("arbitrary"),
        name="inproj",
    )(x2, nw, wa, qnw, kvnw, wuq, *tabs, *prompt_args)


def _retention_kernel(q_ref, k_ref, v_ref, s0_ref, gnw_ref, y_ref, sfin_ref, state_ref, *, blk, nblk):
    c = pl.program_id(1)

    @pl.when(c == 0)
    def _():
        state_ref[...] = s0_ref[...]

    for h in range(RET_HEADS):
        sl = slice(h * RET_DV, (h + 1) * RET_DV)
        q = q_ref[:, h * RET_DK:(h + 1) * RET_DK].astype(F32)
        k = k_ref[:, h * RET_DK:(h + 1) * RET_DK].astype(F32)
        y_ref[:, sl] = _ret_block(q, k, v_ref[:, sl], state_ref, h, _ret_decay(blk, h), gnw_ref[:, sl]).astype(BF16)

    @pl.when(c == nblk - 1)
    def _():
        sfin_ref[...] = state_ref[...]


def _retention(rq, rk, rv, state0, gnw, blk):
    b, s, _ = rq.shape
    nblk = s // blk
    qk_w, v_w = RET_HEADS * RET_DK, RET_HEADS * RET_DV
    st_spec = pl.BlockSpec((None, RET_HEADS, RET_DK, RET_DV), lambda i, c: (i, 0, 0, 0))
    return pl.pallas_call(
        functools.partial(_retention_kernel, blk=blk, nblk=nblk),
        grid=(b, nblk),
        in_specs=[pl.BlockSpec((None, blk, qk_w), lambda i, c: (i, c, 0)),
                  pl.BlockSpec((None, blk, qk_w), lambda i, c: (i, c, 0)),
                  pl.BlockSpec((None, blk, v_w), lambda i, c: (i, c, 0)),
                  st_spec,
                  pl.BlockSpec((1, v_w), lambda i, c: (0, 0))],
        out_specs=[pl.BlockSpec((None, blk, v_w), lambda i, c: (i, c, 0)), st_spec],
        out_shape=[jax.ShapeDtypeStruct((b, s, v_w), BF16),
                   jax.ShapeDtypeStruct((b, RET_HEADS, RET_DK, RET_DV), F32)],
        scratch_shapes=[pltpu.VMEM((RET_HEADS, RET_DK, RET_DV), F32)],
        compiler_params=_params("parallel", "arbitrary"),
        name="retention",
    )(rq, rk, rv, state0, gnw)


def _attn_prompt_kernel(q_ref, kn_ref, kp_ref, vt_ref, o_ref, s_scr, mx_scr, m_scr, acc_scr, *, tq, tk):
    nq = q_ref.shape[0] // tq
    assert tq == 4 * tk
    cb = ATTN_COL_BLOCK
    ones = jnp.ones((acc_scr.shape[0] - V_DIM, tk), BF16)

    def rows(ref, t, n):
        return ref[pl.ds(pl.multiple_of(t * n, n), n), :]

    def col_max(s):
        return jnp.broadcast_to(jnp.max(s, axis=0, keepdims=True), (mx_scr.shape[1], s.shape[1]))

    def park_block(b, qi, t, c):
        cols = slice(c * cb, (c + 1) * cb)
        k = jnp.concatenate([rows(kn_ref, t, tk), rows(kp_ref, t, tk)], axis=1)
        s = _dot_nt(k, q_ref[pl.ds(pl.multiple_of(qi * tq + c * cb, cb), cb), :])
        s_scr[b, :, cols] = s
        mx_scr[b, :, cols] = col_max(s)

    def consume_block(b, t, c, col0, masked):
        cols = slice(c * cb, (c + 1) * cb)
        s = s_scr[b, :, cols]
        mx = mx_scr[b, :, cols]
        if masked and c * cb - col0 < tk:
            key = lax.broadcasted_iota(jnp.int32, s.shape, 0) // CHUNK
            qry = (lax.broadcasted_iota(jnp.int32, s.shape, 1) + (c * cb - col0)) // CHUNK
            s = jnp.where(key <= qry, s, NEG_BIG)
            mx = col_max(s)
        m_old = m_scr[:, cols]
        m_new = jnp.maximum(m_old, mx)
        alpha = jnp.exp2(m_old - m_new)
        p = jnp.exp2(s - m_new[:1, :]).astype(BF16)
        m_scr[:, cols] = m_new
        v_ext = jnp.concatenate([vt_ref[t], ones], axis=0)
        acc_scr[:, cols] = alpha[:1, :] * acc_scr[:, cols] + _dot(v_ext, p)

    def step(consume_args, park_args):
        for c in range(tq // cb):
            if park_args is not None and c * cb >= park_args[3]:
                park_block(park_args[0], park_args[1], park_args[2], c)
            if consume_args is not None and c * cb >= consume_args[2]:
                consume_block(consume_args[0], consume_args[1], c, consume_args[2], consume_args[3])

    def restart():
        m_scr[...] = jnp.full_like(m_scr, NEG_BIG)
        acc_scr[...] = jnp.zeros_like(acc_scr)

    restart()
    step(None, (0, 0, 0, 0))
    step(None, (1, 0, 1, 0))

    def q_tile(qi, _):
        def trip(j, _):
            t = 4 * j
            step((0, t, 0, False), (2, qi, t + 2, 0))
            step((1, t + 1, 0, False), (3, qi, t + 3, 0))
            step((2, t + 2, 0, False), (0, qi, t + 4, 0))
            step((3, t + 3, 0, False), (1, qi, t + 5, 0))
            return 0

        lax.fori_loop(0, qi, trip, 0)
        t = 4 * qi
        nxt = jnp.minimum(qi + 1, nq - 1)
        step((0, t, 0, True), (2, qi, t + 2, 2 * tk))
        step((1, t + 1, tk, True), (3, qi, t + 3, 3 * tk))
        step((2, t + 2, 2 * tk, True), (0, nxt, 0, 0))
        step((3, t + 3, 3 * tk, True), (1, nxt, 1, 0))
        acc = acc_scr[...]
        out_t = acc[:V_DIM, :] / acc[V_DIM:V_DIM + 1, :]
        o_ref[pl.ds(pl.multiple_of(qi * tq, tq), tq), :] = out_t.T.astype(BF16)
        restart()
        return 0

    lax.fori_loop(0, nq, q_tile, 0)


def _attn_prompt(q, kn, kp, vt):
    b, s, _ = q.shape
    tq, tk = ATTN_Q_TILE, ATTN_KV_TILE
    assert tq % (2 * tk) == 0 and s % tq == 0 and vt.shape[2:] == (s // tk, V_DIM, tk)
    stat_rows = 8
    acc_rows = V_DIM + 16
    return pl.pallas_call(
        functools.partial(_attn_prompt_kernel, tq=tq, tk=tk),
        grid=(b, MLA_HEADS),
        in_specs=[pl.BlockSpec((None, s, QK_PAD), lambda i, h: (i, 0, h)),
                  pl.BlockSpec((None, s, NOPE_DIM), lambda i, h: (i, 0, h)),
                  pl.BlockSpec((None, s, LANES), lambda i, h: (i, 0, 0)),
                  pl.BlockSpec((None, None) + vt.shape[2:], lambda i, h: (i, h, 0, 0, 0))],
        out_specs=pl.BlockSpec((None, s, V_DIM), lambda i, h: (i, 0, h)),
        out_shape=jax.ShapeDtypeStruct((b, s, MLA_HEADS * V_DIM), BF16),
        scratch_shapes=[pltpu.VMEM((4, tk, tq), F32), pltpu.VMEM((4, stat_rows, tq), F32),
                        pltpu.VMEM((stat_rows, tq), F32), pltpu.VMEM((acc_rows, tq), F32)],
        compiler_params=_params("parallel", "parallel"),
        name="attn_prompt",
    )(q, kn, kp, vt)


def _attn_sample_kernel(q_ref, cckv_ref, ckpe_t_ref, nckv_ref, nkpe_ref, wuk_ref, wuv_ref, o_ref):
    n = q_ref.shape[0]
    q_lat, q_rope = [], []
    for h in range(MLA_HEADS):
        q_nope = q_ref[:, h * QK_PAD:h * QK_PAD + NOPE_DIM]
        q_lat.append(_dot_nt(q_nope, wuk_ref[:, h * NOPE_DIM:(h + 1) * NOPE_DIM]).astype(BF16))
        q_rope.append(q_ref[:, h * QK_PAD + NOPE_DIM:h * QK_PAD + NOPE_DIM + ROPE_DIM])
    q_lat = jnp.concatenate(q_lat, axis=0)
    q_rope = jnp.concatenate(q_rope, axis=0)
    vc = cckv_ref[...].astype(BF16)
    vn = nckv_ref[...].astype(BF16)
    sc = _dot_nt(q_lat, vc) + _dot(q_rope, ckpe_t_ref[...].astype(BF16))
    sn = _dot_nt(q_lat, vn) + _dot_nt(q_rope, nkpe_ref[...].astype(BF16))
    m = jnp.maximum(jnp.max(sc, axis=-1, keepdims=True), jnp.max(sn, axis=-1, keepdims=True))
    pc = jnp.exp2(sc - m)
    pn = jnp.exp2(sn - m)
    l = jnp.sum(pc, axis=-1, keepdims=True) + jnp.sum(pn, axis=-1, keepdims=True)
    o_lat = ((_dot(pc.astype(BF16), vc) + _dot(pn.astype(BF16), vn)) / l).astype(BF16)
    for h in range(MLA_HEADS):
        sl = slice(h * V_DIM, (h + 1) * V_DIM)
        o_ref[:, sl] = _dot(o_lat[h * n:(h + 1) * n], wuv_ref[:, sl]).astype(BF16)


def _attn_sample(q, cache_ckv, cache_kpe_t, new_ckv, new_kpe, wuk, wuv):
    b, s, _ = q.shape

    def per_stream(a):
        return pl.BlockSpec((None,) + a.shape[1:], lambda i: (i, 0, 0))

    return pl.pallas_call(
        _attn_sample_kernel,
        grid=(b,),
        in_specs=[per_stream(q), per_stream(cache_ckv), per_stream(cache_kpe_t), per_stream(new_ckv),
                  per_stream(new_kpe), _const_spec(wuk.shape), _const_spec(wuv.shape)],
        out_specs=pl.BlockSpec((None, s, MLA_HEADS * V_DIM), lambda i: (i, 0, 0)),
        out_shape=jax.ShapeDtypeStruct((b, s, MLA_HEADS * V_DIM), BF16),
        compiler_params=_params("parallel"),
        name="attn_sample",
    )(q, cache_ckv, cache_kpe_t, new_ckv, new_kpe, wuk, wuv)


def _mix_kernel(x_ref, ret_ref, mla_ref, nw_ref, wg_ref, wro_ref, wmo_ref, wo_ref, h_ref):
    d = x_ref.shape[1]
    v_w = ret_ref.shape[1]
    x = x_ref[...]
    xn = _rms(x, nw_ref[...]).astype(BF16)
    rg = _dot_nt(xn, wg_ref[0:v_w, :])
    ret_b = _dot((ret_ref[...].astype(F32) * _silu(rg)).astype(BF16), wro_ref[...])
    ga = _dot_nt(xn, wg_ref[v_w:v_w + d, :])
    merged = _sigmoid(ga) * ret_b
    mla_b = _dot(mla_ref[...], wmo_ref[...])
    gb = _dot_nt(xn, wg_ref[v_w + d:v_w + 2 * d, :])
    merged = merged + _sigmoid(gb) * mla_b
    h_ref[...] = x + _dot(merged.astype(BF16), wo_ref[...])


def _mix(x2, ret2, mla2, nw, wg, wro, wmo, wo):
    n, d = x2.shape
    tm = min(WIDE_TOKEN_TILE, n)
    return pl.pallas_call(
        _mix_kernel,
        grid=(n // tm,),
        in_specs=[_row_spec(tm, d), _row_spec(tm, ret2.shape[1]), _row_spec(tm, mla2.shape[1]),
                  _const_spec(nw.shape), _const_spec(wg.shape), _const_spec(wro.shape),
                  _const_spec(wmo.shape), _const_spec(wo.shape)],
        out_specs=_row_spec(tm, d),
        out_shape=jax.ShapeDtypeStruct((n, d), F32),
        compiler_params=_params("parallel"),
        name="mix",
    )(x2, ret2, mla2, nw, wg, wro, wmo, wo)


def _ffn_kernel(h_ref, nw_ref, wgu_ref, wd_ref, fw_ref, y_ref, *, final_norm):
    h = h_ref[...]
    hn = _rms(h, nw_ref[...]).astype(BF16)
    acc = h
    d_ff = wd_ref.shape[0]
    for c in range(d_ff // FF_CHUNK):
        sl = slice(c * FF_CHUNK, (c + 1) * FF_CHUNK)
        g = _dot(hn, wgu_ref[:, sl])
        u = _dot(hn, wgu_ref[:, d_ff + c * FF_CHUNK:d_ff + (c + 1) * FF_CHUNK])
        acc = acc + _dot((_silu(g) * u).astype(BF16), wd_ref[sl, :])
    y_ref[...] = _rms(acc, fw_ref[...]) if final_norm else acc


def _ffn(h2, nw, wgu, wd, fw, final_norm):
    n, d = h2.shape
    tm = min(WIDE_TOKEN_TILE, n)
    return pl.pallas_call(
        functools.partial(_ffn_kernel, final_norm=final_norm),
        grid=(n // tm,),
        in_specs=[_row_spec(tm, d), _const_spec(nw.shape), _const_spec(wgu.shape), _const_spec(wd.shape),
                  _const_spec(fw.shape)],
        out_specs=_row_spec(tm, d),
        out_shape=jax.ShapeDtypeStruct((n, d), F32),
        compiler_params=_params("parallel"),
        name="ffn",
    )(h2, nw, wgu, wd, fw)


def _rope_tables(start, n, reps):
    pos = np.arange(start, start + n, dtype=np.float64)[:, None]

    def cs(d):
        inv = ROPE_BASE ** (-np.arange(0, d, 2, dtype=np.float64) / d)
        ang = pos * inv[None, :]
        return [np.tile(f(ang).astype(np.float32), (reps, 1)) for f in (np.cos, np.sin)]

    return tuple(jnp.asarray(t) for t in cs(RET_DK) + cs(ROPE_DIM))


def _regroup_kernel(w_ref, *out_refs, plans, axis):
    for o_ref, plan in zip(out_refs, plans):
        off = 0
        for start, width in plan:
            dst = (slice(off, off + width), slice(None))[::1 if axis == 0 else -1]
            if start is None:
                o_ref[dst] = jnp.zeros(o_ref[dst].shape, o_ref.dtype)
            else:
                src = (slice(start, start + width), slice(None))[::1 if axis == 0 else -1]
                o_ref[dst] = w_ref[src].astype(o_ref.dtype)
            off += width


def _regroup_bf16(w, plans, axis):
    other = w.shape[1 - axis]
    blk = min(other, WEIGHT_STREAM_BLOCK)
    widths = [sum(width for _, width in plan) for plan in plans]

    def spec(width):
        return (pl.BlockSpec((width, blk), lambda i: (0, i)) if axis == 0 else
                pl.BlockSpec((blk, width), lambda i: (i, 0)))

    def shape(width):
        return (width, other) if axis == 0 else (other, width)

    return pl.pallas_call(
        functools.partial(_regroup_kernel, plans=plans, axis=axis),
        grid=(other // blk,),
        in_specs=[spec(w.shape[axis])],
        out_specs=[spec(wd) for wd in widths],
        out_shape=[jax.ShapeDtypeStruct(shape(wd), BF16) for wd in widths],
        compiler_params=_params("parallel"),
        name="regroup",
    )(w)


def _prep_layer_weights(w_in, w_uq, w_ukv, w_ret_o, w_mla_o, w_out, w_gate_up, w_down):
    qk_w, v_w = RET_HEADS * RET_DK, RET_HEADS * RET_DV
    q_lora = w_uq.shape[0]
    kv_lora = w_ukv.shape[0]
    o_rg = 2 * qk_w + v_w
    o_cq = o_rg + v_w
    o_kpe = o_cq + q_lora + kv_lora
    o_ga = o_kpe + ROPE_DIM
    wa, wg = _regroup_bf16(jnp.swapaxes(w_in, 0, 1),
                           [[(0, o_rg), (o_cq, o_ga - o_cq), (None, LANES - ROPE_DIM)],
                            [(o_rg, v_w), (o_ga, w_in.shape[1] - o_ga)]], axis=0)
    head_w = NOPE_DIM + ROPE_DIM
    wuq, = _regroup_bf16(w_uq, [[piece for h in range(MLA_HEADS)
                                 for piece in ((h * head_w, head_w), (None, QK_PAD - head_w))]], axis=1)
    kv_w = NOPE_DIM + V_DIM
    wuk, wuv = _regroup_bf16(w_ukv, [[(h * kv_w, NOPE_DIM) for h in range(MLA_HEADS)],
                                     [(h * kv_w + NOPE_DIM, V_DIM) for h in range(MLA_HEADS)]], axis=1)
    assert w_down.shape[0] % FF_CHUNK == 0
    return dict(wa=wa, wg=wg, wuq=wuq, wuk=wuk, wuv=wuv, wro=w_ret_o.astype(BF16), wmo=w_mla_o.astype(BF16),
                wo=w_out.astype(BF16), wgu=w_gate_up.astype(BF16), wd=w_down.astype(BF16))


def _layer(x, tabs, tab_tiles, lw, norms, state0, cache, fw, final_norm):
    b, s, d = x.shape
    n = b * s
    x2 = x.reshape(n, d)
    nmw, qnw, kvnw, gnw, nfw = norms
    wukv = (lw["wuk"], lw["wuv"])
    if cache is None:
        ret, s_fin, q, ckv, kpe, kn, kp, vt = _inproj(x2, tabs, nmw, lw["wa"], qnw, kvnw, lw["wuq"],
                                                      (lw["wuk"], lw["wuv"].T, state0, gnw), tab_tiles=tab_tiles)
    else:
        rq, rk, rv, q, ckv, kpe = _inproj(x2, tabs, nmw, lw["wa"], qnw, kvnw, lw["wuq"], None, tab_tiles=tab_tiles)
        ret, s_fin = _retention(rq.reshape(b, s, -1), rk.reshape(b, s, -1), rv.reshape(b, s, -1), state0, gnw, s)
    ckv3 = ckv.reshape(b, s, -1)
    kpe3 = jnp.swapaxes(kpe, 1, 2) if cache is None else kpe.reshape(b, s, ROPE_DIM)
    q3 = q.reshape(b, s, MLA_HEADS * QK_PAD)
    if cache is None:
        mla = _attn_prompt(q3, kn.reshape(b, s, -1), kp.reshape(b, s, -1), vt)
    else:
        mla = _attn_sample(q3, cache[0].astype(F32), jnp.swapaxes(cache[1].astype(F32), 1, 2), ckv3, kpe3, *wukv)
    h2 = _mix(x2, ret.reshape(n, -1), mla.reshape(n, -1), nmw, lw["wg"], lw["wro"], lw["wmo"], lw["wo"])
    y2 = _ffn(h2, nfw, lw["wgu"], lw["wd"], fw, final_norm)
    return y2.reshape(b, s, d), (ckv3, kpe3, s_fin)


def kernel(x_prompt, x_sample, cache_ckv, cache_kpe, state_ret, norm_mix_w, w_in, q_norm_w, w_uq, kv_norm_w, w_ukv, ret_gn_w, w_ret_o, w_mla_o, w_out, norm_ffn_w, w_gate_up, w_down, norm_final_w):
    depth = w_in.shape[0]
    bp, sp, _ = x_prompt.shape
    bs, ss, _ = x_sample.shape
    past = cache_ckv.shape[2]
    assert sp % TOKEN_TILE == 0 and sp % ATTN_Q_TILE == 0 and (bs * ss) % min(TOKEN_TILE, bs * ss) == 0
    assert ss == CHUNK, "the sample group is one streaming chunk"

    tabs_p = _rope_tables(0, sp, 1)
    tabs_s = _rope_tables(past, ss, bs)
    fw = norm_final_w.reshape(1, -1)
    state0_p = jnp.zeros((bp, RET_HEADS, RET_DK, RET_DV), F32)

    hp, hs = x_prompt, x_sample
    outs = [[] for _ in range(6)]
    for l in range(depth):
        lw = _prep_layer_weights(w_in[l], w_uq[l], w_ukv[l], w_ret_o[l], w_mla_o[l], w_out[l], w_gate_up[l],
                                 w_down[l])
        norms = (norm_mix_w[l].reshape(1, -1), q_norm_w[l].reshape(1, -1), kv_norm_w[l].reshape(1, -1),
                 ret_gn_w[l].reshape(1, -1), norm_ffn_w[l].reshape(1, -1))
        final = l == depth - 1
        hp, (a, b_, c) = _layer(hp, tabs_p, sp // TOKEN_TILE, lw, norms, state0_p, None, fw, final)
        hs, (d_, e, f) = _layer(hs, tabs_s, 1, lw, norms, state_ret[l].astype(F32),
                                (cache_ckv[l], cache_kpe[l]), fw, final)
        for lst, val in zip(outs, (a, b_, c, d_, e, f)):
            lst.append(val)
    return (hp, hs) + tuple(o[0][None] if depth == 1 else jnp.stack(o) for o in outs)
```

```python
import functools
import math

import jax
import jax.numpy as jnp
import numpy as np
from jax import lax
from jax.experimental import pallas as pl
from jax.experimental.pallas import tpu as pltpu

F32 = jnp.float32
BF16 = jnp.bfloat16

CHUNK = 64
RET_HEADS = 4
RET_DK = 128
RET_DV = 256
MLA_HEADS = 8
NOPE_DIM = 128
ROPE_DIM = 64
V_DIM = 128
ROPE_BASE = 10000.0
EPS = 1e-6

LANES = 128
QK_PAD = 2 * LANES
V7X_VMEM_BYTES = 64 * 2**20
VMEM_LIMIT = V7X_VMEM_BYTES * 7 // 8
TOKEN_TILE = 512
WIDE_TOKEN_TILE = 1024
WEIGHT_STREAM_BLOCK = 128
RET_BLOCK = 256
ATTN_Q_TILE = 2048
ATTN_KV_TILE = 512
ATTN_COL_BLOCK = 256
FF_CHUNK = 256
NEG_BIG = -1e30
SOFTMAX_C = (NOPE_DIM + ROPE_DIM) ** -0.5 * math.log2(math.e)

_dot = functools.partial(jnp.dot, preferred_element_type=F32)


def _dot_nt(a, b):
    return lax.dot_general(a, b, (((1,), (1,)), ((), ())), preferred_element_type=F32)


def _dot_tn(a, b):
    return lax.dot_general(a, b, (((0,), (0,)), ((), ())), preferred_element_type=F32)


def _rms(x, g):
    ms = jnp.mean(x * x, axis=-1, keepdims=True)
    return x * lax.rsqrt(ms + EPS) * g


def _sigmoid(x):
    return 1.0 / (1.0 + jnp.exp(-x))


def _silu(x):
    return x * _sigmoid(x)


def _params(*sem):
    return pltpu.CompilerParams(dimension_semantics=sem, vmem_limit_bytes=VMEM_LIMIT)


def _const_spec(shape):
    return pl.BlockSpec(shape, lambda *_: (0,) * len(shape), pipeline_mode=pl.Buffered(1))


def _row_spec(tm, width):
    return pl.BlockSpec((tm, width), lambda i: (i, 0))


def _rope_full(x, cos2, sin2):
    return x * cos2 + pltpu.roll(x, LANES // 2, 1) * sin2


def _rope_half(x, c, sa, sb):
    q = ROPE_DIM // 2
    return x * c + pltpu.roll(x, LANES - q, 1) * sa + pltpu.roll(x, q, 1) * sb


def _ret_decay(blk, h):
    lg = math.log(1.0 - 2.0 ** (-5.0 - h))
    ri = lax.broadcasted_iota(jnp.int32, (blk, blk), 0)
    ci = lax.broadcasted_iota(jnp.int32, (blk, blk), 1)
    diff = (ri - ci).astype(F32)
    n = lax.broadcasted_iota(jnp.int32, (blk, RET_DK), 0).astype(F32)
    decay = jnp.where(diff >= 0, jnp.exp(lg * jnp.maximum(diff, 0.0)), 0.0)
    return decay, jnp.exp(lg * (n + 1.0)), jnp.exp(lg * (blk - 1.0 - n)), math.exp(lg * blk)


def _ret_block(q, k, v, state_ref, h, dec, gnw):
    decay, q_dec, k_dec, s_dec = dec
    v = v.astype(BF16)
    scores = _dot_nt(q.astype(BF16), k.astype(BF16)) * decay
    st = state_ref[h]
    y = _dot(scores.astype(BF16), v) + _dot((q * q_dec).astype(BF16), st.astype(BF16))
    state_ref[h] = s_dec * st + _dot_tn((k * k_dec).astype(BF16), v)
    yc = y - jnp.mean(y, axis=-1, keepdims=True)
    return yc * lax.rsqrt(jnp.mean(yc * yc, axis=-1, keepdims=True) + EPS) * gnw


def _inproj_kernel(x_ref, nw_ref, wa_ref, qnw_ref, kvnw_ref, wuq_ref, cr_ref, sr_ref, cm_ref, sm_ref, *rest,
                   prompt, seq_tiles):
    if prompt:
        (wuk_ref, wuvt_ref, s0_ref, gnw_ref, ret_ref, sfin_ref, q_ref, ckv_ref, kpe_ref, kn_ref, kp_ref, vt_ref,
         state_scr, dec_scr, qdec_scr, kdec_scr) = rest
    else:
        rq_ref, rk_ref, rv_ref, q_ref, ckv_ref, kpe_ref = rest
    qk_w = RET_HEADS * RET_DK
    v_w = RET_HEADS * RET_DV
    q_lora = wuq_ref.shape[0]
    kv_lora = ckv_ref.shape[1]
    o_rk, o_rv, o_cq = qk_w, 2 * qk_w, 2 * qk_w + v_w
    o_ckv = o_cq + q_lora
    o_kpe = o_ckv + kv_lora

    xn = _rms(x_ref[...], nw_ref[...]).astype(BF16)
    c, s = cr_ref[...], sr_ref[...]
    cr = jnp.concatenate([c, c], axis=1)
    sr = jnp.concatenate([-s, s], axis=1)
    c, s = cm_ref[...], sm_ref[...]
    z = jnp.zeros_like(s)
    cm = jnp.concatenate([c, c, c, c], axis=1)
    sa = jnp.concatenate([-s, z, -s, z], axis=1)
    sb = jnp.concatenate([z, s, z, s], axis=1)

    zq = _dot_nt(xn, wa_ref[0:qk_w, :])
    zk = _dot_nt(xn, wa_ref[o_rk:o_rk + qk_w, :])
    rv = _dot_nt(xn, wa_ref[o_rv:o_rv + v_w, :])
    k_scale = RET_DK ** -0.5
    rq = [_rope_full(zq[:, h * RET_DK:(h + 1) * RET_DK], cr, sr) for h in range(RET_HEADS)]
    rk = [_rope_full(zk[:, h * RET_DK:(h + 1) * RET_DK], cr, sr) * k_scale for h in range(RET_HEADS)]
    if prompt:
        i = pl.program_id(0)
        blk = dec_scr.shape[1]

        @pl.when(i == 0)
        def _():
            for h in range(RET_HEADS):
                dec_scr[h], qdec_scr[h], kdec_scr[h], _ = _ret_decay(blk, h)

        @pl.when(i % seq_tiles == 0)
        def _():
            state_scr[...] = s0_ref[...]

        for b in range(x_ref.shape[0] // blk):
            r = slice(b * blk, (b + 1) * blk)
            for h in range(RET_HEADS):
                sl = slice(h * RET_DV, (h + 1) * RET_DV)
                dec = (dec_scr[h], qdec_scr[h], kdec_scr[h], math.exp(math.log(1.0 - 2.0 ** (-5.0 - h)) * blk))
                ret_ref[r, sl] = _ret_block(rq[h][r], rk[h][r], rv[r, sl], state_scr, h, dec,
                                            gnw_ref[:, sl]).astype(BF16)

        @pl.when(i % seq_tiles == seq_tiles - 1)
        def _():
            sfin_ref[...] = state_scr[...]
    else:
        for h in range(RET_HEADS):
            sl = slice(h * RET_DK, (h + 1) * RET_DK)
            rq_ref[:, sl] = rq[h].astype(BF16)
            rk_ref[:, sl] = rk[h].astype(BF16)
        rv_ref[...] = rv.astype(BF16)

    cq = _dot_nt(xn, wa_ref[o_cq:o_cq + q_lora, :])
    cqn = _rms(cq, qnw_ref[...]).astype(BF16)
    for h in range(MLA_HEADS):
        qh = _dot(cqn, wuq_ref[:, h * QK_PAD:(h + 1) * QK_PAD]) * SOFTMAX_C
        q_ref[:, h * QK_PAD:h * QK_PAD + LANES] = qh[:, :LANES].astype(BF16)
        q_ref[:, h * QK_PAD + LANES:(h + 1) * QK_PAD] = _rope_half(qh[:, LANES:], cm, sa, sb).astype(BF16)

    ckv = _rms(_dot_nt(xn, wa_ref[o_ckv:o_ckv + kv_lora, :]), kvnw_ref[...])
    ckv_ref[...] = ckv
    kpe_pad = _rope_half(_dot_nt(xn, wa_ref[o_kpe:o_kpe + LANES, :]), cm, sa, sb)
    if prompt:
        kpe_ref[...] = kpe_pad.T[:ROPE_DIM, :]
    else:
        kpe_ref[...] = kpe_pad[:, :ROPE_DIM]
    if prompt:
        c16 = ckv.astype(BF16)
        kn_ref[...] = _dot(c16, wuk_ref[...]).astype(BF16)
        kp_ref[...] = kpe_pad.astype(BF16)
        v_t = _dot_nt(wuvt_ref[...], c16)
        for h in range(MLA_HEADS):
            vt_ref[h] = v_t[h * V_DIM:(h + 1) * V_DIM, :].astype(BF16)


def _inproj(x2, tabs, nw, wa, qnw, kvnw, wuq, prompt_args, *, tab_tiles):
    n, d = x2.shape
    tm = min(TOKEN_TILE, n)
    qk_w, v_w = RET_HEADS * RET_DK, RET_HEADS * RET_DV
    kv_lora = kvnw.shape[1]
    prompt = prompt_args is not None
    tab_specs = [pl.BlockSpec((tm, t.shape[1]), lambda i: (i % tab_tiles, 0)) for t in tabs]
    in_specs = [_row_spec(tm, d), _const_spec(nw.shape), _const_spec(wa.shape), _const_spec(qnw.shape),
                _const_spec(kvnw.shape), _const_spec(wuq.shape)] + tab_specs
    mla_outs = [(MLA_HEADS * QK_PAD, BF16), (kv_lora, F32), (ROPE_DIM, F32)]
    scratch = []
    if prompt:
        wuk, wuvt, state0, gnw = prompt_args
        assert tm % RET_BLOCK == 0 and tm == ATTN_KV_TILE
        st_spec = pl.BlockSpec((None,) + state0.shape[1:], lambda i: (i // tab_tiles, 0, 0, 0))
        in_specs += [_const_spec(wuk.shape), _const_spec(wuvt.shape), st_spec, _const_spec(gnw.shape)]
        row_outs = [(v_w, BF16)] + mla_outs + [(MLA_HEADS * NOPE_DIM, BF16), (LANES, BF16)]
        out_specs = [_row_spec(tm, w) for w, _ in row_outs]
        out_shape = [jax.ShapeDtypeStruct((n, w), dt) for w, dt in row_outs]
        out_specs.insert(1, st_spec)
        out_shape.insert(1, jax.ShapeDtypeStruct(state0.shape, F32))
        seq = tab_tiles * tm
        out_specs[4] = pl.BlockSpec((None, ROPE_DIM, tm), lambda i: (i // tab_tiles, 0, i % tab_tiles))
        out_shape[4] = jax.ShapeDtypeStruct((n // seq, ROPE_DIM, seq), F32)
        out_specs.append(pl.BlockSpec((None, MLA_HEADS, None, V_DIM, tm),
                                      lambda i: (i // tab_tiles, 0, i % tab_tiles, 0, 0)))
        out_shape.append(jax.ShapeDtypeStruct((n // seq, MLA_HEADS, tab_tiles, V_DIM, tm), BF16))
        scratch = [pltpu.VMEM(state0.shape[1:], F32), pltpu.VMEM((RET_HEADS, RET_BLOCK, RET_BLOCK), F32),
                   pltpu.VMEM((RET_HEADS, RET_BLOCK, RET_DK), F32), pltpu.VMEM((RET_HEADS, RET_BLOCK, RET_DK), F32)]
    else:
        prompt_args = ()
        row_outs = [(qk_w, BF16), (qk_w, BF16), (v_w, BF16)] + mla_outs
        out_specs = [_row_spec(tm, w) for w, _ in row_outs]
        out_shape = [jax.ShapeDtypeStruct((n, w), dt) for w, dt in row_outs]
    return pl.pallas_call(
        functools.partial(_inproj_kernel, prompt=prompt, seq_tiles=tab_tiles),
        grid=(n // tm,),
        in_specs=in_specs,
        out_specs=out_specs,
        out_shape=out_shape,
        scratch_shapes=scratch,
        compiler_params=_params("arbitrary"),
        name="inproj",
    )(x2, nw, wa, qnw, kvnw, wuq, *tabs, *prompt_args)


def _retention_kernel(q_ref, k_ref, v_ref, s0_ref, gnw_ref, y_ref, sfin_ref, state_ref, *, blk, nblk):
    c = pl.program_id(1)

    @pl.when(c == 0)
    def _():
        state_ref[...] = s0_ref[...]

    for h in range(RET_HEADS):
        sl = slice(h * RET_DV, (h + 1) * RET_DV)
        q = q_ref[:, h * RET_DK:(h + 1) * RET_DK].astype(F32)
        k = k_ref[:, h * RET_DK:(h + 1) * RET_DK].astype(F32)
        y_ref[:, sl] = _ret_block(q, k, v_ref[:, sl], state_ref, h, _ret_decay(blk, h), gnw_ref[:, sl]).astype(BF16)

    @pl.when(c == nblk - 1)
    def _():
        sfin_ref[...] = state_ref[...]


def _retention(rq, rk, rv, state0, gnw, blk):
    b, s, _ = rq.shape
    nblk = s // blk
    qk_w, v_w = RET_HEADS * RET_DK, RET_HEADS * RET_DV
    st_spec = pl.BlockSpec((None, RET_HEADS, RET_DK, RET_DV), lambda i, c: (i, 0, 0, 0))
    return pl.pallas_call(
        functools.partial(_retention_kernel, blk=blk, nblk=nblk),
        grid=(b, nblk),
        in_specs=[pl.BlockSpec((None, blk, qk_w), lambda i, c: (i, c, 0)),
                  pl.BlockSpec((None, blk, qk_w), lambda i, c: (i, c, 0)),
                  pl.BlockSpec((None, blk, v_w), lambda i, c: (i, c, 0)),
                  st_spec,
                  pl.BlockSpec((1, v_w), lambda i, c: (0, 0))],
        out_specs=[pl.BlockSpec((None, blk, v_w), lambda i, c: (i, c, 0)), st_spec],
        out_shape=[jax.ShapeDtypeStruct((b, s, v_w), BF16),
                   jax.ShapeDtypeStruct((b, RET_HEADS, RET_DK, RET_DV), F32)],
        scratch_shapes=[pltpu.VMEM((RET_HEADS, RET_DK, RET_DV), F32)],
        compiler_params=_params("parallel", "arbitrary"),
        name="retention",
    )(rq, rk, rv, state0, gnw)


def _attn_prompt_kernel(q_ref, kn_ref, kp_ref, vt_ref, o_ref, s_scr, mx_scr, m_scr, acc_scr, *, tq, tk):
    nq = q_ref.shape[0] // tq
    assert tq == 4 * tk
    cb = ATTN_COL_BLOCK
    ones = jnp.ones((acc_scr.shape[0] - V_DIM, tk), BF16)

    def rows(ref, t, n):
        return ref[pl.ds(pl.multiple_of(t * n, n), n), :]

    def col_max(s):
        return jnp.broadcast_to(jnp.max(s, axis=0, keepdims=True), (mx_scr.shape[1], s.shape[1]))

    def park_block(b, qi, t, c):
        cols = slice(c * cb, (c + 1) * cb)
        k = jnp.concatenate([rows(kn_ref, t, tk), rows(kp_ref, t, tk)], axis=1)
        s = _dot_nt(k, q_ref[pl.ds(pl.multiple_of(qi * tq + c * cb, cb), cb), :])
        s_scr[b, :, cols] = s
        mx_scr[b, :, cols] = col_max(s)

    def consume_block(b, t, c, col0, masked):
        cols = slice(c * cb, (c + 1) * cb)
        s = s_scr[b, :, cols]
        mx = mx_scr[b, :, cols]
        if masked and c * cb - col0 < tk:
            key = lax.broadcasted_iota(jnp.int32, s.shape, 0) // CHUNK
            qry = (lax.broadcasted_iota(jnp.int32, s.shape, 1) + (c * cb - col0)) // CHUNK
            s = jnp.where(key <= qry, s, NEG_BIG)
            mx = col_max(s)
        m_old = m_scr[:, cols]
        m_new = jnp.maximum(m_old, mx)
        alpha = jnp.exp2(m_old - m_new)
        p = jnp.exp2(s - m_new[:1, :]).astype(BF16)
        m_scr[:, cols] = m_new
        v_ext = jnp.concatenate([vt_ref[t], ones], axis=0)
        acc_scr[:, cols] = alpha[:1, :] * acc_scr[:, cols] + _dot(v_ext, p)

    def step(consumes, parks):
        cons = [functools.partial(consume_block, b, t, c, col0, masked)
                for b, t, col0, masked in consumes for c in range(col0 // cb, tq // cb)]
        prks = [functools.partial(park_block, b, qi, t, c)
                for b, qi, t, col0 in parks for c in range(col0 // cb, tq // cb)]
        i = j = 0
        while i < len(cons) or j < len(prks):
            if j < len(prks) and (i == len(cons) or j * len(cons) <= i * len(prks)):
                prks[j]()
                j += 1
            else:
                cons[i]()
                i += 1

    def restart():
        m_scr[...] = jnp.full_like(m_scr, NEG_BIG)
        acc_scr[...] = jnp.zeros_like(acc_scr)

    restart()
    step([], [(0, 0, 0, 0), (1, 0, 1, 0)])

    def q_tile(qi, _):
        def trip(j, _):
            t = 4 * j
            step([(0, t, 0, False)], [(2, qi, t + 2, 0)])
            step([(1, t + 1, 0, False)], [(3, qi, t + 3, 0)])
            step([(2, t + 2, 0, False)], [(0, qi, t + 4, 0)])
            step([(3, t + 3, 0, False)], [(1, qi, t + 5, 0)])
            return 0

        lax.fori_loop(0, qi, trip, 0)
        t = 4 * qi
        nxt = jnp.minimum(qi + 1, nq - 1)
        step([(0, t, 0, True)], [(2, qi, t + 2, 2 * tk), (3, qi, t + 3, 3 * tk)])
        step([(1, t + 1, tk, True)], [(0, nxt, 0, 0)])
        step([(2, t + 2, 2 * tk, True), (3, t + 3, 3 * tk, True)], [(1, nxt, 1, 0)])
        acc = acc_scr[...]
        out_t = acc[:V_DIM, :] / acc[V_DIM:V_DIM + 1, :]
        o_ref[pl.ds(pl.multiple_of(qi * tq, tq), tq), :] = out_t.T.astype(BF16)
        restart()
        return 0

    lax.fori_loop(0, nq, q_tile, 0)


def _attn_prompt(q, kn, kp, vt):
    b, s, _ = q.shape
    tq, tk = ATTN_Q_TILE, ATTN_KV_TILE
    assert tq % (2 * tk) == 0 and s % tq == 0 and vt.shape[2:] == (s // tk, V_DIM, tk)
    stat_rows = 8
    acc_rows = V_DIM + 16
    return pl.pallas_call(
        functools.partial(_attn_prompt_kernel, tq=tq, tk=tk),
        grid=(b, MLA_HEADS),
        in_specs=[pl.BlockSpec((None, s, QK_PAD), lambda i, h: (i, 0, h)),
                  pl.BlockSpec((None, s, NOPE_DIM), lambda i, h: (i, 0, h)),
                  pl.BlockSpec((None, s, LANES), lambda i, h: (i, 0, 0)),
                  pl.BlockSpec((None, None) + vt.shape[2:], lambda i, h: (i, h, 0, 0, 0))],
        out_specs=pl.BlockSpec((None, s, V_DIM), lambda i, h: (i, 0, h)),
        out_shape=jax.ShapeDtypeStruct((b, s, MLA_HEADS * V_DIM), BF16),
        scratch_shapes=[pltpu.VMEM((4, tk, tq), F32), pltpu.VMEM((4, stat_rows, tq), F32),
                        pltpu.VMEM((stat_rows, tq), F32), pltpu.VMEM((acc_rows, tq), F32)],
        compiler_params=_params("parallel", "parallel"),
        name="attn_prompt",
    )(q, kn, kp, vt)


def _attn_sample_kernel(q_ref, cckv_ref, ckpe_t_ref, nckv_ref, nkpe_ref, wuk_ref, wuv_ref, o_ref):
    n = q_ref.shape[0]
    q_lat, q_rope = [], []
    for h in range(MLA_HEADS):
        q_nope = q_ref[:, h * QK_PAD:h * QK_PAD + NOPE_DIM]
        q_lat.append(_dot_nt(q_nope, wuk_ref[:, h * NOPE_DIM:(h + 1) * NOPE_DIM]).astype(BF16))
        q_rope.append(q_ref[:, h * QK_PAD + NOPE_DIM:h * QK_PAD + NOPE_DIM + ROPE_DIM])
    q_lat = jnp.concatenate(q_lat, axis=0)
    q_rope = jnp.concatenate(q_rope, axis=0)
    vc = cckv_ref[...].astype(BF16)
    vn = nckv_ref[...].astype(BF16)
    sc = _dot_nt(q_lat, vc) + _dot(q_rope, ckpe_t_ref[...].astype(BF16))
    sn = _dot_nt(q_lat, vn) + _dot_nt(q_rope, nkpe_ref[...].astype(BF16))
    m = jnp.maximum(jnp.max(sc, axis=-1, keepdims=True), jnp.max(sn, axis=-1, keepdims=True))
    pc = jnp.exp2(sc - m)
    pn = jnp.exp2(sn - m)
    l = jnp.sum(pc, axis=-1, keepdims=True) + jnp.sum(pn, axis=-1, keepdims=True)
    o_lat = ((_dot(pc.astype(BF16), vc) + _dot(pn.astype(BF16), vn)) / l).astype(BF16)
    for h in range(MLA_HEADS):
        sl = slice(h * V_DIM, (h + 1) * V_DIM)
        o_ref[:, sl] = _dot(o_lat[h * n:(h + 1) * n], wuv_ref[:, sl]).astype(BF16)


def _attn_sample(q, cache_ckv, cache_kpe_t, new_ckv, new_kpe, wuk, wuv):
    b, s, _ = q.shape

    def per_stream(a):
        return pl.BlockSpec((None,) + a.shape[1:], lambda i: (i, 0, 0))

    return pl.pallas_call(
        _attn_sample_kernel,
        grid=(b,),
        in_specs=[per_stream(q), per_stream(cache_ckv), per_stream(cache_kpe_t), per_stream(new_ckv),
                  per_stream(new_kpe), _const_spec(wuk.shape), _const_spec(wuv.shape)],
        out_specs=pl.BlockSpec((None, s, MLA_HEADS * V_DIM), lambda i: (i, 0, 0)),
        out_shape=jax.ShapeDtypeStruct((b, s, MLA_HEADS * V_DIM), BF16),
        compiler_params=_params("parallel"),
        name="attn_sample",
    )(q, cache_ckv, cache_kpe_t, new_ckv, new_kpe, wuk, wuv)


def _mix_kernel(x_ref, ret_ref, mla_ref, nw_ref, wg_ref, wro_ref, wmo_ref, wo_ref, h_ref):
    d = x_ref.shape[1]
    v_w = ret_ref.shape[1]
    x = x_ref[...]
    xn = _rms(x, nw_ref[...]).astype(BF16)
    rg = _dot_nt(xn, wg_ref[0:v_w, :])
    ret_b = _dot((ret_ref[...].astype(F32) * _silu(rg)).astype(BF16), wro_ref[...])
    ga = _dot_nt(xn, wg_ref[v_w:v_w + d, :])
    merged = _sigmoid(ga) * ret_b
    mla_b = _dot(mla_ref[...], wmo_ref[...])
    gb = _dot_nt(xn, wg_ref[v_w + d:v_w + 2 * d, :])
    merged = merged + _sigmoid(gb) * mla_b
    h_ref[...] = x + _dot(merged.astype(BF16), wo_ref[...])


def _mix(x2, ret2, mla2, nw, wg, wro, wmo, wo):
    n, d = x2.shape
    tm = min(WIDE_TOKEN_TILE, n)
    return pl.pallas_call(
        _mix_kernel,
        grid=(n // tm,),
        in_specs=[_row_spec(tm, d), _row_spec(tm, ret2.shape[1]), _row_spec(tm, mla2.shape[1]),
                  _const_spec(nw.shape), _const_spec(wg.shape), _const_spec(wro.shape),
                  _const_spec(wmo.shape), _const_spec(wo.shape)],
        out_specs=_row_spec(tm, d),
        out_shape=jax.ShapeDtypeStruct((n, d), F32),
        compiler_params=_params("parallel"),
        name="mix",
    )(x2, ret2, mla2, nw, wg, wro, wmo, wo)


def _ffn_kernel(h_ref, nw_ref, wgu_ref, wd_ref, fw_ref, y_ref, *, final_norm):
    h = h_ref[...]
    hn = _rms(h, nw_ref[...]).astype(BF16)
    acc = h
    d_ff = wd_ref.shape[0]
    for c in range(d_ff // FF_CHUNK):
        sl = slice(c * FF_CHUNK, (c + 1) * FF_CHUNK)
        g = _dot(hn, wgu_ref[:, sl])
        u = _dot(hn, wgu_ref[:, d_ff + c * FF_CHUNK:d_ff + (c + 1) * FF_CHUNK])
        acc = acc + _dot((_silu(g) * u).astype(BF16), wd_ref[sl, :])
    y_ref[...] = _rms(acc, fw_ref[...]) if final_norm else acc


def _ffn(h2, nw, wgu, wd, fw, final_norm):
    n, d = h2.shape
    tm = min(WIDE_TOKEN_TILE, n)
    return pl.pallas_call(
        functools.partial(_ffn_kernel, final_norm=final_norm),
        grid=(n // tm,),
        in_specs=[_row_spec(tm, d), _const_spec(nw.shape), _const_spec(wgu.shape), _const_spec(wd.shape),
                  _const_spec(fw.shape)],
        out_specs=_row_spec(tm, d),
        out_shape=jax.ShapeDtypeStruct((n, d), F32),
        compiler_params=_params("parallel"),
        name="ffn",
    )(h2, nw, wgu, wd, fw)


def _rope_tables(start, n, reps):
    pos = np.arange(start, start + n, dtype=np.float64)[:, None]

    def cs(d):
        inv = ROPE_BASE ** (-np.arange(0, d, 2, dtype=np.float64) / d)
        ang = pos * inv[None, :]
        return [np.tile(f(ang).astype(np.float32), (reps, 1)) for f in (np.cos, np.sin)]

    return tuple(jnp.asarray(t) for t in cs(RET_DK) + cs(ROPE_DIM))


def _regroup_kernel(w_ref, *out_refs, plans, axis):
    for o_ref, plan in zip(out_refs, plans):
        off = 0
        for start, width in plan:
            dst = (slice(off, off + width), slice(None))[::1 if axis == 0 else -1]
            if start is None:
                o_ref[dst] = jnp.zeros(o_ref[dst].shape, o_ref.dtype)
            else:
                src = (slice(start, start + width), slice(None))[::1 if axis == 0 else -1]
                o_ref[dst] = w_ref[src].astype(o_ref.dtype)
            off += width


def _regroup_bf16(w, plans, axis):
    other = w.shape[1 - axis]
    blk = min(other, WEIGHT_STREAM_BLOCK)
    widths = [sum(width for _, width in plan) for plan in plans]

    def spec(width):
        return (pl.BlockSpec((width, blk), lambda i: (0, i)) if axis == 0 else
                pl.BlockSpec((blk, width), lambda i: (i, 0)))

    def shape(width):
        return (width, other) if axis == 0 else (other, width)

    return pl.pallas_call(
        functools.partial(_regroup_kernel, plans=plans, axis=axis),
        grid=(other // blk,),
        in_specs=[spec(w.shape[axis])],
        out_specs=[spec(wd) for wd in widths],
        out_shape=[jax.ShapeDtypeStruct(shape(wd), BF16) for wd in widths],
        compiler_params=_params("parallel"),
        name="regroup",
    )(w)


def _prep_layer_weights(w_in, w_uq, w_ukv, w_ret_o, w_mla_o, w_out, w_gate_up, w_down):
    qk_w, v_w = RET_HEADS * RET_DK, RET_HEADS * RET_DV
    q_lora = w_uq.shape[0]
    kv_lora = w_ukv.shape[0]
    o_rg = 2 * qk_w + v_w
    o_cq = o_rg + v_w
    o_kpe = o_cq + q_lora + kv_lora
    o_ga = o_kpe + ROPE_DIM
    wa, wg = _regroup_bf16(jnp.swapaxes(w_in, 0, 1),
                           [[(0, o_rg), (o_cq, o_ga - o_cq), (None, LANES - ROPE_DIM)],
                            [(o_rg, v_w), (o_ga, w_in.shape[1] - o_ga)]], axis=0)
    head_w = NOPE_DIM + ROPE_DIM
    wuq, = _regroup_bf16(w_uq, [[piece for h in range(MLA_HEADS)
                                 for piece in ((h * head_w, head_w), (None, QK_PAD - head_w))]], axis=1)
    kv_w = NOPE_DIM + V_DIM
    wuk, wuv = _regroup_bf16(w_ukv, [[(h * kv_w, NOPE_DIM) for h in range(MLA_HEADS)],
                                     [(h * kv_w + NOPE_DIM, V_DIM) for h in range(MLA_HEADS)]], axis=1)
    assert w_down.shape[0] % FF_CHUNK == 0
    return dict(wa=wa, wg=wg, wuq=wuq, wuk=wuk, wuv=wuv, wro=w_ret_o.astype(BF16), wmo=w_mla_o.astype(BF16),
                wo=w_out.astype(BF16), wgu=w_gate_up.astype(BF16), wd=w_down.astype(BF16))


def _layer(x, tabs, tab_tiles, lw, norms, state0, cache, fw, final_norm):
    b, s, d = x.shape
    n = b * s
    x2 = x.reshape(n, d)
    nmw, qnw, kvnw, gnw, nfw = norms
    wukv = (lw["wuk"], lw["wuv"])
    if cache is None:
        ret, s_fin, q, ckv, kpe, kn, kp, vt = _inproj(x2, tabs, nmw, lw["wa"], qnw, kvnw, lw["wuq"],
                                                      (lw["wuk"], lw["wuv"].T, state0, gnw), tab_tiles=tab_tiles)
    else:
        rq, rk, rv, q, ckv, kpe = _inproj(x2, tabs, nmw, lw["wa"], qnw, kvnw, lw["wuq"], None, tab_tiles=tab_tiles)
        ret, s_fin = _retention(rq.reshape(b, s, -1), rk.reshape(b, s, -1), rv.reshape(b, s, -1), state0, gnw, s)
    ckv3 = ckv.reshape(b, s, -1)
    kpe3 = jnp.swapaxes(kpe, 1, 2) if cache is None else kpe.reshape(b, s, ROPE_DIM)
    q3 = q.reshape(b, s, MLA_HEADS * QK_PAD)
    if cache is None:
        mla = _attn_prompt(q3, kn.reshape(b, s, -1), kp.reshape(b, s, -1), vt)
    else:
        mla = _attn_sample(q3, cache[0].astype(F32), jnp.swapaxes(cache[1].astype(F32), 1, 2), ckv3, kpe3, *wukv)
    h2 = _mix(x2, ret.reshape(n, -1), mla.reshape(n, -1), nmw, lw["wg"], lw["wro"], lw["wmo"], lw["wo"])
    y2 = _ffn(h2, nfw, lw["wgu"], lw["wd"], fw, final_norm)
    return y2.reshape(b, s, d), (ckv3, kpe3, s_fin)


def kernel(x_prompt, x_sample, cache_ckv, cache_kpe, state_ret, norm_mix_w, w_in, q_norm_w, w_uq, kv_norm_w, w_ukv, ret_gn_w, w_ret_o, w_mla_o, w_out, norm_ffn_w, w_gate_up, w_down, norm_final_w):
    depth = w_in.shape[0]
    bp, sp, _ = x_prompt.shape
    bs, ss, _ = x_sample.shape
    past = cache_ckv.shape[2]
    assert sp % TOKEN_TILE == 0 and sp % ATTN_Q_TILE == 0 and (bs * ss) % min(TOKEN_TILE, bs * ss) == 0
    assert ss == CHUNK, "the sample group is one streaming chunk"

    tabs_p = _rope_tables(0, sp, 1)
    tabs_s = _rope_tables(past, ss, bs)
    fw = norm_final_w.reshape(1, -1)
    state0_p = jnp.zeros((bp, RET_HEADS, RET_DK, RET_DV), F32)

    hp, hs = x_prompt, x_sample
    outs = [[] for _ in range(6)]
    for l in range(depth):
        lw = _prep_layer_weights(w_in[l], w_uq[l], w_ukv[l], w_ret_o[l], w_mla_o[l], w_out[l], w_gate_up[l],
                                 w_down[l])
        norms = (norm_mix_w[l].reshape(1, -1), q_norm_w[l].reshape(1, -1), kv_norm_w[l].reshape(1, -1),
                 ret_gn_w[l].reshape(1, -1), norm_ffn_w[l].reshape(1, -1))
        final = l == depth - 1
        hp, (a, b_, c) = _layer(hp, tabs_p, sp // TOKEN_TILE, lw, norms, state0_p, None, fw, final)
        hs, (d_, e, f) = _layer(hs, tabs_s, 1, lw, norms, state_ret[l].astype(F32),
                                (cache_ckv[l], cache_kpe[l]), fw, final)
        for lst, val in zip(outs, (a, b_, c, d_, e, f)):
            lst.append(val)
    return (hp, hs) + tuple(o[0][None] if depth == 1 else jnp.stack(o) for o in outs)
```

```python
import functools
import math

import jax
import jax.numpy as jnp
import numpy as np
from jax import lax
from jax.experimental import pallas as pl
from jax.experimental.pallas import tpu as pltpu

F32 = jnp.float32
BF16 = jnp.bfloat16

CHUNK = 64
RET_HEADS = 4
RET_DK = 128
RET_DV = 256
MLA_HEADS = 8
NOPE_DIM = 128
ROPE_DIM = 64
V_DIM = 128
ROPE_BASE = 10000.0
EPS = 1e-6

LANES = 128
QK_PAD = 2 * LANES
V7X_VMEM_BYTES = 64 * 2**20
VMEM_LIMIT = V7X_VMEM_BYTES * 7 // 8
TOKEN_TILE = 512
WIDE_TOKEN_TILE = 1024
WEIGHT_STREAM_BLOCK = 128
RET_BLOCK = 256
ATTN_Q_TILE = 2048
ATTN_KV_TILE = 512
ATTN_COL_BLOCK = 256
FF_CHUNK = 256
NEG_BIG = -1e30
SOFTMAX_C = (NOPE_DIM + ROPE_DIM) ** -0.5 * math.log2(math.e)

_dot = functools.partial(jnp.dot, preferred_element_type=F32)


def _dot_nt(a, b):
    return lax.dot_general(a, b, (((1,), (1,)), ((), ())), preferred_element_type=F32)


def _dot_tn(a, b):
    return lax.dot_general(a, b, (((0,), (0,)), ((), ())), preferred_element_type=F32)


def _rms(x, g):
    ms = jnp.mean(x * x, axis=-1, keepdims=True)
    return x * lax.rsqrt(ms + EPS) * g


def _sigmoid(x):
    return 1.0 / (1.0 + jnp.exp(-x))


def _silu(x):
    return x * _sigmoid(x)


def _params(*sem):
    return pltpu.CompilerParams(dimension_semantics=sem, vmem_limit_bytes=VMEM_LIMIT)


def _const_spec(shape):
    return pl.BlockSpec(shape, lambda *_: (0,) * len(shape), pipeline_mode=pl.Buffered(1))


def _row_spec(tm, width):
    return pl.BlockSpec((tm, width), lambda i: (i, 0))


def _rope_full(x, cos2, sin2):
    return x * cos2 + pltpu.roll(x, LANES // 2, 1) * sin2


def _rope_half(x, c, sa, sb):
    q = ROPE_DIM // 2
    return x * c + pltpu.roll(x, LANES - q, 1) * sa + pltpu.roll(x, q, 1) * sb


def _ret_decay(blk, h):
    lg = math.log(1.0 - 2.0 ** (-5.0 - h))
    ri = lax.broadcasted_iota(jnp.int32, (blk, blk), 0)
    ci = lax.broadcasted_iota(jnp.int32, (blk, blk), 1)
    diff = (ri - ci).astype(F32)
    n = lax.broadcasted_iota(jnp.int32, (blk, RET_DK), 0).astype(F32)
    decay = jnp.where(diff >= 0, jnp.exp(lg * jnp.maximum(diff, 0.0)), 0.0)
    return decay, jnp.exp(lg * (n + 1.0)), jnp.exp(lg * (blk - 1.0 - n)), math.exp(lg * blk)


def _ret_block(q, k, v, state_ref, h, dec, gnw):
    decay, q_dec, k_dec, s_dec = dec
    v = v.astype(BF16)
    scores = _dot_nt(q.astype(BF16), k.astype(BF16)) * decay
    st = state_ref[h]
    y = _dot(scores.astype(BF16), v) + _dot((q * q_dec).astype(BF16), st.astype(BF16))
    state_ref[h] = s_dec * st + _dot_tn((k * k_dec).astype(BF16), v)
    yc = y - jnp.mean(y, axis=-1, keepdims=True)
    return yc * lax.rsqrt(jnp.mean(yc * yc, axis=-1, keepdims=True) + EPS) * gnw


def _inproj_kernel(x_ref, nw_ref, wa_ref, qnw_ref, kvnw_ref, wuq_ref, cr_ref, sr_ref, cm_ref, sm_ref, *rest,
                   prompt, seq_tiles):
    if prompt:
        (wuk_ref, wuvt_ref, s0_ref, gnw_ref, ret_ref, sfin_ref, q_ref, ckv_ref, kpe_ref, kn_ref, kp_ref, vt_ref,
         state_scr, dec_scr, qdec_scr, kdec_scr) = rest
    else:
        rq_ref, rk_ref, rv_ref, q_ref, ckv_ref, kpe_ref = rest
    qk_w = RET_HEADS * RET_DK
    v_w = RET_HEADS * RET_DV
    q_lora = wuq_ref.shape[0]
    kv_lora = ckv_ref.shape[1]
    o_rk, o_rv, o_cq = qk_w, 2 * qk_w, 2 * qk_w + v_w
    o_ckv = o_cq + q_lora
    o_kpe = o_ckv + kv_lora

    if prompt:
        i = pl.program_id(0)
        blk = dec_scr.shape[1]

        @pl.when(i == 0)
        def _():
            for h in range(RET_HEADS):
                dec_scr[h], qdec_scr[h], kdec_scr[h], _ = _ret_decay(blk, h)

        @pl.when(i % seq_tiles == 0)
        def _():
            state_scr[...] = s0_ref[...]

    xn = _rms(x_ref[...], nw_ref[...]).astype(BF16)
    c, s = cr_ref[...], sr_ref[...]
    cr = jnp.concatenate([c, c], axis=1)
    sr = jnp.concatenate([-s, s], axis=1)
    c, s = cm_ref[...], sm_ref[...]
    z = jnp.zeros_like(s)
    cm = jnp.concatenate([c, c, c, c], axis=1)
    sa = jnp.concatenate([-s, z, -s, z], axis=1)
    sb = jnp.concatenate([z, s, z, s], axis=1)

    zq = _dot_nt(xn, wa_ref[0:qk_w, :])
    zk = _dot_nt(xn, wa_ref[o_rk:o_rk + qk_w, :])
    rv = _dot_nt(xn, wa_ref[o_rv:o_rv + v_w, :])
    k_scale = RET_DK ** -0.5
    rq = [_rope_full(zq[:, h * RET_DK:(h + 1) * RET_DK], cr, sr) for h in range(RET_HEADS)]
    rk = [_rope_full(zk[:, h * RET_DK:(h + 1) * RET_DK], cr, sr) * k_scale for h in range(RET_HEADS)]
    if prompt:
        for b in range(x_ref.shape[0] // blk):
            r = slice(b * blk, (b + 1) * blk)
            for h in range(RET_HEADS):
                sl = slice(h * RET_DV, (h + 1) * RET_DV)
                dec = (dec_scr[h], qdec_scr[h], kdec_scr[h], math.exp(math.log(1.0 - 2.0 ** (-5.0 - h)) * blk))
                ret_ref[r, sl] = _ret_block(rq[h][r], rk[h][r], rv[r, sl], state_scr, h, dec,
                                            gnw_ref[:, sl]).astype(BF16)
    else:
        for h in range(RET_HEADS):
            sl = slice(h * RET_DK, (h + 1) * RET_DK)
            rq_ref[:, sl] = rq[h].astype(BF16)
            rk_ref[:, sl] = rk[h].astype(BF16)
        rv_ref[...] = rv.astype(BF16)

    cq = _dot_nt(xn, wa_ref[o_cq:o_cq + q_lora, :])
    cqn = _rms(cq, qnw_ref[...]).astype(BF16)
    for h in range(MLA_HEADS):
        qh = _dot(cqn, wuq_ref[:, h * QK_PAD:(h + 1) * QK_PAD]) * SOFTMAX_C
        q_ref[:, h * QK_PAD:h * QK_PAD + LANES] = qh[:, :LANES].astype(BF16)
        q_ref[:, h * QK_PAD + LANES:(h + 1) * QK_PAD] = _rope_half(qh[:, LANES:], cm, sa, sb).astype(BF16)

    ckv = _rms(_dot_nt(xn, wa_ref[o_ckv:o_ckv + kv_lora, :]), kvnw_ref[...])
    ckv_ref[...] = ckv
    kpe_pad = _rope_half(_dot_nt(xn, wa_ref[o_kpe:o_kpe + LANES, :]), cm, sa, sb)
    if prompt:
        kpe_ref[...] = kpe_pad.T[:ROPE_DIM, :]
    else:
        kpe_ref[...] = kpe_pad[:, :ROPE_DIM]
    if prompt:
        c16 = ckv.astype(BF16)
        kn_ref[...] = _dot(c16, wuk_ref[...]).astype(BF16)
        kp_ref[...] = kpe_pad.astype(BF16)
        v_t = _dot_nt(wuvt_ref[...], c16)
        for h in range(MLA_HEADS):
            vt_ref[h] = v_t[h * V_DIM:(h + 1) * V_DIM, :].astype(BF16)

        @pl.when(i % seq_tiles == seq_tiles - 1)
        def _():
            sfin_ref[...] = state_scr[...]


def _inproj(x2, tabs, nw, wa, qnw, kvnw, wuq, prompt_args, *, tab_tiles):
    n, d = x2.shape
    tm = min(TOKEN_TILE, n)
    qk_w, v_w = RET_HEADS * RET_DK, RET_HEADS * RET_DV
    kv_lora = kvnw.shape[1]
    prompt = prompt_args is not None
    tab_specs = [pl.BlockSpec((tm, t.shape[1]), lambda i: (i % tab_tiles, 0)) for t in tabs]
    in_specs = [_row_spec(tm, d), _const_spec(nw.shape), _const_spec(wa.shape), _const_spec(qnw.shape),
                _const_spec(kvnw.shape), _const_spec(wuq.shape)] + tab_specs
    mla_outs = [(MLA_HEADS * QK_PAD, BF16), (kv_lora, F32), (ROPE_DIM, F32)]
    scratch = []
    if prompt:
        wuk, wuvt, state0, gnw = prompt_args
        assert tm % RET_BLOCK == 0 and tm == ATTN_KV_TILE
        st_spec = pl.BlockSpec((None,) + state0.shape[1:], lambda i: (i // tab_tiles, 0, 0, 0))
        in_specs += [_const_spec(wuk.shape), _const_spec(wuvt.shape), st_spec, _const_spec(gnw.shape)]
        row_outs = [(v_w, BF16)] + mla_outs + [(MLA_HEADS * NOPE_DIM, BF16), (LANES, BF16)]
        out_specs = [_row_spec(tm, w) for w, _ in row_outs]
        out_shape = [jax.ShapeDtypeStruct((n, w), dt) for w, dt in row_outs]
        out_specs.insert(1, st_spec)
        out_shape.insert(1, jax.ShapeDtypeStruct(state0.shape, F32))
        seq = tab_tiles * tm
        out_specs[4] = pl.BlockSpec((None, ROPE_DIM, tm), lambda i: (i // tab_tiles, 0, i % tab_tiles))
        out_shape[4] = jax.ShapeDtypeStruct((n // seq, ROPE_DIM, seq), F32)
        out_specs.append(pl.BlockSpec((None, MLA_HEADS, None, V_DIM, tm),
                                      lambda i: (i // tab_tiles, 0, i % tab_tiles, 0, 0)))
        out_shape.append(jax.ShapeDtypeStruct((n // seq, MLA_HEADS, tab_tiles, V_DIM, tm), BF16))
        scratch = [pltpu.VMEM(state0.shape[1:], F32), pltpu.VMEM((RET_HEADS, RET_BLOCK, RET_BLOCK), F32),
                   pltpu.VMEM((RET_HEADS, RET_BLOCK, RET_DK), F32), pltpu.VMEM((RET_HEADS, RET_BLOCK, RET_DK), F32)]
    else:
        prompt_args = ()
        row_outs = [(qk_w, BF16), (qk_w, BF16), (v_w, BF16)] + mla_outs
        out_specs = [_row_spec(tm, w) for w, _ in row_outs]
        out_shape = [jax.ShapeDtypeStruct((n, w), dt) for w, dt in row_outs]
    return pl.pallas_call(
        functools.partial(_inproj_kernel, prompt=prompt, seq_tiles=tab_tiles),
        grid=(n // tm,),
        in_specs=in_specs,
        out_specs=out_specs,
        out_shape=out_shape,
        scratch_shapes=scratch,
        compiler_params=_params("arbitrary"),
        name="inproj",
    )(x2, nw, wa, qnw, kvnw, wuq, *tabs, *prompt_args)


def _retention_kernel(q_ref, k_ref, v_ref, s0_ref, gnw_ref, y_ref, sfin_ref, state_ref, *, blk, nblk):
    c = pl.program_id(1)

    @pl.when(c == 0)
    def _():
        state_ref[...] = s0_ref[...]

    for h in range(RET_HEADS):
        sl = slice(h * RET_DV, (h + 1) * RET_DV)
        q = q_ref[:, h * RET_DK:(h + 1) * RET_DK].astype(F32)
        k = k_ref[:, h * RET_DK:(h + 1) * RET_DK].astype(F32)
        y_ref[:, sl] = _ret_block(q, k, v_ref[:, sl], state_ref, h, _ret_decay(blk, h), gnw_ref[:, sl]).astype(BF16)

    @pl.when(c == nblk - 1)
    def _():
        sfin_ref[...] = state_ref[...]


def _retention(rq, rk, rv, state0, gnw, blk):
    b, s, _ = rq.shape
    nblk = s // blk
    qk_w, v_w = RET_HEADS * RET_DK, RET_HEADS * RET_DV
    st_spec = pl.BlockSpec((None, RET_HEADS, RET_DK, RET_DV), lambda i, c: (i, 0, 0, 0))
    return pl.pallas_call(
        functools.partial(_retention_kernel, blk=blk, nblk=nblk),
        grid=(b, nblk),
        in_specs=[pl.BlockSpec((None, blk, qk_w), lambda i, c: (i, c, 0)),
                  pl.BlockSpec((None, blk, qk_w), lambda i, c: (i, c, 0)),
                  pl.BlockSpec((None, blk, v_w), lambda i, c: (i, c, 0)),
                  st_spec,
                  pl.BlockSpec((1, v_w), lambda i, c: (0, 0))],
        out_specs=[pl.BlockSpec((None, blk, v_w), lambda i, c: (i, c, 0)), st_spec],
        out_shape=[jax.ShapeDtypeStruct((b, s, v_w), BF16),
                   jax.ShapeDtypeStruct((b, RET_HEADS, RET_DK, RET_DV), F32)],
        scratch_shapes=[pltpu.VMEM((RET_HEADS, RET_DK, RET_DV), F32)],
        compiler_params=_params("parallel", "arbitrary"),
        name="retention",
    )(rq, rk, rv, state0, gnw)


def _attn_prompt_kernel(q_ref, kn_ref, kp_ref, vt_ref, o_ref, s_scr, mx_scr, m_scr, acc_scr, *, tq, tk):
    nq = q_ref.shape[0] // tq
    assert tq == 4 * tk
    cb = ATTN_COL_BLOCK
    ones = jnp.ones((acc_scr.shape[0] - V_DIM, tk), BF16)

    def rows(ref, t, n):
        return ref[pl.ds(pl.multiple_of(t * n, n), n), :]

    def col_max(s):
        return jnp.broadcast_to(jnp.max(s, axis=0, keepdims=True), (mx_scr.shape[1], s.shape[1]))

    def park_block(b, qi, t, c):
        cols = slice(c * cb, (c + 1) * cb)
        k = jnp.concatenate([rows(kn_ref, t, tk), rows(kp_ref, t, tk)], axis=1)
        s = _dot_nt(k, q_ref[pl.ds(pl.multiple_of(qi * tq + c * cb, cb), cb), :])
        s_scr[b, :, cols] = s
        mx_scr[b, :, cols] = col_max(s)

    def consume_block(b, t, c, col0, masked):
        cols = slice(c * cb, (c + 1) * cb)
        s = s_scr[b, :, cols]
        mx = mx_scr[b, :, cols]
        if masked and c * cb - col0 < tk:
            key = lax.broadcasted_iota(jnp.int32, s.shape, 0) // CHUNK
            qry = (lax.broadcasted_iota(jnp.int32, s.shape, 1) + (c * cb - col0)) // CHUNK
            s = jnp.where(key <= qry, s, NEG_BIG)
            mx = col_max(s)
        m_old = m_scr[:, cols]
        m_new = jnp.maximum(m_old, mx)
        alpha = jnp.exp2(m_old - m_new)
        p = jnp.exp2(s - m_new[:1, :]).astype(BF16)
        m_scr[:, cols] = m_new
        v_ext = jnp.concatenate([vt_ref[t], ones], axis=0)
        acc_scr[:, cols] = alpha[:1, :] * acc_scr[:, cols] + _dot(v_ext, p)

    def step(consumes, parks):
        cons = [functools.partial(consume_block, b, t, c, col0, masked)
                for b, t, col0, masked in consumes for c in range(col0 // cb, tq // cb)]
        prks = [functools.partial(park_block, b, qi, t, c)
                for b, qi, t, col0 in parks for c in range(col0 // cb, tq // cb)]
        i = j = 0
        while i < len(cons) or j < len(prks):
            if j < len(prks) and (i == len(cons) or j * len(cons) <= i * len(prks)):
                prks[j]()
                j += 1
            else:
                cons[i]()
                i += 1

    def restart():
        m_scr[...] = jnp.full_like(m_scr, NEG_BIG)
        acc_scr[...] = jnp.zeros_like(acc_scr)

    def diagonal(qi, second_parked):
        t = 4 * qi
        nxt = jnp.minimum(qi + 1, nq - 1)
        late = [(2, qi, t + 2, 2 * tk), (3, qi, t + 3, 3 * tk)]
        step([(0, t, 0, True)], late if second_parked else [(1, qi, t + 1, tk)] + late)
        step([(1, t + 1, tk, True)], [(0, nxt, 0, 0)])
        step([(2, t + 2, 2 * tk, True), (3, t + 3, 3 * tk, True)], [(1, nxt, 1, 0)])
        acc = acc_scr[...]
        out_t = acc[:V_DIM, :] / acc[V_DIM:V_DIM + 1, :]
        o_ref[pl.ds(pl.multiple_of(qi * tq, tq), tq), :] = out_t.T.astype(BF16)
        restart()

    def q_tile(qi, _):
        def trip(j, _):
            t = 4 * j
            step([(0, t, 0, False)], [(2, qi, t + 2, 0)])
            step([(1, t + 1, 0, False)], [(3, qi, t + 3, 0)])
            step([(2, t + 2, 0, False)], [(0, qi, t + 4, 0)])
            step([(3, t + 3, 0, False)], [(1, qi, t + 5, 0)])
            return 0

        lax.fori_loop(0, qi, trip, 0)
        diagonal(qi, True)
        return 0

    restart()
    step([], [(0, 0, 0, 0)])
    diagonal(0, False)
    lax.fori_loop(1, nq, q_tile, 0)


def _attn_prompt(q, kn, kp, vt):
    b, s, _ = q.shape
    tq, tk = ATTN_Q_TILE, ATTN_KV_TILE
    assert tq % (2 * tk) == 0 and s % tq == 0 and vt.shape[2:] == (s // tk, V_DIM, tk)
    stat_rows = 8
    acc_rows = V_DIM + 16
    return pl.pallas_call(
        functools.partial(_attn_prompt_kernel, tq=tq, tk=tk),
        grid=(b, MLA_HEADS),
        in_specs=[pl.BlockSpec((None, s, QK_PAD), lambda i, h: (i, 0, h)),
                  pl.BlockSpec((None, s, NOPE_DIM), lambda i, h: (i, 0, h)),
                  pl.BlockSpec((None, s, LANES), lambda i, h: (i, 0, 0)),
                  pl.BlockSpec((None, None) + vt.shape[2:], lambda i, h: (i, h, 0, 0, 0))],
        out_specs=pl.BlockSpec((None, s, V_DIM), lambda i, h: (i, 0, h)),
        out_shape=jax.ShapeDtypeStruct((b, s, MLA_HEADS * V_DIM), BF16),
        scratch_shapes=[pltpu.VMEM((4, tk, tq), F32), pltpu.VMEM((4, stat_rows, tq), F32),
                        pltpu.VMEM((stat_rows, tq), F32), pltpu.VMEM((acc_rows, tq), F32)],
        compiler_params=_params("parallel", "parallel"),
        name="attn_prompt",
    )(q, kn, kp, vt)


def _attn_sample_kernel(q_ref, cckv_ref, ckpe_t_ref, nckv_ref, nkpe_ref, wuk_ref, wuv_ref, o_ref):
    n = q_ref.shape[0]
    q_lat, q_rope = [], []
    for h in range(MLA_HEADS):
        q_nope = q_ref[:, h * QK_PAD:h * QK_PAD + NOPE_DIM]
        q_lat.append(_dot_nt(q_nope, wuk_ref[:, h * NOPE_DIM:(h + 1) * NOPE_DIM]).astype(BF16))
        q_rope.append(q_ref[:, h * QK_PAD + NOPE_DIM:h * QK_PAD + NOPE_DIM + ROPE_DIM])
    q_lat = jnp.concatenate(q_lat, axis=0)
    q_rope = jnp.concatenate(q_rope, axis=0)
    vc = cckv_ref[...].astype(BF16)
    vn = nckv_ref[...].astype(BF16)
    sc = _dot_nt(q_lat, vc) + _dot(q_rope, ckpe_t_ref[...].astype(BF16))
    sn = _dot_nt(q_lat, vn) + _dot_nt(q_rope, nkpe_ref[...].astype(BF16))
    m = jnp.maximum(jnp.max(sc, axis=-1, keepdims=True), jnp.max(sn, axis=-1, keepdims=True))
    pc = jnp.exp2(sc - m)
    pn = jnp.exp2(sn - m)
    l = jnp.sum(pc, axis=-1, keepdims=True) + jnp.sum(pn, axis=-1, keepdims=True)
    o_lat = ((_dot(pc.astype(BF16), vc) + _dot(pn.astype(BF16), vn)) / l).astype(BF16)
    for h in range(MLA_HEADS):
        sl = slice(h * V_DIM, (h + 1) * V_DIM)
        o_ref[:, sl] = _dot(o_lat[h * n:(h + 1) * n], wuv_ref[:, sl]).astype(BF16)


def _attn_sample(q, cache_ckv, cache_kpe_t, new_ckv, new_kpe, wuk, wuv):
    b, s, _ = q.shape

    def per_stream(a):
        return pl.BlockSpec((None,) + a.shape[1:], lambda i: (i, 0, 0))

    return pl.pallas_call(
        _attn_sample_kernel,
        grid=(b,),
        in_specs=[per_stream(q), per_stream(cache_ckv), per_stream(cache_kpe_t), per_stream(new_ckv),
                  per_stream(new_kpe), _const_spec(wuk.shape), _const_spec(wuv.shape)],
        out_specs=pl.BlockSpec((None, s, MLA_HEADS * V_DIM), lambda i: (i, 0, 0)),
        out_shape=jax.ShapeDtypeStruct((b, s, MLA_HEADS * V_DIM), BF16),
        compiler_params=_params("parallel"),
        name="attn_sample",
    )(q, cache_ckv, cache_kpe_t, new_ckv, new_kpe, wuk, wuv)


def _mix_kernel(x_ref, ret_ref, mla_ref, nw_ref, wg_ref, wro_ref, wmo_ref, wo_ref, h_ref):
    d = x_ref.shape[1]
    v_w = ret_ref.shape[1]
    x = x_ref[...]
    xn = _rms(x, nw_ref[...]).astype(BF16)
    rg = _dot_nt(xn, wg_ref[0:v_w, :])
    ret_b = _dot((ret_ref[...].astype(F32) * _silu(rg)).astype(BF16), wro_ref[...])
    ga = _dot_nt(xn, wg_ref[v_w:v_w + d, :])
    merged = _sigmoid(ga) * ret_b
    mla_b = _dot(mla_ref[...], wmo_ref[...])
    gb = _dot_nt(xn, wg_ref[v_w + d:v_w + 2 * d, :])
    merged = merged + _sigmoid(gb) * mla_b
    h_ref[...] = x + _dot(merged.astype(BF16), wo_ref[...])


def _mix(x2, ret2, mla2, nw, wg, wro, wmo, wo):
    n, d = x2.shape
    tm = min(WIDE_TOKEN_TILE, n)
    return pl.pallas_call(
        _mix_kernel,
        grid=(n // tm,),
        in_specs=[_row_spec(tm, d), _row_spec(tm, ret2.shape[1]), _row_spec(tm, mla2.shape[1]),
                  _const_spec(nw.shape), _const_spec(wg.shape), _const_spec(wro.shape),
                  _const_spec(wmo.shape), _const_spec(wo.shape)],
        out_specs=_row_spec(tm, d),
        out_shape=jax.ShapeDtypeStruct((n, d), F32),
        compiler_params=_params("parallel"),
        name="mix",
    )(x2, ret2, mla2, nw, wg, wro, wmo, wo)


def _ffn_kernel(h_ref, nw_ref, wgu_ref, wd_ref, fw_ref, y_ref, *, final_norm):
    h = h_ref[...]
    hn = _rms(h, nw_ref[...]).astype(BF16)
    acc = h
    d_ff = wd_ref.shape[0]
    for c in range(d_ff // FF_CHUNK):
        sl = slice(c * FF_CHUNK, (c + 1) * FF_CHUNK)
        g = _dot(hn, wgu_ref[:, sl])
        u = _dot(hn, wgu_ref[:, d_ff + c * FF_CHUNK:d_ff + (c + 1) * FF_CHUNK])
        acc = acc + _dot((_silu(g) * u).astype(BF16), wd_ref[sl, :])
    y_ref[...] = _rms(acc, fw_ref[...]) if final_norm else acc


def _ffn(h2, nw, wgu, wd, fw, final_norm):
    n, d = h2.shape
    tm = min(WIDE_TOKEN_TILE, n)
    return pl.pallas_call(
        functools.partial(_ffn_kernel, final_norm=final_norm),
        grid=(n // tm,),
        in_specs=[_row_spec(tm, d), _const_spec(nw.shape), _const_spec(wgu.shape), _const_spec(wd.shape),
                  _const_spec(fw.shape)],
        out_specs=_row_spec(tm, d),
        out_shape=jax.ShapeDtypeStruct((n, d), F32),
        compiler_params=_params("parallel"),
        name="ffn",
    )(h2, nw, wgu, wd, fw)


def _rope_tables(start, n, reps):
    pos = np.arange(start, start + n, dtype=np.float64)[:, None]

    def cs(d):
        inv = ROPE_BASE ** (-np.arange(0, d, 2, dtype=np.float64) / d)
        ang = pos * inv[None, :]
        return [np.tile(f(ang).astype(np.float32), (reps, 1)) for f in (np.cos, np.sin)]

    return tuple(jnp.asarray(t) for t in cs(RET_DK) + cs(ROPE_DIM))


def _regroup_kernel(w_ref, *out_refs, plans, axis):
    for o_ref, plan in zip(out_refs, plans):
        off = 0
        for start, width in plan:
            dst = (slice(off, off + width), slice(None))[::1 if axis == 0 else -1]
            if start is None:
                o_ref[dst] = jnp.zeros(o_ref[dst].shape, o_ref.dtype)
            else:
                src = (slice(start, start + width), slice(None))[::1 if axis == 0 else -1]
                o_ref[dst] = w_ref[src].astype(o_ref.dtype)
            off += width


def _regroup_bf16(w, plans, axis):
    other = w.shape[1 - axis]
    blk = min(other, WEIGHT_STREAM_BLOCK)
    widths = [sum(width for _, width in plan) for plan in plans]

    def spec(width):
        return (pl.BlockSpec((width, blk), lambda i: (0, i)) if axis == 0 else
                pl.BlockSpec((blk, width), lambda i: (i, 0)))

    def shape(width):
        return (width, other) if axis == 0 else (other, width)

    return pl.pallas_call(
        functools.partial(_regroup_kernel, plans=plans, axis=axis),
        grid=(other // blk,),
        in_specs=[spec(w.shape[axis])],
        out_specs=[spec(wd) for wd in widths],
        out_shape=[jax.ShapeDtypeStruct(shape(wd), BF16) for wd in widths],
        compiler_params=_params("parallel"),
        name="regroup",
    )(w)


def _prep_layer_weights(w_in, w_uq, w_ukv, w_ret_o, w_mla_o, w_out, w_gate_up, w_down):
    qk_w, v_w = RET_HEADS * RET_DK, RET_HEADS * RET_DV
    q_lora = w_uq.shape[0]
    kv_lora = w_ukv.shape[0]
    o_rg = 2 * qk_w + v_w
    o_cq = o_rg + v_w
    o_kpe = o_cq + q_lora + kv_lora
    o_ga = o_kpe + ROPE_DIM
    wa, wg = _regroup_bf16(jnp.swapaxes(w_in, 0, 1),
                           [[(0, o_rg), (o_cq, o_ga - o_cq), (None, LANES - ROPE_DIM)],
                            [(o_rg, v_w), (o_ga, w_in.shape[1] - o_ga)]], axis=0)
    head_w = NOPE_DIM + ROPE_DIM
    wuq, = _regroup_bf16(w_uq, [[piece for h in range(MLA_HEADS)
                                 for piece in ((h * head_w, head_w), (None, QK_PAD - head_w))]], axis=1)
    kv_w = NOPE_DIM + V_DIM
    wuk, wuv = _regroup_bf16(w_ukv, [[(h * kv_w, NOPE_DIM) for h in range(MLA_HEADS)],
                                     [(h * kv_w + NOPE_DIM, V_DIM) for h in range(MLA_HEADS)]], axis=1)
    assert w_down.shape[0] % FF_CHUNK == 0
    return dict(wa=wa, wg=wg, wuq=wuq, wuk=wuk, wuv=wuv, wro=w_ret_o.astype(BF16), wmo=w_mla_o.astype(BF16),
                wo=w_out.astype(BF16), wgu=w_gate_up.astype(BF16), wd=w_down.astype(BF16))


def _layer(x, tabs, tab_tiles, lw, norms, state0, cache, fw, final_norm):
    b, s, d = x.shape
    n = b * s
    x2 = x.reshape(n, d)
    nmw, qnw, kvnw, gnw, nfw = norms
    wukv = (lw["wuk"], lw["wuv"])
    if cache is None:
        ret, s_fin, q, ckv, kpe, kn, kp, vt = _inproj(x2, tabs, nmw, lw["wa"], qnw, kvnw, lw["wuq"],
                                                      (lw["wuk"], lw["wuv"].T, state0, gnw), tab_tiles=tab_tiles)
    else:
        rq, rk, rv, q, ckv, kpe = _inproj(x2, tabs, nmw, lw["wa"], qnw, kvnw, lw["wuq"], None, tab_tiles=tab_tiles)
        ret, s_fin = _retention(rq.reshape(b, s, -1), rk.reshape(b, s, -1), rv.reshape(b, s, -1), state0, gnw, s)
    ckv3 = ckv.reshape(b, s, -1)
    kpe3 = jnp.swapaxes(kpe, 1, 2) if cache is None else kpe.reshape(b, s, ROPE_DIM)
    q3 = q.reshape(b, s, MLA_HEADS * QK_PAD)
    if cache is None:
        mla = _attn_prompt(q3, kn.reshape(b, s, -1), kp.reshape(b, s, -1), vt)
    else:
        mla = _attn_sample(q3, cache[0].astype(F32), jnp.swapaxes(cache[1].astype(F32), 1, 2), ckv3, kpe3, *wukv)
    h2 = _mix(x2, ret.reshape(n, -1), mla.reshape(n, -1), nmw, lw["wg"], lw["wro"], lw["wmo"], lw["wo"])
    y2 = _ffn(h2, nfw, lw["wgu"], lw["wd"], fw, final_norm)
    return y2.reshape(b, s, d), (ckv3, kpe3, s_fin)


def kernel(x_prompt, x_sample, cache_ckv, cache_kpe, state_ret, norm_mix_w, w_in, q_norm_w, w_uq, kv_norm_w, w_ukv, ret_gn_w, w_ret_o, w_mla_o, w_out, norm_ffn_w, w_gate_up, w_down, norm_final_w):
    depth = w_in.shape[0]
    bp, sp, _ = x_prompt.shape
    bs, ss, _ = x_sample.shape
    past = cache_ckv.shape[2]
    assert sp % TOKEN_TILE == 0 and sp % ATTN_Q_TILE == 0 and (bs * ss) % min(TOKEN_TILE, bs * ss) == 0
    assert ss == CHUNK, "the sample group is one streaming chunk"

    tabs_p = _rope_tables(0, sp, 1)
    tabs_s = _rope_tables(past, ss, bs)
    fw = norm_final_w.reshape(1, -1)
    state0_p = jnp.zeros((bp, RET_HEADS, RET_DK, RET_DV), F32)

    hp, hs = x_prompt, x_sample
    outs = [[] for _ in range(6)]
    for l in range(depth):
        lw = _prep_layer_weights(w_in[l], w_uq[l], w_ukv[l], w_ret_o[l], w_mla_o[l], w_out[l], w_gate_up[l],
                                 w_down[l])
        norms = (norm_mix_w[l].reshape(1, -1), q_norm_w[l].reshape(1, -1), kv_norm_w[l].reshape(1, -1),
                 ret_gn_w[l].reshape(1, -1), norm_ffn_w[l].reshape(1, -1))
        final = l == depth - 1
        hp, (a, b_, c) = _layer(hp, tabs_p, sp // TOKEN_TILE, lw, norms, state0_p, None, fw, final)
        hs, (d_, e, f) = _layer(hs, tabs_s, 1, lw, norms, state_ret[l].astype(F32),
                                (cache_ckv[l], cache_kpe[l]), fw, final)
        for lst, val in zip(outs, (a, b_, c, d_, e, f)):
            lst.append(val)
    return (hp, hs) + tuple(o[0][None] if depth == 1 else jnp.stack(o) for o in outs)
```

```python
import functools
import math

import jax
import jax.numpy as jnp
import numpy as np
from jax import lax
from jax.experimental import pallas as pl
from jax.experimental.pallas import tpu as pltpu

F32 = jnp.float32
BF16 = jnp.bfloat16

CHUNK = 64
RET_HEADS = 4
RET_DK = 128
RET_DV = 256
MLA_HEADS = 8
NOPE_DIM = 128
ROPE_DIM = 64
V_DIM = 128
ROPE_BASE = 10000.0
EPS = 1e-6

LANES = 128
QK_PAD = 2 * LANES
V7X_VMEM_BYTES = 64 * 2**20
VMEM_LIMIT = V7X_VMEM_BYTES * 7 // 8
TOKEN_TILE = 512
WIDE_TOKEN_TILE = 1024
WEIGHT_STREAM_BLOCK = 128
RET_BLOCK = 256
ATTN_Q_TILE = 2048
ATTN_KV_TILE = 512
ATTN_COL_BLOCK = 256
PV_LAG = 1
FF_CHUNK = 256
NEG_BIG = -1e30
SOFTMAX_C = (NOPE_DIM + ROPE_DIM) ** -0.5 * math.log2(math.e)

_dot = functools.partial(jnp.dot, preferred_element_type=F32)


def _dot_nt(a, b):
    return lax.dot_general(a, b, (((1,), (1,)), ((), ())), preferred_element_type=F32)


def _dot_tn(a, b):
    return lax.dot_general(a, b, (((0,), (0,)), ((), ())), preferred_element_type=F32)


def _rms(x, g):
    ms = jnp.mean(x * x, axis=-1, keepdims=True)
    return x * lax.rsqrt(ms + EPS) * g


def _sigmoid(x):
    return 1.0 / (1.0 + jnp.exp(-x))


def _silu(x):
    return x * _sigmoid(x)


def _params(*sem):
    return pltpu.CompilerParams(dimension_semantics=sem, vmem_limit_bytes=VMEM_LIMIT)


def _const_spec(shape):
    return pl.BlockSpec(shape, lambda *_: (0,) * len(shape), pipeline_mode=pl.Buffered(1))


def _row_spec(tm, width):
    return pl.BlockSpec((tm, width), lambda i: (i, 0))


def _rope_full(x, cos2, sin2):
    return x * cos2 + pltpu.roll(x, LANES // 2, 1) * sin2


def _rope_half(x, c, sa, sb):
    q = ROPE_DIM // 2
    return x * c + pltpu.roll(x, LANES - q, 1) * sa + pltpu.roll(x, q, 1) * sb


def _ret_decay(blk, h):
    lg = math.log(1.0 - 2.0 ** (-5.0 - h))
    ri = lax.broadcasted_iota(jnp.int32, (blk, blk), 0)
    ci = lax.broadcasted_iota(jnp.int32, (blk, blk), 1)
    diff = (ri - ci).astype(F32)
    n = lax.broadcasted_iota(jnp.int32, (blk, RET_DK), 0).astype(F32)
    decay = jnp.where(diff >= 0, jnp.exp(lg * jnp.maximum(diff, 0.0)), 0.0)
    return decay, jnp.exp(lg * (n + 1.0)), jnp.exp(lg * (blk - 1.0 - n)), math.exp(lg * blk)


def _ret_block(q, k, v, state_ref, h, dec, gnw):
    decay, q_dec, k_dec, s_dec = dec
    v = v.astype(BF16)
    scores = _dot_nt(q.astype(BF16), k.astype(BF16)) * decay
    st = state_ref[h]
    y = _dot(scores.astype(BF16), v) + _dot((q * q_dec).astype(BF16), st.astype(BF16))
    state_ref[h] = s_dec * st + _dot_tn((k * k_dec).astype(BF16), v)
    yc = y - jnp.mean(y, axis=-1, keepdims=True)
    return yc * lax.rsqrt(jnp.mean(yc * yc, axis=-1, keepdims=True) + EPS) * gnw


def _inproj_kernel(x_ref, nw_ref, wa_ref, qnw_ref, kvnw_ref, wuq_ref, cr_ref, sr_ref, cm_ref, sm_ref, *rest,
                   prompt, seq_tiles):
    if prompt:
        (wuk_ref, wuvt_ref, s0_ref, gnw_ref, ret_ref, sfin_ref, q_ref, ckv_ref, kpe_ref, kn_ref, kp_ref, vt_ref,
         state_scr, dec_scr, qdec_scr, kdec_scr) = rest
    else:
        rq_ref, rk_ref, rv_ref, q_ref, ckv_ref, kpe_ref = rest
    qk_w = RET_HEADS * RET_DK
    v_w = RET_HEADS * RET_DV
    q_lora = wuq_ref.shape[0]
    kv_lora = ckv_ref.shape[1]
    o_rk, o_rv, o_cq = qk_w, 2 * qk_w, 2 * qk_w + v_w
    o_ckv = o_cq + q_lora
    o_kpe = o_ckv + kv_lora

    if prompt:
        i = pl.program_id(0)
        blk = dec_scr.shape[1]

        @pl.when(i == 0)
        def _():
            for h in range(RET_HEADS):
                dec_scr[h], qdec_scr[h], kdec_scr[h], _ = _ret_decay(blk, h)

        @pl.when(i % seq_tiles == 0)
        def _():
            state_scr[...] = s0_ref[...]

    xn = _rms(x_ref[...], nw_ref[...]).astype(BF16)
    c, s = cr_ref[...], sr_ref[...]
    cr = jnp.concatenate([c, c], axis=1)
    sr = jnp.concatenate([-s, s], axis=1)
    c, s = cm_ref[...], sm_ref[...]
    z = jnp.zeros_like(s)
    cm = jnp.concatenate([c, c, c, c], axis=1)
    sa = jnp.concatenate([-s, z, -s, z], axis=1)
    sb = jnp.concatenate([z, s, z, s], axis=1)

    zq = _dot_nt(xn, wa_ref[0:qk_w, :])
    zk = _dot_nt(xn, wa_ref[o_rk:o_rk + qk_w, :])
    rv = _dot_nt(xn, wa_ref[o_rv:o_rv + v_w, :])
    k_scale = RET_DK ** -0.5
    rq = [_rope_full(zq[:, h * RET_DK:(h + 1) * RET_DK], cr, sr) for h in range(RET_HEADS)]
    rk = [_rope_full(zk[:, h * RET_DK:(h + 1) * RET_DK], cr, sr) * k_scale for h in range(RET_HEADS)]
    if prompt:
        for b in range(x_ref.shape[0] // blk):
            r = slice(b * blk, (b + 1) * blk)
            for h in range(RET_HEADS):
                sl = slice(h * RET_DV, (h + 1) * RET_DV)
                dec = (dec_scr[h], qdec_scr[h], kdec_scr[h], math.exp(math.log(1.0 - 2.0 ** (-5.0 - h)) * blk))
                ret_ref[r, sl] = _ret_block(rq[h][r], rk[h][r], rv[r, sl], state_scr, h, dec,
                                            gnw_ref[:, sl]).astype(BF16)
    else:
        for h in range(RET_HEADS):
            sl = slice(h * RET_DK, (h + 1) * RET_DK)
            rq_ref[:, sl] = rq[h].astype(BF16)
            rk_ref[:, sl] = rk[h].astype(BF16)
        rv_ref[...] = rv.astype(BF16)

    cq = _dot_nt(xn, wa_ref[o_cq:o_cq + q_lora, :])
    cqn = _rms(cq, qnw_ref[...]).astype(BF16)
    for h in range(MLA_HEADS):
        qh = _dot(cqn, wuq_ref[:, h * QK_PAD:(h + 1) * QK_PAD]) * SOFTMAX_C
        q_ref[:, h * QK_PAD:h * QK_PAD + LANES] = qh[:, :LANES].astype(BF16)
        q_ref[:, h * QK_PAD + LANES:(h + 1) * QK_PAD] = _rope_half(qh[:, LANES:], cm, sa, sb).astype(BF16)

    ckv = _rms(_dot_nt(xn, wa_ref[o_ckv:o_ckv + kv_lora, :]), kvnw_ref[...])
    ckv_ref[...] = ckv
    kpe_pad = _rope_half(_dot_nt(xn, wa_ref[o_kpe:o_kpe + LANES, :]), cm, sa, sb)
    if prompt:
        kpe_ref[...] = kpe_pad.T[:ROPE_DIM, :]
    else:
        kpe_ref[...] = kpe_pad[:, :ROPE_DIM]
    if prompt:
        c16 = ckv.astype(BF16)
        kn_ref[...] = _dot(c16, wuk_ref[...]).astype(BF16)
        kp_ref[...] = kpe_pad.astype(BF16)
        v_t = _dot_nt(wuvt_ref[...], c16)
        for h in range(MLA_HEADS):
            vt_ref[h] = v_t[h * V_DIM:(h + 1) * V_DIM, :].astype(BF16)

        @pl.when(i % seq_tiles == seq_tiles - 1)
        def _():
            sfin_ref[...] = state_scr[...]


def _inproj(x2, tabs, nw, wa, qnw, kvnw, wuq, prompt_args, *, tab_tiles):
    n, d = x2.shape
    tm = min(TOKEN_TILE, n)
    qk_w, v_w = RET_HEADS * RET_DK, RET_HEADS * RET_DV
    kv_lora = kvnw.shape[1]
    prompt = prompt_args is not None
    tab_specs = [pl.BlockSpec((tm, t.shape[1]), lambda i: (i % tab_tiles, 0)) for t in tabs]
    in_specs = [_row_spec(tm, d), _const_spec(nw.shape), _const_spec(wa.shape), _const_spec(qnw.shape),
                _const_spec(kvnw.shape), _const_spec(wuq.shape)] + tab_specs
    mla_outs = [(MLA_HEADS * QK_PAD, BF16), (kv_lora, F32), (ROPE_DIM, F32)]
    scratch = []
    if prompt:
        wuk, wuvt, state0, gnw = prompt_args
        assert tm % RET_BLOCK == 0 and tm == ATTN_KV_TILE
        st_spec = pl.BlockSpec((None,) + state0.shape[1:], lambda i: (i // tab_tiles, 0, 0, 0))
        in_specs += [_const_spec(wuk.shape), _const_spec(wuvt.shape), st_spec, _const_spec(gnw.shape)]
        row_outs = [(v_w, BF16)] + mla_outs + [(MLA_HEADS * NOPE_DIM, BF16), (LANES, BF16)]
        out_specs = [_row_spec(tm, w) for w, _ in row_outs]
        out_shape = [jax.ShapeDtypeStruct((n, w), dt) for w, dt in row_outs]
        out_specs.insert(1, st_spec)
        out_shape.insert(1, jax.ShapeDtypeStruct(state0.shape, F32))
        seq = tab_tiles * tm
        out_specs[4] = pl.BlockSpec((None, ROPE_DIM, tm), lambda i: (i // tab_tiles, 0, i % tab_tiles))
        out_shape[4] = jax.ShapeDtypeStruct((n // seq, ROPE_DIM, seq), F32)
        out_specs.append(pl.BlockSpec((None, MLA_HEADS, None, V_DIM, tm),
                                      lambda i: (i // tab_tiles, 0, i % tab_tiles, 0, 0)))
        out_shape.append(jax.ShapeDtypeStruct((n // seq, MLA_HEADS, tab_tiles, V_DIM, tm), BF16))
        scratch = [pltpu.VMEM(state0.shape[1:], F32), pltpu.VMEM((RET_HEADS, RET_BLOCK, RET_BLOCK), F32),
                   pltpu.VMEM((RET_HEADS, RET_BLOCK, RET_DK), F32), pltpu.VMEM((RET_HEADS, RET_BLOCK, RET_DK), F32)]
    else:
        prompt_args = ()
        row_outs = [(qk_w, BF16), (qk_w, BF16), (v_w, BF16)] + mla_outs
        out_specs = [_row_spec(tm, w) for w, _ in row_outs]
        out_shape = [jax.ShapeDtypeStruct((n, w), dt) for w, dt in row_outs]
    return pl.pallas_call(
        functools.partial(_inproj_kernel, prompt=prompt, seq_tiles=tab_tiles),
        grid=(n // tm,),
        in_specs=in_specs,
        out_specs=out_specs,
        out_shape=out_shape,
        scratch_shapes=scratch,
        compiler_params=_params("arbitrary"),
        name="inproj",
    )(x2, nw, wa, qnw, kvnw, wuq, *tabs, *prompt_args)


def _retention_kernel(q_ref, k_ref, v_ref, s0_ref, gnw_ref, y_ref, sfin_ref, state_ref, *, blk, nblk):
    c = pl.program_id(1)

    @pl.when(c == 0)
    def _():
        state_ref[...] = s0_ref[...]

    for h in range(RET_HEADS):
        sl = slice(h * RET_DV, (h + 1) * RET_DV)
        q = q_ref[:, h * RET_DK:(h + 1) * RET_DK].astype(F32)
        k = k_ref[:, h * RET_DK:(h + 1) * RET_DK].astype(F32)
        y_ref[:, sl] = _ret_block(q, k, v_ref[:, sl], state_ref, h, _ret_decay(blk, h), gnw_ref[:, sl]).astype(BF16)

    @pl.when(c == nblk - 1)
    def _():
        sfin_ref[...] = state_ref[...]


def _retention(rq, rk, rv, state0, gnw, blk):
    b, s, _ = rq.shape
    nblk = s // blk
    qk_w, v_w = RET_HEADS * RET_DK, RET_HEADS * RET_DV
    st_spec = pl.BlockSpec((None, RET_HEADS, RET_DK, RET_DV), lambda i, c: (i, 0, 0, 0))
    return pl.pallas_call(
        functools.partial(_retention_kernel, blk=blk, nblk=nblk),
        grid=(b, nblk),
        in_specs=[pl.BlockSpec((None, blk, qk_w), lambda i, c: (i, c, 0)),
                  pl.BlockSpec((None, blk, qk_w), lambda i, c: (i, c, 0)),
                  pl.BlockSpec((None, blk, v_w), lambda i, c: (i, c, 0)),
                  st_spec,
                  pl.BlockSpec((1, v_w), lambda i, c: (0, 0))],
        out_specs=[pl.BlockSpec((None, blk, v_w), lambda i, c: (i, c, 0)), st_spec],
        out_shape=[jax.ShapeDtypeStruct((b, s, v_w), BF16),
                   jax.ShapeDtypeStruct((b, RET_HEADS, RET_DK, RET_DV), F32)],
        scratch_shapes=[pltpu.VMEM((RET_HEADS, RET_DK, RET_DV), F32)],
        compiler_params=_params("parallel", "arbitrary"),
        name="retention",
    )(rq, rk, rv, state0, gnw)


def _attn_prompt_kernel(q_ref, kn_ref, kp_ref, vt_ref, o_ref, s_scr, mx_scr, m_scr, acc_scr, *, tq, tk):
    nq = q_ref.shape[0] // tq
    assert tq == 4 * tk
    cb = ATTN_COL_BLOCK
    ones = jnp.ones((acc_scr.shape[0] - V_DIM, tk), BF16)

    def rows(ref, t, n):
        return ref[pl.ds(pl.multiple_of(t * n, n), n), :]

    def col_max(s):
        return jnp.broadcast_to(jnp.max(s, axis=0, keepdims=True), (mx_scr.shape[1], s.shape[1]))

    def park_block(b, qi, t, c):
        cols = slice(c * cb, (c + 1) * cb)
        k = jnp.concatenate([rows(kn_ref, t, tk), rows(kp_ref, t, tk)], axis=1)
        s = _dot_nt(k, q_ref[pl.ds(pl.multiple_of(qi * tq + c * cb, cb), cb), :])
        s_scr[b, :, cols] = s
        mx_scr[b, :, cols] = col_max(s)

    def exp_block(held, b, t, c, col0, masked):
        cols = slice(c * cb, (c + 1) * cb)
        s = s_scr[b, :, cols]
        mx = mx_scr[b, :, cols]
        if masked and c * cb - col0 < tk:
            key = lax.broadcasted_iota(jnp.int32, s.shape, 0) // CHUNK
            qry = (lax.broadcasted_iota(jnp.int32, s.shape, 1) + (c * cb - col0)) // CHUNK
            s = jnp.where(key <= qry, s, NEG_BIG)
            mx = col_max(s)
        m_old = m_scr[:, cols]
        m_new = jnp.maximum(m_old, mx)
        m_scr[:, cols] = m_new
        held[b, c] = (jnp.exp2(m_old - m_new), jnp.exp2(s - m_new[:1, :]).astype(BF16))

    def pv_block(held, b, t, c):
        cols = slice(c * cb, (c + 1) * cb)
        alpha, p = held.pop((b, c))
        v_ext = jnp.concatenate([vt_ref[t], ones], axis=0)
        acc_scr[:, cols] = alpha[:1, :] * acc_scr[:, cols] + _dot(v_ext, p)

    def step(consumes, parks):
        held = {}
        blocks = [(b, t, c, col0, masked) for b, t, col0, masked in consumes for c in range(col0 // cb, tq // cb)]
        cons = []
        for n, (b, t, c, col0, masked) in enumerate(blocks):
            cons.append(functools.partial(exp_block, held, b, t, c, col0, masked))
            if n >= PV_LAG:
                cons.append(functools.partial(pv_block, held, *blocks[n - PV_LAG][:3]))
        for blk_ in blocks[max(len(blocks) - PV_LAG, 0):]:
            cons.append(functools.partial(pv_block, held, *blk_[:3]))
        prks = [functools.partial(park_block, b, qi, t, c)
                for b, qi, t, col0 in parks for c in range(col0 // cb, tq // cb)]
        i = j = 0
        while i < len(cons) or j < len(prks):
            if j < len(prks) and (i == len(cons) or j * len(cons) <= i * len(prks)):
                prks[j]()
                j += 1
            else:
                cons[i]()
                i += 1

    def restart():
        m_scr[...] = jnp.full_like(m_scr, NEG_BIG)
        acc_scr[...] = jnp.zeros_like(acc_scr)

    def diagonal(qi, second_parked):
        t = 4 * qi
        nxt = jnp.minimum(qi + 1, nq - 1)
        late = [(2, qi, t + 2, 2 * tk), (3, qi, t + 3, 3 * tk)]
        step([(0, t, 0, True)], late if second_parked else [(1, qi, t + 1, tk)] + late)
        step([(1, t + 1, tk, True)], [(0, nxt, 0, 0)])
        step([(2, t + 2, 2 * tk, True), (3, t + 3, 3 * tk, True)], [(1, nxt, 1, 0)])
        acc = acc_scr[...]
        out_t = acc[:V_DIM, :] / acc[V_DIM:V_DIM + 1, :]
        o_ref[pl.ds(pl.multiple_of(qi * tq, tq), tq), :] = out_t.T.astype(BF16)
        restart()

    def q_tile(qi, _):
        def trip(j, _):
            t = 4 * j
            step([(0, t, 0, False)], [(2, qi, t + 2, 0)])
            step([(1, t + 1, 0, False)], [(3, qi, t + 3, 0)])
            step([(2, t + 2, 0, False)], [(0, qi, t + 4, 0)])
            step([(3, t + 3, 0, False)], [(1, qi, t + 5, 0)])
            return 0

        lax.fori_loop(0, qi, trip, 0)
        diagonal(qi, True)
        return 0

    restart()
    step([], [(0, 0, 0, 0)])
    diagonal(0, False)
    lax.fori_loop(1, nq, q_tile, 0)


def _attn_prompt(q, kn, kp, vt):
    b, s, _ = q.shape
    tq, tk = ATTN_Q_TILE, ATTN_KV_TILE
    assert tq % (2 * tk) == 0 and s % tq == 0 and vt.shape[2:] == (s // tk, V_DIM, tk)
    stat_rows = 8
    acc_rows = V_DIM + 16
    return pl.pallas_call(
        functools.partial(_attn_prompt_kernel, tq=tq, tk=tk),
        grid=(b, MLA_HEADS),
        in_specs=[pl.BlockSpec((None, s, QK_PAD), lambda i, h: (i, 0, h)),
                  pl.BlockSpec((None, s, NOPE_DIM), lambda i, h: (i, 0, h)),
                  pl.BlockSpec((None, s, LANES), lambda i, h: (i, 0, 0)),
                  pl.BlockSpec((None, None) + vt.shape[2:], lambda i, h: (i, h, 0, 0, 0))],
        out_specs=pl.BlockSpec((None, s, V_DIM), lambda i, h: (i, 0, h)),
        out_shape=jax.ShapeDtypeStruct((b, s, MLA_HEADS * V_DIM), BF16),
        scratch_shapes=[pltpu.VMEM((4, tk, tq), F32), pltpu.VMEM((4, stat_rows, tq), F32),
                        pltpu.VMEM((stat_rows, tq), F32), pltpu.VMEM((acc_rows, tq), F32)],
        compiler_params=_params("parallel", "parallel"),
        name="attn_prompt",
    )(q, kn, kp, vt)


def _attn_sample_kernel(q_ref, cckv_ref, ckpe_t_ref, nckv_ref, nkpe_ref, wuk_ref, wuv_ref, o_ref):
    n = q_ref.shape[0]
    q_lat, q_rope = [], []
    for h in range(MLA_HEADS):
        q_nope = q_ref[:, h * QK_PAD:h * QK_PAD + NOPE_DIM]
        q_lat.append(_dot_nt(q_nope, wuk_ref[:, h * NOPE_DIM:(h + 1) * NOPE_DIM]).astype(BF16))
        q_rope.append(q_ref[:, h * QK_PAD + NOPE_DIM:h * QK_PAD + NOPE_DIM + ROPE_DIM])
    q_lat = jnp.concatenate(q_lat, axis=0)
    q_rope = jnp.concatenate(q_rope, axis=0)
    vc = cckv_ref[...].astype(BF16)
    vn = nckv_ref[...].astype(BF16)
    sc = _dot_nt(q_lat, vc) + _dot(q_rope, ckpe_t_ref[...].astype(BF16))
    sn = _dot_nt(q_lat, vn) + _dot_nt(q_rope, nkpe_ref[...].astype(BF16))
    m = jnp.maximum(jnp.max(sc, axis=-1, keepdims=True), jnp.max(sn, axis=-1, keepdims=True))
    pc = jnp.exp2(sc - m)
    pn = jnp.exp2(sn - m)
    l = jnp.sum(pc, axis=-1, keepdims=True) + jnp.sum(pn, axis=-1, keepdims=True)
    o_lat = ((_dot(pc.astype(BF16), vc) + _dot(pn.astype(BF16), vn)) / l).astype(BF16)
    for h in range(MLA_HEADS):
        sl = slice(h * V_DIM, (h + 1) * V_DIM)
        o_ref[:, sl] = _dot(o_lat[h * n:(h + 1) * n], wuv_ref[:, sl]).astype(BF16)


def _attn_sample(q, cache_ckv, cache_kpe_t, new_ckv, new_kpe, wuk, wuv):
    b, s, _ = q.shape

    def per_stream(a):
        return pl.BlockSpec((None,) + a.shape[1:], lambda i: (i, 0, 0))

    return pl.pallas_call(
        _attn_sample_kernel,
        grid=(b,),
        in_specs=[per_stream(q), per_stream(cache_ckv), per_stream(cache_kpe_t), per_stream(new_ckv),
                  per_stream(new_kpe), _const_spec(wuk.shape), _const_spec(wuv.shape)],
        out_specs=pl.BlockSpec((None, s, MLA_HEADS * V_DIM), lambda i: (i, 0, 0)),
        out_shape=jax.ShapeDtypeStruct((b, s, MLA_HEADS * V_DIM), BF16),
        compiler_params=_params("parallel"),
        name="attn_sample",
    )(q, cache_ckv, cache_kpe_t, new_ckv, new_kpe, wuk, wuv)


def _mix_kernel(x_ref, ret_ref, mla_ref, nw_ref, wg_ref, wro_ref, wmo_ref, wo_ref, h_ref):
    d = x_ref.shape[1]
    v_w = ret_ref.shape[1]
    x = x_ref[...]
    xn = _rms(x, nw_ref[...]).astype(BF16)
    rg = _dot_nt(xn, wg_ref[0:v_w, :])
    ret_b = _dot((ret_ref[...].astype(F32) * _silu(rg)).astype(BF16), wro_ref[...])
    ga = _dot_nt(xn, wg_ref[v_w:v_w + d, :])
    merged = _sigmoid(ga) * ret_b
    mla_b = _dot(mla_ref[...], wmo_ref[...])
    gb = _dot_nt(xn, wg_ref[v_w + d:v_w + 2 * d, :])
    merged = merged + _sigmoid(gb) * mla_b
    h_ref[...] = x + _dot(merged.astype(BF16), wo_ref[...])


def _mix(x2, ret2, mla2, nw, wg, wro, wmo, wo):
    n, d = x2.shape
    tm = min(WIDE_TOKEN_TILE, n)
    return pl.pallas_call(
        _mix_kernel,
        grid=(n // tm,),
        in_specs=[_row_spec(tm, d), _row_spec(tm, ret2.shape[1]), _row_spec(tm, mla2.shape[1]),
                  _const_spec(nw.shape), _const_spec(wg.shape), _const_spec(wro.shape),
                  _const_spec(wmo.shape), _const_spec(wo.shape)],
        out_specs=_row_spec(tm, d),
        out_shape=jax.ShapeDtypeStruct((n, d), F32),
        compiler_params=_params("parallel"),
        name="mix",
    )(x2, ret2, mla2, nw, wg, wro, wmo, wo)


def _ffn_kernel(h_ref, nw_ref, wgu_ref, wd_ref, fw_ref, y_ref, *, final_norm):
    h = h_ref[...]
    hn = _rms(h, nw_ref[...]).astype(BF16)
    acc = h
    d_ff = wd_ref.shape[0]
    for c in range(d_ff // FF_CHUNK):
        sl = slice(c * FF_CHUNK, (c + 1) * FF_CHUNK)
        g = _dot(hn, wgu_ref[:, sl])
        u = _dot(hn, wgu_ref[:, d_ff + c * FF_CHUNK:d_ff + (c + 1) * FF_CHUNK])
        acc = acc + _dot((_silu(g) * u).astype(BF16), wd_ref[sl, :])
    y_ref[...] = _rms(acc, fw_ref[...]) if final_norm else acc


def _ffn(h2, nw, wgu, wd, fw, final_norm):
    n, d = h2.shape
    tm = min(WIDE_TOKEN_TILE, n)
    return pl.pallas_call(
        functools.partial(_ffn_kernel, final_norm=final_norm),
        grid=(n // tm,),
        in_specs=[_row_spec(tm, d), _const_spec(nw.shape), _const_spec(wgu.shape), _const_spec(wd.shape),
                  _const_spec(fw.shape)],
        out_specs=_row_spec(tm, d),
        out_shape=jax.ShapeDtypeStruct((n, d), F32),
        compiler_params=_params("parallel"),
        name="ffn",
    )(h2, nw, wgu, wd, fw)


def _rope_tables(start, n, reps):
    pos = np.arange(start, start + n, dtype=np.float64)[:, None]

    def cs(d):
        inv = ROPE_BASE ** (-np.arange(0, d, 2, dtype=np.float64) / d)
        ang = pos * inv[None, :]
        return [np.tile(f(ang).astype(np.float32), (reps, 1)) for f in (np.cos, np.sin)]

    return tuple(jnp.asarray(t) for t in cs(RET_DK) + cs(ROPE_DIM))


def _regroup_kernel(w_ref, *out_refs, plans, axis):
    for o_ref, plan in zip(out_refs, plans):
        off = 0
        for start, width in plan:
            dst = (slice(off, off + width), slice(None))[::1 if axis == 0 else -1]
            if start is None:
                o_ref[dst] = jnp.zeros(o_ref[dst].shape, o_ref.dtype)
            else:
                src = (slice(start, start + width), slice(None))[::1 if axis == 0 else -1]
                o_ref[dst] = w_ref[src].astype(o_ref.dtype)
            off += width


def _regroup_bf16(w, plans, axis):
    other = w.shape[1 - axis]
    blk = min(other, WEIGHT_STREAM_BLOCK)
    widths = [sum(width for _, width in plan) for plan in plans]

    def spec(width):
        return (pl.BlockSpec((width, blk), lambda i: (0, i)) if axis == 0 else
                pl.BlockSpec((blk, width), lambda i: (i, 0)))

    def shape(width):
        return (width, other) if axis == 0 else (other, width)

    return pl.pallas_call(
        functools.partial(_regroup_kernel, plans=plans, axis=axis),
        grid=(other // blk,),
        in_specs=[spec(w.shape[axis])],
        out_specs=[spec(wd) for wd in widths],
        out_shape=[jax.ShapeDtypeStruct(shape(wd), BF16) for wd in widths],
        compiler_params=_params("parallel"),
        name="regroup",
    )(w)


def _prep_layer_weights(w_in, w_uq, w_ukv, w_ret_o, w_mla_o, w_out, w_gate_up, w_down):
    qk_w, v_w = RET_HEADS * RET_DK, RET_HEADS * RET_DV
    q_lora = w_uq.shape[0]
    kv_lora = w_ukv.shape[0]
    o_rg = 2 * qk_w + v_w
    o_cq = o_rg + v_w
    o_kpe = o_cq + q_lora + kv_lora
    o_ga = o_kpe + ROPE_DIM
    wa, wg = _regroup_bf16(jnp.swapaxes(w_in, 0, 1),
                           [[(0, o_rg), (o_cq, o_ga - o_cq), (None, LANES - ROPE_DIM)],
                            [(o_rg, v_w), (o_ga, w_in.shape[1] - o_ga)]], axis=0)
    head_w = NOPE_DIM + ROPE_DIM
    wuq, = _regroup_bf16(w_uq, [[piece for h in range(MLA_HEADS)
                                 for piece in ((h * head_w, head_w), (None, QK_PAD - head_w))]], axis=1)
    kv_w = NOPE_DIM + V_DIM
    wuk, wuv = _regroup_bf16(w_ukv, [[(h * kv_w, NOPE_DIM) for h in range(MLA_HEADS)],
                                     [(h * kv_w + NOPE_DIM, V_DIM) for h in range(MLA_HEADS)]], axis=1)
    assert w_down.shape[0] % FF_CHUNK == 0
    return dict(wa=wa, wg=wg, wuq=wuq, wuk=wuk, wuv=wuv, wro=w_ret_o.astype(BF16), wmo=w_mla_o.astype(BF16),
                wo=w_out.astype(BF16), wgu=w_gate_up.astype(BF16), wd=w_down.astype(BF16))


def _layer(x, tabs, tab_tiles, lw, norms, state0, cache, fw, final_norm):
    b, s, d = x.shape
    n = b * s
    x2 = x.reshape(n, d)
    nmw, qnw, kvnw, gnw, nfw = norms
    wukv = (lw["wuk"], lw["wuv"])
    if cache is None:
        ret, s_fin, q, ckv, kpe, kn, kp, vt = _inproj(x2, tabs, nmw, lw["wa"], qnw, kvnw, lw["wuq"],
                                                      (lw["wuk"], lw["wuv"].T, state0, gnw), tab_tiles=tab_tiles)
    else:
        rq, rk, rv, q, ckv, kpe = _inproj(x2, tabs, nmw, lw["wa"], qnw, kvnw, lw["wuq"], None, tab_tiles=tab_tiles)
        ret, s_fin = _retention(rq.reshape(b, s, -1), rk.reshape(b, s, -1), rv.reshape(b, s, -1), state0, gnw, s)
    ckv3 = ckv.reshape(b, s, -1)
    kpe3 = jnp.swapaxes(kpe, 1, 2) if cache is None else kpe.reshape(b, s, ROPE_DIM)
    q3 = q.reshape(b, s, MLA_HEADS * QK_PAD)
    if cache is None:
        mla = _attn_prompt(q3, kn.reshape(b, s, -1), kp.reshape(b, s, -1), vt)
    else:
        mla = _attn_sample(q3, cache[0].astype(F32), jnp.swapaxes(cache[1].astype(F32), 1, 2), ckv3, kpe3, *wukv)
    h2 = _mix(x2, ret.reshape(n, -1), mla.reshape(n, -1), nmw, lw["wg"], lw["wro"], lw["wmo"], lw["wo"])
    y2 = _ffn(h2, nfw, lw["wgu"], lw["wd"], fw, final_norm)
    return y2.reshape(b, s, d), (ckv3, kpe3, s_fin)


def kernel(x_prompt, x_sample, cache_ckv, cache_kpe, state_ret, norm_mix_w, w_in, q_norm_w, w_uq, kv_norm_w, w_ukv, ret_gn_w, w_ret_o, w_mla_o, w_out, norm_ffn_w, w_gate_up, w_down, norm_final_w):
    depth = w_in.shape[0]
    bp, sp, _ = x_prompt.shape
    bs, ss, _ = x_sample.shape
    past = cache_ckv.shape[2]
    assert sp % TOKEN_TILE == 0 and sp % ATTN_Q_TILE == 0 and (bs * ss) % min(TOKEN_TILE, bs * ss) == 0
    assert ss == CHUNK, "the sample group is one streaming chunk"

    tabs_p = _rope_tables(0, sp, 1)
    tabs_s = _rope_tables(past, ss, bs)
    fw = norm_final_w.reshape(1, -1)
    state0_p = jnp.zeros((bp, RET_HEADS, RET_DK, RET_DV), F32)

    hp, hs = x_prompt, x_sample
    outs = [[] for _ in range(6)]
    for l in range(depth):
        lw = _prep_layer_weights(w_in[l], w_uq[l], w_ukv[l], w_ret_o[l], w_mla_o[l], w_out[l], w_gate_up[l],
                                 w_down[l])
        norms = (norm_mix_w[l].reshape(1, -1), q_norm_w[l].reshape(1, -1), kv_norm_w[l].reshape(1, -1),
                 ret_gn_w[l].reshape(1, -1), norm_ffn_w[l].reshape(1, -1))
        final = l == depth - 1
        hp, (a, b_, c) = _layer(hp, tabs_p, sp // TOKEN_TILE, lw, norms, state0_p, None, fw, final)
        hs, (d_, e, f) = _layer(hs, tabs_s, 1, lw, norms, state_ret[l].astype(F32),
                                (cache_ckv[l], cache_kpe[l]), fw, final)
        for lst, val in zip(outs, (a, b_, c, d_, e, f)):
            lst.append(val)
    return (hp, hs) + tuple(o[0][None] if depth == 1 else jnp.stack(o) for o in outs)
```

```python
import functools
import math

import jax
import jax.numpy as jnp
import numpy as np
from jax import lax
from jax.experimental import pallas as pl
from jax.experimental.pallas import tpu as pltpu

F32 = jnp.float32
BF16 = jnp.bfloat16

CHUNK = 64
RET_HEADS = 4
RET_DK = 128
RET_DV = 256
MLA_HEADS = 8
NOPE_DIM = 128
ROPE_DIM = 64
V_DIM = 128
ROPE_BASE = 10000.0
EPS = 1e-6

LANES = 128
QK_PAD = 2 * LANES
V7X_VMEM_BYTES = 64 * 2**20
VMEM_LIMIT = V7X_VMEM_BYTES * 7 // 8
TOKEN_TILE = 512
WIDE_TOKEN_TILE = 1024
WEIGHT_STREAM_BLOCK = 128
RET_BLOCK = 256
ATTN_Q_TILE = 2048
ATTN_KV_TILE = 512
ATTN_COL_BLOCK = 256
PV_LAG = 1
FF_CHUNK = 256
NEG_BIG = -1e30
SOFTMAX_C = (NOPE_DIM + ROPE_DIM) ** -0.5 * math.log2(math.e)

_dot = functools.partial(jnp.dot, preferred_element_type=F32)


def _dot_nt(a, b):
    return lax.dot_general(a, b, (((1,), (1,)), ((), ())), preferred_element_type=F32)


def _dot_tn(a, b):
    return lax.dot_general(a, b, (((0,), (0,)), ((), ())), preferred_element_type=F32)


def _rms(x, g):
    ms = jnp.mean(x * x, axis=-1, keepdims=True)
    return x * lax.rsqrt(ms + EPS) * g


def _sigmoid(x):
    return 1.0 / (1.0 + jnp.exp(-x))


def _silu(x):
    return x * _sigmoid(x)


def _params(*sem):
    return pltpu.CompilerParams(dimension_semantics=sem, vmem_limit_bytes=VMEM_LIMIT)


def _const_spec(shape):
    return pl.BlockSpec(shape, lambda *_: (0,) * len(shape), pipeline_mode=pl.Buffered(1))


def _row_spec(tm, width):
    return pl.BlockSpec((tm, width), lambda i: (i, 0))


def _rope_full(x, cos2, sin2):
    return x * cos2 + pltpu.roll(x, LANES // 2, 1) * sin2


def _rope_half(x, c, sa, sb):
    q = ROPE_DIM // 2
    return x * c + pltpu.roll(x, LANES - q, 1) * sa + pltpu.roll(x, q, 1) * sb


def _ret_decay(blk, h):
    lg = math.log(1.0 - 2.0 ** (-5.0 - h))
    ri = lax.broadcasted_iota(jnp.int32, (blk, blk), 0)
    ci = lax.broadcasted_iota(jnp.int32, (blk, blk), 1)
    diff = (ri - ci).astype(F32)
    n = lax.broadcasted_iota(jnp.int32, (blk, RET_DK), 0).astype(F32)
    decay = jnp.where(diff >= 0, jnp.exp(lg * jnp.maximum(diff, 0.0)), 0.0)
    return decay, jnp.exp(lg * (n + 1.0)), jnp.exp(lg * (blk - 1.0 - n)), math.exp(lg * blk)


def _ret_block(q, k, v, state_ref, h, dec, gnw):
    decay, q_dec, k_dec, s_dec = dec
    v = v.astype(BF16)
    scores = _dot_nt(q.astype(BF16), k.astype(BF16)) * decay
    st = state_ref[h]
    y = _dot(scores.astype(BF16), v) + _dot((q * q_dec).astype(BF16), st.astype(BF16))
    state_ref[h] = s_dec * st + _dot_tn((k * k_dec).astype(BF16), v)
    yc = y - jnp.mean(y, axis=-1, keepdims=True)
    return yc * lax.rsqrt(jnp.mean(yc * yc, axis=-1, keepdims=True) + EPS) * gnw


def _inproj_kernel(x_ref, nw_ref, wa_ref, qnw_ref, kvnw_ref, wuq_ref, cr_ref, sr_ref, cm_ref, sm_ref, *rest,
                   prompt, seq_tiles):
    if prompt:
        (wuk_ref, wuvt_ref, s0_ref, gnw_ref, ret_ref, sfin_ref, q_ref, ckv_ref, kpe_ref, kn_ref, kp_ref, vt_ref,
         state_scr, dec_scr, qdec_scr, kdec_scr) = rest
    else:
        rq_ref, rk_ref, rv_ref, q_ref, ckv_ref, kpe_ref = rest
    qk_w = RET_HEADS * RET_DK
    v_w = RET_HEADS * RET_DV
    q_lora = wuq_ref.shape[0]
    kv_lora = ckv_ref.shape[1]
    o_rk, o_rv, o_cq = qk_w, 2 * qk_w, 2 * qk_w + v_w
    o_ckv = o_cq + q_lora
    o_kpe = o_ckv + kv_lora

    if prompt:
        i = pl.program_id(0)
        blk = dec_scr.shape[1]

        @pl.when(i == 0)
        def _():
            for h in range(RET_HEADS):
                dec_scr[h], qdec_scr[h], kdec_scr[h], _ = _ret_decay(blk, h)

        @pl.when(i % seq_tiles == 0)
        def _():
            state_scr[...] = s0_ref[...]

    xn = _rms(x_ref[...], nw_ref[...]).astype(BF16)
    c, s = cr_ref[...], sr_ref[...]
    cr = jnp.concatenate([c, c], axis=1)
    sr = jnp.concatenate([-s, s], axis=1)
    c, s = cm_ref[...], sm_ref[...]
    z = jnp.zeros_like(s)
    cm = jnp.concatenate([c, c, c, c], axis=1)
    sa = jnp.concatenate([-s, z, -s, z], axis=1)
    sb = jnp.concatenate([z, s, z, s], axis=1)

    zq = _dot_nt(xn, wa_ref[0:qk_w, :])
    zk = _dot_nt(xn, wa_ref[o_rk:o_rk + qk_w, :])
    rv = _dot_nt(xn, wa_ref[o_rv:o_rv + v_w, :])
    k_scale = RET_DK ** -0.5
    rq = [_rope_full(zq[:, h * RET_DK:(h + 1) * RET_DK], cr, sr) for h in range(RET_HEADS)]
    rk = [_rope_full(zk[:, h * RET_DK:(h + 1) * RET_DK], cr, sr) * k_scale for h in range(RET_HEADS)]
    if prompt:
        for b in range(x_ref.shape[0] // blk):
            r = slice(b * blk, (b + 1) * blk)
            for h in range(RET_HEADS):
                sl = slice(h * RET_DV, (h + 1) * RET_DV)
                dec = (dec_scr[h], qdec_scr[h], kdec_scr[h], math.exp(math.log(1.0 - 2.0 ** (-5.0 - h)) * blk))
                ret_ref[r, sl] = _ret_block(rq[h][r], rk[h][r], rv[r, sl], state_scr, h, dec,
                                            gnw_ref[:, sl]).astype(BF16)
    else:
        for h in range(RET_HEADS):
            sl = slice(h * RET_DK, (h + 1) * RET_DK)
            rq_ref[:, sl] = rq[h].astype(BF16)
            rk_ref[:, sl] = rk[h].astype(BF16)
        rv_ref[...] = rv.astype(BF16)

    cq = _dot_nt(xn, wa_ref[o_cq:o_cq + q_lora, :])
    cqn = _rms(cq, qnw_ref[...]).astype(BF16)
    for h in range(MLA_HEADS):
        qh = _dot(cqn, wuq_ref[:, h * QK_PAD:(h + 1) * QK_PAD]) * SOFTMAX_C
        q_ref[:, h * QK_PAD:h * QK_PAD + LANES] = qh[:, :LANES].astype(BF16)
        q_ref[:, h * QK_PAD + LANES:(h + 1) * QK_PAD] = _rope_half(qh[:, LANES:], cm, sa, sb).astype(BF16)

    ckv = _rms(_dot_nt(xn, wa_ref[o_ckv:o_ckv + kv_lora, :]), kvnw_ref[...])
    ckv_ref[...] = ckv
    kpe_pad = _rope_half(_dot_nt(xn, wa_ref[o_kpe:o_kpe + LANES, :]), cm, sa, sb)
    if prompt:
        kpe_ref[...] = kpe_pad.T[:ROPE_DIM, :]
    else:
        kpe_ref[...] = kpe_pad[:, :ROPE_DIM]
    if prompt:
        c16 = ckv.astype(BF16)
        kn_ref[...] = _dot(c16, wuk_ref[...]).astype(BF16)
        kp_ref[...] = kpe_pad.astype(BF16)
        v_t = _dot_nt(wuvt_ref[...], c16)
        for h in range(MLA_HEADS):
            vt_ref[h] = v_t[h * V_DIM:(h + 1) * V_DIM, :].astype(BF16)

        @pl.when(i % seq_tiles == seq_tiles - 1)
        def _():
            sfin_ref[...] = state_scr[...]


def _inproj(x2, tabs, nw, wa, qnw, kvnw, wuq, prompt_args, *, tab_tiles):
    n, d = x2.shape
    tm = min(TOKEN_TILE, n)
    qk_w, v_w = RET_HEADS * RET_DK, RET_HEADS * RET_DV
    kv_lora = kvnw.shape[1]
    prompt = prompt_args is not None
    tab_specs = [pl.BlockSpec((tm, t.shape[1]), lambda i: (i % tab_tiles, 0)) for t in tabs]
    in_specs = [_row_spec(tm, d), _const_spec(nw.shape), _const_spec(wa.shape), _const_spec(qnw.shape),
                _const_spec(kvnw.shape), _const_spec(wuq.shape)] + tab_specs
    mla_outs = [(MLA_HEADS * QK_PAD, BF16), (kv_lora, F32), (ROPE_DIM, F32)]
    scratch = []
    if prompt:
        wuk, wuvt, state0, gnw = prompt_args
        assert tm % RET_BLOCK == 0 and tm == ATTN_KV_TILE
        st_spec = pl.BlockSpec((None,) + state0.shape[1:], lambda i: (i // tab_tiles, 0, 0, 0))
        in_specs += [_const_spec(wuk.shape), _const_spec(wuvt.shape), st_spec, _const_spec(gnw.shape)]
        row_outs = [(v_w, BF16)] + mla_outs + [(MLA_HEADS * NOPE_DIM, BF16), (LANES, BF16)]
        out_specs = [_row_spec(tm, w) for w, _ in row_outs]
        out_shape = [jax.ShapeDtypeStruct((n, w), dt) for w, dt in row_outs]
        out_specs.insert(1, st_spec)
        out_shape.insert(1, jax.ShapeDtypeStruct(state0.shape, F32))
        seq = tab_tiles * tm
        out_specs[4] = pl.BlockSpec((None, ROPE_DIM, tm), lambda i: (i // tab_tiles, 0, i % tab_tiles))
        out_shape[4] = jax.ShapeDtypeStruct((n // seq, ROPE_DIM, seq), F32)
        out_specs.append(pl.BlockSpec((None, MLA_HEADS, None, V_DIM, tm),
                                      lambda i: (i // tab_tiles, 0, i % tab_tiles, 0, 0)))
        out_shape.append(jax.ShapeDtypeStruct((n // seq, MLA_HEADS, tab_tiles, V_DIM, tm), BF16))
        scratch = [pltpu.VMEM(state0.shape[1:], F32), pltpu.VMEM((RET_HEADS, RET_BLOCK, RET_BLOCK), F32),
                   pltpu.VMEM((RET_HEADS, RET_BLOCK, RET_DK), F32), pltpu.VMEM((RET_HEADS, RET_BLOCK, RET_DK), F32)]
    else:
        prompt_args = ()
        row_outs = [(qk_w, BF16), (qk_w, BF16), (v_w, BF16)] + mla_outs
        out_specs = [_row_spec(tm, w) for w, _ in row_outs]
        out_shape = [jax.ShapeDtypeStruct((n, w), dt) for w, dt in row_outs]
    return pl.pallas_call(
        functools.partial(_inproj_kernel, prompt=prompt, seq_tiles=tab_tiles),
        grid=(n // tm,),
        in_specs=in_specs,
        out_specs=out_specs,
        out_shape=out_shape,
        scratch_shapes=scratch,
        compiler_params=_params("arbitrary"),
        name="inproj",
    )(x2, nw, wa, qnw, kvnw, wuq, *tabs, *prompt_args)


def _retention_kernel(q_ref, k_ref, v_ref, s0_ref, gnw_ref, y_ref, sfin_ref, state_ref, *, blk, nblk):
    c = pl.program_id(1)

    @pl.when(c == 0)
    def _():
        state_ref[...] = s0_ref[...]

    for h in range(RET_HEADS):
        sl = slice(h * RET_DV, (h + 1) * RET_DV)
        q = q_ref[:, h * RET_DK:(h + 1) * RET_DK].astype(F32)
        k = k_ref[:, h * RET_DK:(h + 1) * RET_DK].astype(F32)
        y_ref[:, sl] = _ret_block(q, k, v_ref[:, sl], state_ref, h, _ret_decay(blk, h), gnw_ref[:, sl]).astype(BF16)

    @pl.when(c == nblk - 1)
    def _():
        sfin_ref[...] = state_ref[...]


def _retention(rq, rk, rv, state0, gnw, blk):
    b, s, _ = rq.shape
    nblk = s // blk
    qk_w, v_w = RET_HEADS * RET_DK, RET_HEADS * RET_DV
    st_spec = pl.BlockSpec((None, RET_HEADS, RET_DK, RET_DV), lambda i, c: (i, 0, 0, 0))
    return pl.pallas_call(
        functools.partial(_retention_kernel, blk=blk, nblk=nblk),
        grid=(b, nblk),
        in_specs=[pl.BlockSpec((None, blk, qk_w), lambda i, c: (i, c, 0)),
                  pl.BlockSpec((None, blk, qk_w), lambda i, c: (i, c, 0)),
                  pl.BlockSpec((None, blk, v_w), lambda i, c: (i, c, 0)),
                  st_spec,
                  pl.BlockSpec((1, v_w), lambda i, c: (0, 0))],
        out_specs=[pl.BlockSpec((None, blk, v_w), lambda i, c: (i, c, 0)), st_spec],
        out_shape=[jax.ShapeDtypeStruct((b, s, v_w), BF16),
                   jax.ShapeDtypeStruct((b, RET_HEADS, RET_DK, RET_DV), F32)],
        scratch_shapes=[pltpu.VMEM((RET_HEADS, RET_DK, RET_DV), F32)],
        compiler_params=_params("parallel", "arbitrary"),
        name="retention",
    )(rq, rk, rv, state0, gnw)


def _attn_prompt_kernel(q_ref, kn_ref, kp_ref, vt_ref, o_ref, s_scr, mx_scr, m_scr, acc_scr, *, tq, tk):
    nq = q_ref.shape[0] // tq
    assert tq == 4 * tk
    cb = ATTN_COL_BLOCK
    ones = jnp.ones((acc_scr.shape[1] - V_DIM, tk), BF16)

    def rows(ref, t, n):
        return ref[pl.ds(pl.multiple_of(t * n, n), n), :]

    def col_max(s):
        return jnp.broadcast_to(jnp.max(s, axis=0, keepdims=True), (mx_scr.shape[2], s.shape[1]))

    def park_block(b, qi, t, c):
        k = jnp.concatenate([rows(kn_ref, t, tk), rows(kp_ref, t, tk)], axis=1)
        s = _dot_nt(k, q_ref[pl.ds(pl.multiple_of(qi * tq + c * cb, cb), cb), :])
        s_scr[b, c] = s
        mx_scr[b, c] = col_max(s)

    def exp_block(held, b, t, c, col0, masked):
        s = s_scr[b, c]
        mx = mx_scr[b, c]
        if masked and c * cb - col0 < tk:
            key = lax.broadcasted_iota(jnp.int32, s.shape, 0) // CHUNK
            qry = (lax.broadcasted_iota(jnp.int32, s.shape, 1) + (c * cb - col0)) // CHUNK
            s = jnp.where(key <= qry, s, NEG_BIG)
            mx = col_max(s)
        m_old = m_scr[c]
        m_new = jnp.maximum(m_old, mx)
        m_scr[c] = m_new
        held[b, c] = (jnp.exp2(m_old - m_new), jnp.exp2(s - m_new[:1, :]).astype(BF16))

    def pv_block(held, b, t, c):
        alpha, p = held.pop((b, c))
        v_ext = jnp.concatenate([vt_ref[t], ones], axis=0)
        acc_scr[c] = alpha[:1, :] * acc_scr[c] + _dot(v_ext, p)

    def step(consumes, parks):
        held = {}
        blocks = [(b, t, c, col0, masked) for b, t, col0, masked in consumes for c in range(col0 // cb, tq // cb)]
        cons = []
        for n, (b, t, c, col0, masked) in enumerate(blocks):
            cons.append(functools.partial(exp_block, held, b, t, c, col0, masked))
            if n >= PV_LAG:
                cons.append(functools.partial(pv_block, held, *blocks[n - PV_LAG][:3]))
        for blk_ in blocks[max(len(blocks) - PV_LAG, 0):]:
            cons.append(functools.partial(pv_block, held, *blk_[:3]))
        prks = [functools.partial(park_block, b, qi, t, c)
                for b, qi, t, col0 in parks for c in range(col0 // cb, tq // cb)]
        i = j = 0
        while i < len(cons) or j < len(prks):
            if j < len(prks) and (i == len(cons) or j * len(cons) <= i * len(prks)):
                prks[j]()
                j += 1
            else:
                cons[i]()
                i += 1

    def restart():
        m_scr[...] = jnp.full_like(m_scr, NEG_BIG)
        acc_scr[...] = jnp.zeros_like(acc_scr)

    def diagonal(qi, second_parked):
        t = 4 * qi
        nxt = jnp.minimum(qi + 1, nq - 1)
        late = [(2, qi, t + 2, 2 * tk), (3, qi, t + 3, 3 * tk)]
        step([(0, t, 0, True)], late if second_parked else [(1, qi, t + 1, tk)] + late)
        step([(1, t + 1, tk, True)], [(0, nxt, 0, 0)])
        step([(2, t + 2, 2 * tk, True), (3, t + 3, 3 * tk, True)], [(1, nxt, 1, 0)])
        for c in range(tq // cb):
            acc = acc_scr[c]
            out_t = acc[:V_DIM, :] / acc[V_DIM:V_DIM + 1, :]
            o_ref[pl.ds(pl.multiple_of(qi * tq + c * cb, cb), cb), :] = out_t.T.astype(BF16)
        restart()

    def q_tile(qi, _):
        def trip(j, _):
            t = 4 * j
            step([(0, t, 0, False)], [(2, qi, t + 2, 0)])
            step([(1, t + 1, 0, False)], [(3, qi, t + 3, 0)])
            step([(2, t + 2, 0, False)], [(0, qi, t + 4, 0)])
            step([(3, t + 3, 0, False)], [(1, qi, t + 5, 0)])
            return 0

        lax.fori_loop(0, qi, trip, 0)
        diagonal(qi, True)
        return 0

    restart()
    step([], [(0, 0, 0, 0)])
    diagonal(0, False)
    lax.fori_loop(1, nq, q_tile, 0)


def _attn_prompt(q, kn, kp, vt):
    b, s, _ = q.shape
    tq, tk = ATTN_Q_TILE, ATTN_KV_TILE
    assert tq % (2 * tk) == 0 and s % tq == 0 and vt.shape[2:] == (s // tk, V_DIM, tk)
    cb = ATTN_COL_BLOCK
    nblk = tq // cb
    stat_rows = 8
    acc_rows = V_DIM + 16
    return pl.pallas_call(
        functools.partial(_attn_prompt_kernel, tq=tq, tk=tk),
        grid=(b, MLA_HEADS),
        in_specs=[pl.BlockSpec((None, s, QK_PAD), lambda i, h: (i, 0, h)),
                  pl.BlockSpec((None, s, NOPE_DIM), lambda i, h: (i, 0, h)),
                  pl.BlockSpec((None, s, LANES), lambda i, h: (i, 0, 0)),
                  pl.BlockSpec((None, None) + vt.shape[2:], lambda i, h: (i, h, 0, 0, 0))],
        out_specs=pl.BlockSpec((None, s, V_DIM), lambda i, h: (i, 0, h)),
        out_shape=jax.ShapeDtypeStruct((b, s, MLA_HEADS * V_DIM), BF16),
        scratch_shapes=[pltpu.VMEM((4, nblk, tk, cb), F32), pltpu.VMEM((4, nblk, stat_rows, cb), F32),
                        pltpu.VMEM((nblk, stat_rows, cb), F32), pltpu.VMEM((nblk, acc_rows, cb), F32)],
        compiler_params=_params("parallel", "parallel"),
        name="attn_prompt",
    )(q, kn, kp, vt)


def _attn_sample_kernel(q_ref, cckv_ref, ckpe_t_ref, nckv_ref, nkpe_ref, wuk_ref, wuv_ref, o_ref):
    n = q_ref.shape[0]
    q_lat, q_rope = [], []
    for h in range(MLA_HEADS):
        q_nope = q_ref[:, h * QK_PAD:h * QK_PAD + NOPE_DIM]
        q_lat.append(_dot_nt(q_nope, wuk_ref[:, h * NOPE_DIM:(h + 1) * NOPE_DIM]).astype(BF16))
        q_rope.append(q_ref[:, h * QK_PAD + NOPE_DIM:h * QK_PAD + NOPE_DIM + ROPE_DIM])
    q_lat = jnp.concatenate(q_lat, axis=0)
    q_rope = jnp.concatenate(q_rope, axis=0)
    vc = cckv_ref[...].astype(BF16)
    vn = nckv_ref[...].astype(BF16)
    sc = _dot_nt(q_lat, vc) + _dot(q_rope, ckpe_t_ref[...].astype(BF16))
    sn = _dot_nt(q_lat, vn) + _dot_nt(q_rope, nkpe_ref[...].astype(BF16))
    m = jnp.maximum(jnp.max(sc, axis=-1, keepdims=True), jnp.max(sn, axis=-1, keepdims=True))
    pc = jnp.exp2(sc - m)
    pn = jnp.exp2(sn - m)
    l = jnp.sum(pc, axis=-1, keepdims=True) + jnp.sum(pn, axis=-1, keepdims=True)
    o_lat = ((_dot(pc.astype(BF16), vc) + _dot(pn.astype(BF16), vn)) / l).astype(BF16)
    for h in range(MLA_HEADS):
        sl = slice(h * V_DIM, (h + 1) * V_DIM)
        o_ref[:, sl] = _dot(o_lat[h * n:(h + 1) * n], wuv_ref[:, sl]).astype(BF16)


def _attn_sample(q, cache_ckv, cache_kpe_t, new_ckv, new_kpe, wuk, wuv):
    b, s, _ = q.shape

    def per_stream(a):
        return pl.BlockSpec((None,) + a.shape[1:], lambda i: (i, 0, 0))

    return pl.pallas_call(
        _attn_sample_kernel,
        grid=(b,),
        in_specs=[per_stream(q), per_stream(cache_ckv), per_stream(cache_kpe_t), per_stream(new_ckv),
                  per_stream(new_kpe), _const_spec(wuk.shape), _const_spec(wuv.shape)],
        out_specs=pl.BlockSpec((None, s, MLA_HEADS * V_DIM), lambda i: (i, 0, 0)),
        out_shape=jax.ShapeDtypeStruct((b, s, MLA_HEADS * V_DIM), BF16),
        compiler_params=_params("parallel"),
        name="attn_sample",
    )(q, cache_ckv, cache_kpe_t, new_ckv, new_kpe, wuk, wuv)


def _mix_kernel(x_ref, ret_ref, mla_ref, nw_ref, wg_ref, wro_ref, wmo_ref, wo_ref, h_ref):
    d = x_ref.shape[1]
    v_w = ret_ref.shape[1]
    x = x_ref[...]
    xn = _rms(x, nw_ref[...]).astype(BF16)
    rg = _dot_nt(xn, wg_ref[0:v_w, :])
    ret_b = _dot((ret_ref[...].astype(F32) * _silu(rg)).astype(BF16), wro_ref[...])
    ga = _dot_nt(xn, wg_ref[v_w:v_w + d, :])
    merged = _sigmoid(ga) * ret_b
    mla_b = _dot(mla_ref[...], wmo_ref[...])
    gb = _dot_nt(xn, wg_ref[v_w + d:v_w + 2 * d, :])
    merged = merged + _sigmoid(gb) * mla_b
    h_ref[...] = x + _dot(merged.astype(BF16), wo_ref[...])


def _mix(x2, ret2, mla2, nw, wg, wro, wmo, wo):
    n, d = x2.shape
    tm = min(WIDE_TOKEN_TILE, n)
    return pl.pallas_call(
        _mix_kernel,
        grid=(n // tm,),
        in_specs=[_row_spec(tm, d), _row_spec(tm, ret2.shape[1]), _row_spec(tm, mla2.shape[1]),
                  _const_spec(nw.shape), _const_spec(wg.shape), _const_spec(wro.shape),
                  _const_spec(wmo.shape), _const_spec(wo.shape)],
        out_specs=_row_spec(tm, d),
        out_shape=jax.ShapeDtypeStruct((n, d), F32),
        compiler_params=_params("parallel"),
        name="mix",
    )(x2, ret2, mla2, nw, wg, wro, wmo, wo)


def _ffn_kernel(h_ref, nw_ref, wgu_ref, wd_ref, fw_ref, y_ref, *, final_norm):
    h = h_ref[...]
    hn = _rms(h, nw_ref[...]).astype(BF16)
    acc = h
    d_ff = wd_ref.shape[0]
    for c in range(d_ff // FF_CHUNK):
        sl = slice(c * FF_CHUNK, (c + 1) * FF_CHUNK)
        g = _dot(hn, wgu_ref[:, sl])
        u = _dot(hn, wgu_ref[:, d_ff + c * FF_CHUNK:d_ff + (c + 1) * FF_CHUNK])
        acc = acc + _dot((_silu(g) * u).astype(BF16), wd_ref[sl, :])
    y_ref[...] = _rms(acc, fw_ref[...]) if final_norm else acc


def _ffn(h2, nw, wgu, wd, fw, final_norm):
    n, d = h2.shape
    tm = min(WIDE_TOKEN_TILE, n)
    return pl.pallas_call(
        functools.partial(_ffn_kernel, final_norm=final_norm),
        grid=(n // tm,),
        in_specs=[_row_spec(tm, d), _const_spec(nw.shape), _const_spec(wgu.shape), _const_spec(wd.shape),
                  _const_spec(fw.shape)],
        out_specs=_row_spec(tm, d),
        out_shape=jax.ShapeDtypeStruct((n, d), F32),
        compiler_params=_params("parallel"),
        name="ffn",
    )(h2, nw, wgu, wd, fw)


def _rope_tables(start, n, reps):
    pos = np.arange(start, start + n, dtype=np.float64)[:, None]

    def cs(d):
        inv = ROPE_BASE ** (-np.arange(0, d, 2, dtype=np.float64) / d)
        ang = pos * inv[None, :]
        return [np.tile(f(ang).astype(np.float32), (reps, 1)) for f in (np.cos, np.sin)]

    return tuple(jnp.asarray(t) for t in cs(RET_DK) + cs(ROPE_DIM))


def _regroup_kernel(w_ref, *out_refs, plans, axis):
    for o_ref, plan in zip(out_refs, plans):
        off = 0
        for start, width in plan:
            dst = (slice(off, off + width), slice(None))[::1 if axis == 0 else -1]
            if start is None:
                o_ref[dst] = jnp.zeros(o_ref[dst].shape, o_ref.dtype)
            else:
                src = (slice(start, start + width), slice(None))[::1 if axis == 0 else -1]
                o_ref[dst] = w_ref[src].astype(o_ref.dtype)
            off += width


def _regroup_bf16(w, plans, axis):
    other = w.shape[1 - axis]
    blk = min(other, WEIGHT_STREAM_BLOCK)
    widths = [sum(width for _, width in plan) for plan in plans]

    def spec(width):
        return (pl.BlockSpec((width, blk), lambda i: (0, i)) if axis == 0 else
                pl.BlockSpec((blk, width), lambda i: (i, 0)))

    def shape(width):
        return (width, other) if axis == 0 else (other, width)

    return pl.pallas_call(
        functools.partial(_regroup_kernel, plans=plans, axis=axis),
        grid=(other // blk,),
        in_specs=[spec(w.shape[axis])],
        out_specs=[spec(wd) for wd in widths],
        out_shape=[jax.ShapeDtypeStruct(shape(wd), BF16) for wd in widths],
        compiler_params=_params("parallel"),
        name="regroup",
    )(w)


def _prep_layer_weights(w_in, w_uq, w_ukv, w_ret_o, w_mla_o, w_out, w_gate_up, w_down):
    qk_w, v_w = RET_HEADS * RET_DK, RET_HEADS * RET_DV
    q_lora = w_uq.shape[0]
    kv_lora = w_ukv.shape[0]
    o_rg = 2 * qk_w + v_w
    o_cq = o_rg + v_w
    o_kpe = o_cq + q_lora + kv_lora
    o_ga = o_kpe + ROPE_DIM
    wa, wg = _regroup_bf16(jnp.swapaxes(w_in, 0, 1),
                           [[(0, o_rg), (o_cq, o_ga - o_cq), (None, LANES - ROPE_DIM)],
                            [(o_rg, v_w), (o_ga, w_in.shape[1] - o_ga)]], axis=0)
    head_w = NOPE_DIM + ROPE_DIM
    wuq, = _regroup_bf16(w_uq, [[piece for h in range(MLA_HEADS)
                                 for piece in ((h * head_w, head_w), (None, QK_PAD - head_w))]], axis=1)
    kv_w = NOPE_DIM + V_DIM
    wuk, wuv = _regroup_bf16(w_ukv, [[(h * kv_w, NOPE_DIM) for h in range(MLA_HEADS)],
                                     [(h * kv_w + NOPE_DIM, V_DIM) for h in range(MLA_HEADS)]], axis=1)
    assert w_down.shape[0] % FF_CHUNK == 0
    return dict(wa=wa, wg=wg, wuq=wuq, wuk=wuk, wuv=wuv, wro=w_ret_o.astype(BF16), wmo=w_mla_o.astype(BF16),
                wo=w_out.astype(BF16), wgu=w_gate_up.astype(BF16), wd=w_down.astype(BF16))


def _layer(x, tabs, tab_tiles, lw, norms, state0, cache, fw, final_norm):
    b, s, d = x.shape
    n = b * s
    x2 = x.reshape(n, d)
    nmw, qnw, kvnw, gnw, nfw = norms
    wukv = (lw["wuk"], lw["wuv"])
    if cache is None:
        ret, s_fin, q, ckv, kpe, kn, kp, vt = _inproj(x2, tabs, nmw, lw["wa"], qnw, kvnw, lw["wuq"],
                                                      (lw["wuk"], lw["wuv"].T, state0, gnw), tab_tiles=tab_tiles)
    else:
        rq, rk, rv, q, ckv, kpe = _inproj(x2, tabs, nmw, lw["wa"], qnw, kvnw, lw["wuq"], None, tab_tiles=tab_tiles)
        ret, s_fin = _retention(rq.reshape(b, s, -1), rk.reshape(b, s, -1), rv.reshape(b, s, -1), state0, gnw, s)
    ckv3 = ckv.reshape(b, s, -1)
    kpe3 = jnp.swapaxes(kpe, 1, 2) if cache is None else kpe.reshape(b, s, ROPE_DIM)
    q3 = q.reshape(b, s, MLA_HEADS * QK_PAD)
    if cache is None:
        mla = _attn_prompt(q3, kn.reshape(b, s, -1), kp.reshape(b, s, -1), vt)
    else:
        mla = _attn_sample(q3, cache[0].astype(F32), jnp.swapaxes(cache[1].astype(F32), 1, 2), ckv3, kpe3, *wukv)
    h2 = _mix(x2, ret.reshape(n, -1), mla.reshape(n, -1), nmw, lw["wg"], lw["wro"], lw["wmo"], lw["wo"])
    y2 = _ffn(h2, nfw, lw["wgu"], lw["wd"], fw, final_norm)
    return y2.reshape(b, s, d), (ckv3, kpe3, s_fin)


def kernel(x_prompt, x_sample, cache_ckv, cache_kpe, state_ret, norm_mix_w, w_in, q_norm_w, w_uq, kv_norm_w, w_ukv, ret_gn_w, w_ret_o, w_mla_o, w_out, norm_ffn_w, w_gate_up, w_down, norm_final_w):
    depth = w_in.shape[0]
    bp, sp, _ = x_prompt.shape
    bs, ss, _ = x_sample.shape
    past = cache_ckv.shape[2]
    assert sp % TOKEN_TILE == 0 and sp % ATTN_Q_TILE == 0 and (bs * ss) % min(TOKEN_TILE, bs * ss) == 0
    assert ss == CHUNK, "the sample group is one streaming chunk"

    tabs_p = _rope_tables(0, sp, 1)
    tabs_s = _rope_tables(past, ss, bs)
    fw = norm_final_w.reshape(1, -1)
    state0_p = jnp.zeros((bp, RET_HEADS, RET_DK, RET_DV), F32)

    hp, hs = x_prompt, x_sample
    outs = [[] for _ in range(6)]
    for l in range(depth):
        lw = _prep_layer_weights(w_in[l], w_uq[l], w_ukv[l], w_ret_o[l], w_mla_o[l], w_out[l], w_gate_up[l],
                                 w_down[l])
        norms = (norm_mix_w[l].reshape(1, -1), q_norm_w[l].reshape(1, -1), kv_norm_w[l].reshape(1, -1),
                 ret_gn_w[l].reshape(1, -1), norm_ffn_w[l].reshape(1, -1))
        final = l == depth - 1
        hp, (a, b_, c) = _layer(hp, tabs_p, sp // TOKEN_TILE, lw, norms, state0_p, None, fw, final)
        hs, (d_, e, f) = _layer(hs, tabs_s, 1, lw, norms, state_ret[l].astype(F32),
                                (cache_ckv[l], cache_kpe[l]), fw, final)
        for lst, val in zip(outs, (a, b_, c, d_, e, f)):
            lst.append(val)
    return (hp, hs) + tuple(o[0][None] if depth == 1 else jnp.stack(o) for o in outs)
```

```python
import functools
import math

import jax
import jax.numpy as jnp
import numpy as np
from jax import lax
from jax.experimental import pallas as pl
from jax.experimental.pallas import tpu as pltpu

F32 = jnp.float32
BF16 = jnp.bfloat16

CHUNK = 64
RET_HEADS = 4
RET_DK = 128
RET_DV = 256
MLA_HEADS = 8
NOPE_DIM = 128
ROPE_DIM = 64
V_DIM = 128
ROPE_BASE = 10000.0
EPS = 1e-6

LANES = 128
QK_PAD = 2 * LANES
V7X_VMEM_BYTES = 64 * 2**20
VMEM_LIMIT = V7X_VMEM_BYTES * 7 // 8
TOKEN_TILE = 512
WIDE_TOKEN_TILE = 1024
WEIGHT_STREAM_BLOCK = 128
RET_BLOCK = 256
ATTN_Q_TILE = 2048
ATTN_KV_TILE = 512
ATTN_COL_BLOCK = 256
PV_LAG = 1
FF_CHUNK = 256
NEG_BIG = -1e30
SOFTMAX_C = (NOPE_DIM + ROPE_DIM) ** -0.5 * math.log2(math.e)

_dot = functools.partial(jnp.dot, preferred_element_type=F32)


def _dot_nt(a, b):
    return lax.dot_general(a, b, (((1,), (1,)), ((), ())), preferred_element_type=F32)


def _dot_tn(a, b):
    return lax.dot_general(a, b, (((0,), (0,)), ((), ())), preferred_element_type=F32)


def _rms(x, g):
    ms = jnp.mean(x * x, axis=-1, keepdims=True)
    return x * lax.rsqrt(ms + EPS) * g


def _sigmoid(x):
    return 1.0 / (1.0 + jnp.exp(-x))


def _silu(x):
    return x * _sigmoid(x)


def _params(*sem):
    return pltpu.CompilerParams(dimension_semantics=sem, vmem_limit_bytes=VMEM_LIMIT)


def _const_spec(shape):
    return pl.BlockSpec(shape, lambda *_: (0,) * len(shape), pipeline_mode=pl.Buffered(1))


def _row_spec(tm, width):
    return pl.BlockSpec((tm, width), lambda i: (i, 0))


def _rope_full(x, cos2, sin2):
    return x * cos2 + pltpu.roll(x, LANES // 2, 1) * sin2


def _rope_half(x, c, sa, sb):
    q = ROPE_DIM // 2
    return x * c + pltpu.roll(x, LANES - q, 1) * sa + pltpu.roll(x, q, 1) * sb


def _ret_decay(blk, h):
    lg = math.log(1.0 - 2.0 ** (-5.0 - h))
    ri = lax.broadcasted_iota(jnp.int32, (blk, blk), 0)
    ci = lax.broadcasted_iota(jnp.int32, (blk, blk), 1)
    diff = (ri - ci).astype(F32)
    n = lax.broadcasted_iota(jnp.int32, (blk, RET_DK), 0).astype(F32)
    decay = jnp.where(diff >= 0, jnp.exp(lg * jnp.maximum(diff, 0.0)), 0.0)
    return decay, jnp.exp(lg * (n + 1.0)), jnp.exp(lg * (blk - 1.0 - n)), math.exp(lg * blk)


def _ret_block(q, k, v, state_ref, h, dec, gnw):
    decay, q_dec, k_dec, s_dec = dec
    v = v.astype(BF16)
    scores = _dot_nt(q.astype(BF16), k.astype(BF16)) * decay
    st = state_ref[h]
    y = _dot(scores.astype(BF16), v) + _dot((q * q_dec).astype(BF16), st.astype(BF16))
    state_ref[h] = s_dec * st + _dot_tn((k * k_dec).astype(BF16), v)
    yc = y - jnp.mean(y, axis=-1, keepdims=True)
    return yc * lax.rsqrt(jnp.mean(yc * yc, axis=-1, keepdims=True) + EPS) * gnw


def _inproj_kernel(x_ref, nw_ref, wa_ref, qnw_ref, kvnw_ref, wuq_ref, cr_ref, sr_ref, cm_ref, sm_ref, *rest,
                   prompt, seq_tiles):
    if prompt:
        (wuk_ref, wuvt_ref, s0_ref, gnw_ref, ret_ref, sfin_ref, q_ref, ckv_ref, kpe_ref, kn_ref, kp_ref, vt_ref,
         state_scr, dec_scr, qdec_scr, kdec_scr) = rest
    else:
        rq_ref, rk_ref, rv_ref, q_ref, ckv_ref, kpe_ref = rest
    qk_w = RET_HEADS * RET_DK
    v_w = RET_HEADS * RET_DV
    q_lora = wuq_ref.shape[0]
    kv_lora = ckv_ref.shape[1]
    o_rk, o_rv, o_cq = qk_w, 2 * qk_w, 2 * qk_w + v_w
    o_ckv = o_cq + q_lora
    o_kpe = o_ckv + kv_lora

    if prompt:
        i = pl.program_id(0)
        blk = dec_scr.shape[1]

        @pl.when(i == 0)
        def _():
            for h in range(RET_HEADS):
                dec_scr[h], qdec_scr[h], kdec_scr[h], _ = _ret_decay(blk, h)

        @pl.when(i % seq_tiles == 0)
        def _():
            state_scr[...] = s0_ref[...]

    xn = _rms(x_ref[...], nw_ref[...]).astype(BF16)
    c, s = cr_ref[...], sr_ref[...]
    cr = jnp.concatenate([c, c], axis=1)
    sr = jnp.concatenate([-s, s], axis=1)
    c, s = cm_ref[...], sm_ref[...]
    z = jnp.zeros_like(s)
    cm = jnp.concatenate([c, c, c, c], axis=1)
    sa = jnp.concatenate([-s, z, -s, z], axis=1)
    sb = jnp.concatenate([z, s, z, s], axis=1)

    zq = _dot_nt(xn, wa_ref[0:qk_w, :])
    zk = _dot_nt(xn, wa_ref[o_rk:o_rk + qk_w, :])
    rv = _dot_nt(xn, wa_ref[o_rv:o_rv + v_w, :])
    k_scale = RET_DK ** -0.5
    rq = [_rope_full(zq[:, h * RET_DK:(h + 1) * RET_DK], cr, sr) for h in range(RET_HEADS)]
    rk = [_rope_full(zk[:, h * RET_DK:(h + 1) * RET_DK], cr, sr) * k_scale for h in range(RET_HEADS)]
    if prompt:
        for b in range(x_ref.shape[0] // blk):
            r = slice(b * blk, (b + 1) * blk)
            for h in range(RET_HEADS):
                sl = slice(h * RET_DV, (h + 1) * RET_DV)
                dec = (dec_scr[h], qdec_scr[h], kdec_scr[h], math.exp(math.log(1.0 - 2.0 ** (-5.0 - h)) * blk))
                ret_ref[r, sl] = _ret_block(rq[h][r], rk[h][r], rv[r, sl], state_scr, h, dec,
                                            gnw_ref[:, sl]).astype(BF16)
    else:
        for h in range(RET_HEADS):
            sl = slice(h * RET_DK, (h + 1) * RET_DK)
            rq_ref[:, sl] = rq[h].astype(BF16)
            rk_ref[:, sl] = rk[h].astype(BF16)
        rv_ref[...] = rv.astype(BF16)

    cq = _dot_nt(xn, wa_ref[o_cq:o_cq + q_lora, :])
    cqn = _rms(cq, qnw_ref[...]).astype(BF16)
    for h in range(MLA_HEADS):
        qh = _dot(cqn, wuq_ref[:, h * QK_PAD:(h + 1) * QK_PAD]) * SOFTMAX_C
        q_ref[:, h * QK_PAD:h * QK_PAD + LANES] = qh[:, :LANES].astype(BF16)
        q_ref[:, h * QK_PAD + LANES:(h + 1) * QK_PAD] = _rope_half(qh[:, LANES:], cm, sa, sb).astype(BF16)

    ckv = _rms(_dot_nt(xn, wa_ref[o_ckv:o_ckv + kv_lora, :]), kvnw_ref[...])
    ckv_ref[...] = ckv
    kpe_pad = _rope_half(_dot_nt(xn, wa_ref[o_kpe:o_kpe + LANES, :]), cm, sa, sb)
    if prompt:
        kpe_ref[...] = kpe_pad.T[:ROPE_DIM, :]
    else:
        kpe_ref[...] = kpe_pad[:, :ROPE_DIM]
    if prompt:
        c16 = ckv.astype(BF16)
        kn_ref[...] = _dot(c16, wuk_ref[...]).astype(BF16)
        kp_ref[...] = kpe_pad.astype(BF16)
        v_t = _dot_nt(wuvt_ref[...], c16)
        for h in range(MLA_HEADS):
            vt_ref[h] = v_t[h * V_DIM:(h + 1) * V_DIM, :].astype(BF16)

        @pl.when(i % seq_tiles == seq_tiles - 1)
        def _():
            sfin_ref[...] = state_scr[...]


def _inproj(x2, tabs, nw, wa, qnw, kvnw, wuq, prompt_args, *, tab_tiles):
    n, d = x2.shape
    tm = min(TOKEN_TILE, n)
    qk_w, v_w = RET_HEADS * RET_DK, RET_HEADS * RET_DV
    kv_lora = kvnw.shape[1]
    prompt = prompt_args is not None
    tab_specs = [pl.BlockSpec((tm, t.shape[1]), lambda i: (i % tab_tiles, 0)) for t in tabs]
    in_specs = [_row_spec(tm, d), _const_spec(nw.shape), _const_spec(wa.shape), _const_spec(qnw.shape),
                _const_spec(kvnw.shape), _const_spec(wuq.shape)] + tab_specs
    mla_outs = [(MLA_HEADS * QK_PAD, BF16), (kv_lora, F32), (ROPE_DIM, F32)]
    scratch = []
    if prompt:
        wuk, wuvt, state0, gnw = prompt_args
        assert tm % RET_BLOCK == 0 and tm == ATTN_KV_TILE
        st_spec = pl.BlockSpec((None,) + state0.shape[1:], lambda i: (i // tab_tiles, 0, 0, 0))
        in_specs += [_const_spec(wuk.shape), _const_spec(wuvt.shape), st_spec, _const_spec(gnw.shape)]
        row_outs = [(v_w, BF16)] + mla_outs + [(MLA_HEADS * NOPE_DIM, BF16), (LANES, BF16)]
        out_specs = [_row_spec(tm, w) for w, _ in row_outs]
        out_shape = [jax.ShapeDtypeStruct((n, w), dt) for w, dt in row_outs]
        out_specs.insert(1, st_spec)
        out_shape.insert(1, jax.ShapeDtypeStruct(state0.shape, F32))
        seq = tab_tiles * tm
        out_specs[4] = pl.BlockSpec((None, ROPE_DIM, tm), lambda i: (i // tab_tiles, 0, i % tab_tiles))
        out_shape[4] = jax.ShapeDtypeStruct((n // seq, ROPE_DIM, seq), F32)
        out_specs.append(pl.BlockSpec((None, MLA_HEADS, None, V_DIM, tm),
                                      lambda i: (i // tab_tiles, 0, i % tab_tiles, 0, 0)))
        out_shape.append(jax.ShapeDtypeStruct((n // seq, MLA_HEADS, tab_tiles, V_DIM, tm), BF16))
        scratch = [pltpu.VMEM(state0.shape[1:], F32), pltpu.VMEM((RET_HEADS, RET_BLOCK, RET_BLOCK), F32),
                   pltpu.VMEM((RET_HEADS, RET_BLOCK, RET_DK), F32), pltpu.VMEM((RET_HEADS, RET_BLOCK, RET_DK), F32)]
    else:
        prompt_args = ()
        row_outs = [(qk_w, BF16), (qk_w, BF16), (v_w, BF16)] + mla_outs
        out_specs = [_row_spec(tm, w) for w, _ in row_outs]
        out_shape = [jax.ShapeDtypeStruct((n, w), dt) for w, dt in row_outs]
    return pl.pallas_call(
        functools.partial(_inproj_kernel, prompt=prompt, seq_tiles=tab_tiles),
        grid=(n // tm,),
        in_specs=in_specs,
        out_specs=out_specs,
        out_shape=out_shape,
        scratch_shapes=scratch,
        compiler_params=_params("arbitrary"),
        name="inproj",
    )(x2, nw, wa, qnw, kvnw, wuq, *tabs, *prompt_args)


def _retention_kernel(q_ref, k_ref, v_ref, s0_ref, gnw_ref, y_ref, sfin_ref, state_ref, *, blk, nblk):
    c = pl.program_id(1)

    @pl.when(c == 0)
    def _():
        state_ref[...] = s0_ref[...]

    for h in range(RET_HEADS):
        sl = slice(h * RET_DV, (h + 1) * RET_DV)
        q = q_ref[:, h * RET_DK:(h + 1) * RET_DK].astype(F32)
        k = k_ref[:, h * RET_DK:(h + 1) * RET_DK].astype(F32)
        y_ref[:, sl] = _ret_block(q, k, v_ref[:, sl], state_ref, h, _ret_decay(blk, h), gnw_ref[:, sl]).astype(BF16)

    @pl.when(c == nblk - 1)
    def _():
        sfin_ref[...] = state_ref[...]


def _retention(rq, rk, rv, state0, gnw, blk):
    b, s, _ = rq.shape
    nblk = s // blk
    qk_w, v_w = RET_HEADS * RET_DK, RET_HEADS * RET_DV
    st_spec = pl.BlockSpec((None, RET_HEADS, RET_DK, RET_DV), lambda i, c: (i, 0, 0, 0))
    return pl.pallas_call(
        functools.partial(_retention_kernel, blk=blk, nblk=nblk),
        grid=(b, nblk),
        in_specs=[pl.BlockSpec((None, blk, qk_w), lambda i, c: (i, c, 0)),
                  pl.BlockSpec((None, blk, qk_w), lambda i, c: (i, c, 0)),
                  pl.BlockSpec((None, blk, v_w), lambda i, c: (i, c, 0)),
                  st_spec,
                  pl.BlockSpec((1, v_w), lambda i, c: (0, 0))],
        out_specs=[pl.BlockSpec((None, blk, v_w), lambda i, c: (i, c, 0)), st_spec],
        out_shape=[jax.ShapeDtypeStruct((b, s, v_w), BF16),
                   jax.ShapeDtypeStruct((b, RET_HEADS, RET_DK, RET_DV), F32)],
        scratch_shapes=[pltpu.VMEM((RET_HEADS, RET_DK, RET_DV), F32)],
        compiler_params=_params("parallel", "arbitrary"),
        name="retention",
    )(rq, rk, rv, state0, gnw)


def _attn_prompt_kernel(q_ref, kn_ref, kp_ref, vt_ref, o_ref, s_scr, mx_scr, m_scr, acc_scr, *, tq, tk):
    nq = q_ref.shape[0] // tq
    assert tq == 4 * tk
    cb = ATTN_COL_BLOCK
    ones = jnp.ones((acc_scr.shape[1] - V_DIM, tk), BF16)

    def rows(ref, t, n):
        return ref[pl.ds(pl.multiple_of(t * n, n), n), :]

    def col_max(s):
        return jnp.broadcast_to(jnp.max(s, axis=0, keepdims=True), (mx_scr.shape[2], s.shape[1]))

    def park_block(b, qi, t, c):
        k = jnp.concatenate([rows(kn_ref, t, tk), rows(kp_ref, t, tk)], axis=1)
        s = _dot_nt(k, q_ref[pl.ds(pl.multiple_of(qi * tq + c * cb, cb), cb), :])
        s_scr[b, c] = s
        mx_scr[b, c] = col_max(s)

    def exp_block(held, b, t, c, col0, masked):
        s = s_scr[b, c]
        mx = mx_scr[b, c]
        if masked and c * cb - col0 < tk:
            key = lax.broadcasted_iota(jnp.int32, s.shape, 0) // CHUNK
            qry = (lax.broadcasted_iota(jnp.int32, s.shape, 1) + (c * cb - col0)) // CHUNK
            s = jnp.where(key <= qry, s, NEG_BIG)
            mx = col_max(s)
        m_old = m_scr[c]
        m_new = jnp.maximum(m_old, mx)
        m_scr[c] = m_new
        held[b, c] = (jnp.exp2(m_old - m_new), jnp.exp2(s - m_new[:1, :]).astype(BF16))

    def pv_block(held, b, t, c):
        alpha, p = held.pop((b, c))
        v_ext = jnp.concatenate([vt_ref[t], ones], axis=0)
        acc_scr[c] = alpha[:1, :] * acc_scr[c] + _dot(v_ext, p)

    def step(consumes, parks):
        held = {}
        blocks = [(b, t, c, col0, masked) for b, t, col0, masked in consumes for c in range(col0 // cb, tq // cb)]
        cons = []
        for n, (b, t, c, col0, masked) in enumerate(blocks):
            cons.append(functools.partial(exp_block, held, b, t, c, col0, masked))
            if n >= PV_LAG:
                cons.append(functools.partial(pv_block, held, *blocks[n - PV_LAG][:3]))
        for blk_ in blocks[max(len(blocks) - PV_LAG, 0):]:
            cons.append(functools.partial(pv_block, held, *blk_[:3]))
        prks = [functools.partial(park_block, b, qi, t, c)
                for b, qi, t, col0 in parks for c in range(col0 // cb, tq // cb)]
        i = j = 0
        while i < len(cons) or j < len(prks):
            if j < len(prks) and (i == len(cons) or j * len(cons) <= i * len(prks)):
                prks[j]()
                j += 1
            else:
                cons[i]()
                i += 1

    def restart():
        m_scr[...] = jnp.full_like(m_scr, NEG_BIG)
        acc_scr[...] = jnp.zeros_like(acc_scr)

    def diagonal(qi, second_parked):
        t = 4 * qi
        nxt = jnp.minimum(qi + 1, nq - 1)
        late = [(2, qi, t + 2, 2 * tk), (3, qi, t + 3, 3 * tk)]
        step([(0, t, 0, True)], late if second_parked else [(1, qi, t + 1, tk)] + late)
        step([(1, t + 1, tk, True)], [(0, nxt, 0, 0)])
        step([(2, t + 2, 2 * tk, True), (3, t + 3, 3 * tk, True)], [(1, nxt, 1, 0)])
        for c in range(tq // cb):
            acc = acc_scr[c]
            out_t = acc[:V_DIM, :] / acc[V_DIM:V_DIM + 1, :]
            o_ref[pl.ds(pl.multiple_of(qi * tq + c * cb, cb), cb), :] = out_t.T.astype(BF16)
        restart()

    def q_tile(qi, _):
        def trip(j, _):
            t = 4 * j
            step([(0, t, 0, False)], [(2, qi, t + 2, 0)])
            step([(1, t + 1, 0, False)], [(3, qi, t + 3, 0)])
            step([(2, t + 2, 0, False)], [(0, qi, t + 4, 0)])
            step([(3, t + 3, 0, False)], [(1, qi, t + 5, 0)])
            return 0

        lax.fori_loop(0, qi, trip, 0)
        diagonal(qi, True)
        return 0

    restart()
    step([], [(0, 0, 0, 0)])
    diagonal(0, False)
    lax.fori_loop(1, nq, q_tile, 0)


def _attn_prompt(q, kn, kp, vt):
    b, s, _ = q.shape
    tq, tk = ATTN_Q_TILE, ATTN_KV_TILE
    assert tq % (2 * tk) == 0 and s % tq == 0 and vt.shape[2:] == (s // tk, V_DIM, tk)
    cb = ATTN_COL_BLOCK
    nblk = tq // cb
    stat_rows = 8
    acc_rows = V_DIM + 16
    return pl.pallas_call(
        functools.partial(_attn_prompt_kernel, tq=tq, tk=tk),
        grid=(b, MLA_HEADS),
        in_specs=[pl.BlockSpec((None, s, QK_PAD), lambda i, h: (i, 0, h)),
                  pl.BlockSpec((None, s, NOPE_DIM), lambda i, h: (i, 0, h)),
                  pl.BlockSpec((None, s, LANES), lambda i, h: (i, 0, 0)),
                  pl.BlockSpec((None, None) + vt.shape[2:], lambda i, h: (i, h, 0, 0, 0))],
        out_specs=pl.BlockSpec((None, s, V_DIM), lambda i, h: (i, 0, h)),
        out_shape=jax.ShapeDtypeStruct((b, s, MLA_HEADS * V_DIM), BF16),
        scratch_shapes=[pltpu.VMEM((4, nblk, tk, cb), F32), pltpu.VMEM((4, nblk, stat_rows, cb), F32),
                        pltpu.VMEM((nblk, stat_rows, cb), F32), pltpu.VMEM((nblk, acc_rows, cb), F32)],
        compiler_params=_params("parallel", "parallel"),
        name="attn_prompt",
    )(q, kn, kp, vt)


def _attn_sample_kernel(q_ref, cckv_ref, ckpe_t_ref, nckv_ref, nkpe_ref, wuk_ref, wuv_ref, o_ref):
    n = q_ref.shape[0]
    q_lat, q_rope = [], []
    for h in range(MLA_HEADS):
        q_nope = q_ref[:, h * QK_PAD:h * QK_PAD + NOPE_DIM]
        q_lat.append(_dot_nt(q_nope, wuk_ref[:, h * NOPE_DIM:(h + 1) * NOPE_DIM]).astype(BF16))
        q_rope.append(q_ref[:, h * QK_PAD + NOPE_DIM:h * QK_PAD + NOPE_DIM + ROPE_DIM])
    q_lat = jnp.concatenate(q_lat, axis=0)
    q_rope = jnp.concatenate(q_rope, axis=0)
    vc = cckv_ref[...].astype(BF16)
    vn = nckv_ref[...].astype(BF16)
    sc = _dot_nt(q_lat, vc) + _dot(q_rope, ckpe_t_ref[...].astype(BF16))
    sn = _dot_nt(q_lat, vn) + _dot_nt(q_rope, nkpe_ref[...].astype(BF16))
    m = jnp.maximum(jnp.max(sc, axis=-1, keepdims=True), jnp.max(sn, axis=-1, keepdims=True))
    pc = jnp.exp2(sc - m)
    pn = jnp.exp2(sn - m)
    l = jnp.sum(pc, axis=-1, keepdims=True) + jnp.sum(pn, axis=-1, keepdims=True)
    o_lat = ((_dot(pc.astype(BF16), vc) + _dot(pn.astype(BF16), vn)) / l).astype(BF16)
    for h in range(MLA_HEADS):
        sl = slice(h * V_DIM, (h + 1) * V_DIM)
        o_ref[:, sl] = _dot(o_lat[h * n:(h + 1) * n], wuv_ref[:, sl]).astype(BF16)


def _attn_sample(q, cache_ckv, cache_kpe_t, new_ckv, new_kpe, wuk, wuv):
    b, s, _ = q.shape

    def per_stream(a):
        return pl.BlockSpec((None,) + a.shape[1:], lambda i: (i, 0, 0))

    return pl.pallas_call(
        _attn_sample_kernel,
        grid=(b,),
        in_specs=[per_stream(q), per_stream(cache_ckv), per_stream(cache_kpe_t), per_stream(new_ckv),
                  per_stream(new_kpe), _const_spec(wuk.shape), _const_spec(wuv.shape)],
        out_specs=pl.BlockSpec((None, s, MLA_HEADS * V_DIM), lambda i: (i, 0, 0)),
        out_shape=jax.ShapeDtypeStruct((b, s, MLA_HEADS * V_DIM), BF16),
        compiler_params=_params("parallel"),
        name="attn_sample",
    )(q, cache_ckv, cache_kpe_t, new_ckv, new_kpe, wuk, wuv)


def _mix_kernel(x_ref, ret_ref, mla_ref, nw_ref, wg_ref, wro_ref, wmo_ref, wo_ref, h_ref):
    d = x_ref.shape[1]
    v_w = ret_ref.shape[1]
    x = x_ref[...]
    xn = _rms(x, nw_ref[...]).astype(BF16)
    rg = _dot_nt(xn, wg_ref[0:v_w, :])
    ret_b = _dot((ret_ref[...].astype(F32) * _silu(rg)).astype(BF16), wro_ref[...])
    ga = _dot_nt(xn, wg_ref[v_w:v_w + d, :])
    merged = _sigmoid(ga) * ret_b
    mla_b = _dot(mla_ref[...], wmo_ref[...])
    gb = _dot_nt(xn, wg_ref[v_w + d:v_w + 2 * d, :])
    merged = merged + _sigmoid(gb) * mla_b
    h_ref[...] = x + _dot(merged.astype(BF16), wo_ref[...])


def _two_group_call(body, name, rows_main, rows_tail, consts, out_width):
    n = rows_main[0].shape[0]
    tm = min(WIDE_TOKEN_TILE, n)
    steps = n // tm
    nt = rows_tail[0].shape[0]
    nr = len(rows_main)

    def kern(*refs):
        main, tail = refs[:nr], refs[nr:2 * nr]
        cs = refs[2 * nr:2 * nr + len(consts)]
        out_main, out_tail = refs[-2:]
        i = pl.program_id(0)

        @pl.when(i < steps)
        def _():
            body(*main, *cs, out_main)

        @pl.when(i == steps)
        def _():
            body(*tail, *cs, out_tail)

    def main_spec(width):
        return pl.BlockSpec((tm, width), lambda i: (jnp.minimum(i, steps - 1), 0))

    def tail_spec(width):
        return pl.BlockSpec((nt, width), lambda i: (0, 0))

    return pl.pallas_call(
        kern,
        grid=(steps + 1,),
        in_specs=[main_spec(a.shape[1]) for a in rows_main] + [tail_spec(a.shape[1]) for a in rows_tail]
        + [_const_spec(c.shape) for c in consts],
        out_specs=[main_spec(out_width), tail_spec(out_width)],
        out_shape=[jax.ShapeDtypeStruct((n, out_width), F32), jax.ShapeDtypeStruct((nt, out_width), F32)],
        compiler_params=_params("arbitrary"),
        name=name,
    )(*rows_main, *rows_tail, *consts)


def _mix(rows_main, rows_tail, nw, wg, wro, wmo, wo):
    return _two_group_call(_mix_kernel, "mix", rows_main, rows_tail, (nw, wg, wro, wmo, wo), rows_main[0].shape[1])


def _ffn_kernel(h_ref, nw_ref, wgu_ref, wd_ref, fw_ref, y_ref, *, final_norm):
    h = h_ref[...]
    hn = _rms(h, nw_ref[...]).astype(BF16)
    acc = h
    d_ff = wd_ref.shape[0]
    for c in range(d_ff // FF_CHUNK):
        sl = slice(c * FF_CHUNK, (c + 1) * FF_CHUNK)
        g = _dot(hn, wgu_ref[:, sl])
        u = _dot(hn, wgu_ref[:, d_ff + c * FF_CHUNK:d_ff + (c + 1) * FF_CHUNK])
        acc = acc + _dot((_silu(g) * u).astype(BF16), wd_ref[sl, :])
    y_ref[...] = _rms(acc, fw_ref[...]) if final_norm else acc


def _ffn(h_main, h_tail, nw, wgu, wd, fw, final_norm):
    return _two_group_call(functools.partial(_ffn_kernel, final_norm=final_norm), "ffn", (h_main,), (h_tail,),
                           (nw, wgu, wd, fw), h_main.shape[1])


def _rope_tables(start, n, reps):
    pos = np.arange(start, start + n, dtype=np.float64)[:, None]

    def cs(d):
        inv = ROPE_BASE ** (-np.arange(0, d, 2, dtype=np.float64) / d)
        ang = pos * inv[None, :]
        return [np.tile(f(ang).astype(np.float32), (reps, 1)) for f in (np.cos, np.sin)]

    return tuple(jnp.asarray(t) for t in cs(RET_DK) + cs(ROPE_DIM))


def _regroup_kernel(w_ref, *out_refs, plans, axis):
    for o_ref, plan in zip(out_refs, plans):
        off = 0
        for start, width in plan:
            dst = (slice(off, off + width), slice(None))[::1 if axis == 0 else -1]
            if start is None:
                o_ref[dst] = jnp.zeros(o_ref[dst].shape, o_ref.dtype)
            else:
                src = (slice(start, start + width), slice(None))[::1 if axis == 0 else -1]
                o_ref[dst] = w_ref[src].astype(o_ref.dtype)
            off += width


def _regroup_bf16(w, plans, axis):
    other = w.shape[1 - axis]
    blk = min(other, WEIGHT_STREAM_BLOCK)
    widths = [sum(width for _, width in plan) for plan in plans]

    def spec(width):
        return (pl.BlockSpec((width, blk), lambda i: (0, i)) if axis == 0 else
                pl.BlockSpec((blk, width), lambda i: (i, 0)))

    def shape(width):
        return (width, other) if axis == 0 else (other, width)

    return pl.pallas_call(
        functools.partial(_regroup_kernel, plans=plans, axis=axis),
        grid=(other // blk,),
        in_specs=[spec(w.shape[axis])],
        out_specs=[spec(wd) for wd in widths],
        out_shape=[jax.ShapeDtypeStruct(shape(wd), BF16) for wd in widths],
        compiler_params=_params("parallel"),
        name="regroup",
    )(w)


def _prep_layer_weights(w_in, w_uq, w_ukv, w_ret_o, w_mla_o, w_out, w_gate_up, w_down):
    qk_w, v_w = RET_HEADS * RET_DK, RET_HEADS * RET_DV
    q_lora = w_uq.shape[0]
    kv_lora = w_ukv.shape[0]
    o_rg = 2 * qk_w + v_w
    o_cq = o_rg + v_w
    o_kpe = o_cq + q_lora + kv_lora
    o_ga = o_kpe + ROPE_DIM
    wa, wg = _regroup_bf16(jnp.swapaxes(w_in, 0, 1),
                           [[(0, o_rg), (o_cq, o_ga - o_cq), (None, LANES - ROPE_DIM)],
                            [(o_rg, v_w), (o_ga, w_in.shape[1] - o_ga)]], axis=0)
    head_w = NOPE_DIM + ROPE_DIM
    wuq, = _regroup_bf16(w_uq, [[piece for h in range(MLA_HEADS)
                                 for piece in ((h * head_w, head_w), (None, QK_PAD - head_w))]], axis=1)
    kv_w = NOPE_DIM + V_DIM
    wuk, wuv = _regroup_bf16(w_ukv, [[(h * kv_w, NOPE_DIM) for h in range(MLA_HEADS)],
                                     [(h * kv_w + NOPE_DIM, V_DIM) for h in range(MLA_HEADS)]], axis=1)
    assert w_down.shape[0] % FF_CHUNK == 0
    return dict(wa=wa, wg=wg, wuq=wuq, wuk=wuk, wuv=wuv, wro=w_ret_o.astype(BF16), wmo=w_mla_o.astype(BF16),
                wo=w_out.astype(BF16), wgu=w_gate_up.astype(BF16), wd=w_down.astype(BF16))


def _mixers(x, tabs, tab_tiles, lw, norms, state0, cache):
    b, s, d = x.shape
    n = b * s
    x2 = x.reshape(n, d)
    nmw, qnw, kvnw, gnw, _ = norms
    wukv = (lw["wuk"], lw["wuv"])
    if cache is None:
        ret, s_fin, q, ckv, kpe, kn, kp, vt = _inproj(x2, tabs, nmw, lw["wa"], qnw, kvnw, lw["wuq"],
                                                      (lw["wuk"], lw["wuv"].T, state0, gnw), tab_tiles=tab_tiles)
    else:
        rq, rk, rv, q, ckv, kpe = _inproj(x2, tabs, nmw, lw["wa"], qnw, kvnw, lw["wuq"], None, tab_tiles=tab_tiles)
        ret, s_fin = _retention(rq.reshape(b, s, -1), rk.reshape(b, s, -1), rv.reshape(b, s, -1), state0, gnw, s)
    ckv3 = ckv.reshape(b, s, -1)
    kpe3 = jnp.swapaxes(kpe, 1, 2) if cache is None else kpe.reshape(b, s, ROPE_DIM)
    q3 = q.reshape(b, s, MLA_HEADS * QK_PAD)
    if cache is None:
        mla = _attn_prompt(q3, kn.reshape(b, s, -1), kp.reshape(b, s, -1), vt)
    else:
        mla = _attn_sample(q3, cache[0].astype(F32), jnp.swapaxes(cache[1].astype(F32), 1, 2), ckv3, kpe3, *wukv)
    return (x2, ret.reshape(n, -1), mla.reshape(n, -1)), (ckv3, kpe3, s_fin)


def kernel(x_prompt, x_sample, cache_ckv, cache_kpe, state_ret, norm_mix_w, w_in, q_norm_w, w_uq, kv_norm_w, w_ukv, ret_gn_w, w_ret_o, w_mla_o, w_out, norm_ffn_w, w_gate_up, w_down, norm_final_w):
    depth = w_in.shape[0]
    bp, sp, _ = x_prompt.shape
    bs, ss, _ = x_sample.shape
    past = cache_ckv.shape[2]
    assert sp % TOKEN_TILE == 0 and sp % ATTN_Q_TILE == 0 and (bs * ss) % min(TOKEN_TILE, bs * ss) == 0
    assert ss == CHUNK, "the sample group is one streaming chunk"

    tabs_p = _rope_tables(0, sp, 1)
    tabs_s = _rope_tables(past, ss, bs)
    fw = norm_final_w.reshape(1, -1)
    state0_p = jnp.zeros((bp, RET_HEADS, RET_DK, RET_DV), F32)

    hp, hs = x_prompt, x_sample
    outs = [[] for _ in range(6)]
    for l in range(depth):
        lw = _prep_layer_weights(w_in[l], w_uq[l], w_ukv[l], w_ret_o[l], w_mla_o[l], w_out[l], w_gate_up[l],
                                 w_down[l])
        norms = (norm_mix_w[l].reshape(1, -1), q_norm_w[l].reshape(1, -1), kv_norm_w[l].reshape(1, -1),
                 ret_gn_w[l].reshape(1, -1), norm_ffn_w[l].reshape(1, -1))
        rows_p, new_p = _mixers(hp, tabs_p, sp // TOKEN_TILE, lw, norms, state0_p, None)
        rows_s, new_s = _mixers(hs, tabs_s, 1, lw, norms, state_ret[l].astype(F32), (cache_ckv[l], cache_kpe[l]))
        h2p, h2s = _mix(rows_p, rows_s, norms[0], lw["wg"], lw["wro"], lw["wmo"], lw["wo"])
        y2p, y2s = _ffn(h2p, h2s, norms[4], lw["wgu"], lw["wd"], fw, l == depth - 1)
        hp, hs = y2p.reshape(hp.shape), y2s.reshape(hs.shape)
        for lst, val in zip(outs, new_p + new_s):
            lst.append(val)
    return (hp, hs) + tuple(o[0][None] if depth == 1 else jnp.stack(o) for o in outs)
```

```python
import functools
import math

import jax
import jax.numpy as jnp
import numpy as np
from jax import lax
from jax.experimental import pallas as pl
from jax.experimental.pallas import tpu as pltpu

F32 = jnp.float32
BF16 = jnp.bfloat16

CHUNK = 64
RET_HEADS = 4
RET_DK = 128
RET_DV = 256
MLA_HEADS = 8
NOPE_DIM = 128
ROPE_DIM = 64
V_DIM = 128
ROPE_BASE = 10000.0
EPS = 1e-6

LANES = 128
QK_PAD = 2 * LANES
V7X_VMEM_BYTES = 64 * 2**20
VMEM_LIMIT = V7X_VMEM_BYTES * 7 // 8
TOKEN_TILE = 512
WIDE_TOKEN_TILE = 1024
WEIGHT_STREAM_BLOCK = 128
RET_BLOCK = 256
ATTN_Q_TILE = 2048
ATTN_KV_TILE = 512
ATTN_COL_BLOCK = 256
PV_LAG = 1
FF_CHUNK = 256
SAMPLE_STREAMS = 2
NEG_BIG = -1e30
SOFTMAX_C = (NOPE_DIM + ROPE_DIM) ** -0.5 * math.log2(math.e)

_dot = functools.partial(jnp.dot, preferred_element_type=F32)


def _dot_nt(a, b):
    return lax.dot_general(a, b, (((1,), (1,)), ((), ())), preferred_element_type=F32)


def _dot_tn(a, b):
    return lax.dot_general(a, b, (((0,), (0,)), ((), ())), preferred_element_type=F32)


def _rms(x, g):
    ms = jnp.mean(x * x, axis=-1, keepdims=True)
    return x * lax.rsqrt(ms + EPS) * g


def _sigmoid(x):
    return 1.0 / (1.0 + jnp.exp(-x))


def _silu(x):
    return x * _sigmoid(x)


def _params(*sem):
    return pltpu.CompilerParams(dimension_semantics=sem, vmem_limit_bytes=VMEM_LIMIT)


def _const_spec(shape):
    return pl.BlockSpec(shape, lambda *_: (0,) * len(shape), pipeline_mode=pl.Buffered(1))


def _row_spec(tm, width):
    return pl.BlockSpec((tm, width), lambda i: (i, 0))


def _rope_full(x, cos2, sin2):
    return x * cos2 + pltpu.roll(x, LANES // 2, 1) * sin2


def _rope_half(x, c, sa, sb):
    q = ROPE_DIM // 2
    return x * c + pltpu.roll(x, LANES - q, 1) * sa + pltpu.roll(x, q, 1) * sb


def _ret_decay(blk, h):
    lg = math.log(1.0 - 2.0 ** (-5.0 - h))
    ri = lax.broadcasted_iota(jnp.int32, (blk, blk), 0)
    ci = lax.broadcasted_iota(jnp.int32, (blk, blk), 1)
    diff = (ri - ci).astype(F32)
    n = lax.broadcasted_iota(jnp.int32, (blk, RET_DK), 0).astype(F32)
    decay = jnp.where(diff >= 0, jnp.exp(lg * jnp.maximum(diff, 0.0)), 0.0)
    return decay, jnp.exp(lg * (n + 1.0)), jnp.exp(lg * (blk - 1.0 - n)), math.exp(lg * blk)


def _ret_block(q, k, v, state_ref, h, dec, gnw):
    decay, q_dec, k_dec, s_dec = dec
    v = v.astype(BF16)
    scores = _dot_nt(q.astype(BF16), k.astype(BF16)) * decay
    st = state_ref[h]
    y = _dot(scores.astype(BF16), v) + _dot((q * q_dec).astype(BF16), st.astype(BF16))
    state_ref[h] = s_dec * st + _dot_tn((k * k_dec).astype(BF16), v)
    yc = y - jnp.mean(y, axis=-1, keepdims=True)
    return yc * lax.rsqrt(jnp.mean(yc * yc, axis=-1, keepdims=True) + EPS) * gnw


def _inproj_kernel(x_ref, nw_ref, wa_ref, qnw_ref, kvnw_ref, wuq_ref, cr_ref, sr_ref, cm_ref, sm_ref, *rest,
                   prompt, seq_tiles):
    if prompt:
        (wuk_ref, wuvt_ref, s0_ref, gnw_ref, ret_ref, sfin_ref, q_ref, ckv_ref, kpe_ref, kn_ref, kp_ref, vt_ref,
         state_scr, dec_scr, qdec_scr, kdec_scr) = rest
    else:
        rq_ref, rk_ref, rv_ref, q_ref, ckv_ref, kpe_ref = rest
    qk_w = RET_HEADS * RET_DK
    v_w = RET_HEADS * RET_DV
    q_lora = wuq_ref.shape[0]
    kv_lora = ckv_ref.shape[1]
    o_rk, o_rv, o_cq = qk_w, 2 * qk_w, 2 * qk_w + v_w
    o_ckv = o_cq + q_lora
    o_kpe = o_ckv + kv_lora

    if prompt:
        i = pl.program_id(0)
        blk = dec_scr.shape[1]

        @pl.when(i == 0)
        def _():
            for h in range(RET_HEADS):
                dec_scr[h], qdec_scr[h], kdec_scr[h], _ = _ret_decay(blk, h)

        @pl.when(i % seq_tiles == 0)
        def _():
            state_scr[...] = s0_ref[...]

    xn = _rms(x_ref[...], nw_ref[...]).astype(BF16)
    c, s = cr_ref[...], sr_ref[...]
    cr = jnp.concatenate([c, c], axis=1)
    sr = jnp.concatenate([-s, s], axis=1)
    c, s = cm_ref[...], sm_ref[...]
    z = jnp.zeros_like(s)
    cm = jnp.concatenate([c, c, c, c], axis=1)
    sa = jnp.concatenate([-s, z, -s, z], axis=1)
    sb = jnp.concatenate([z, s, z, s], axis=1)

    zq = _dot_nt(xn, wa_ref[0:qk_w, :])
    zk = _dot_nt(xn, wa_ref[o_rk:o_rk + qk_w, :])
    rv = _dot_nt(xn, wa_ref[o_rv:o_rv + v_w, :])
    k_scale = RET_DK ** -0.5
    rq = [_rope_full(zq[:, h * RET_DK:(h + 1) * RET_DK], cr, sr) for h in range(RET_HEADS)]
    rk = [_rope_full(zk[:, h * RET_DK:(h + 1) * RET_DK], cr, sr) * k_scale for h in range(RET_HEADS)]
    if prompt:
        for b in range(x_ref.shape[0] // blk):
            r = slice(b * blk, (b + 1) * blk)
            for h in range(RET_HEADS):
                sl = slice(h * RET_DV, (h + 1) * RET_DV)
                dec = (dec_scr[h], qdec_scr[h], kdec_scr[h], math.exp(math.log(1.0 - 2.0 ** (-5.0 - h)) * blk))
                ret_ref[r, sl] = _ret_block(rq[h][r], rk[h][r], rv[r, sl], state_scr, h, dec,
                                            gnw_ref[:, sl]).astype(BF16)
    else:
        for h in range(RET_HEADS):
            sl = slice(h * RET_DK, (h + 1) * RET_DK)
            rq_ref[:, sl] = rq[h].astype(BF16)
            rk_ref[:, sl] = rk[h].astype(BF16)
        rv_ref[...] = rv.astype(BF16)

    cq = _dot_nt(xn, wa_ref[o_cq:o_cq + q_lora, :])
    cqn = _rms(cq, qnw_ref[...]).astype(BF16)
    for h in range(MLA_HEADS):
        qh = _dot(cqn, wuq_ref[:, h * QK_PAD:(h + 1) * QK_PAD]) * SOFTMAX_C
        q_ref[:, h * QK_PAD:h * QK_PAD + LANES] = qh[:, :LANES].astype(BF16)
        q_ref[:, h * QK_PAD + LANES:(h + 1) * QK_PAD] = _rope_half(qh[:, LANES:], cm, sa, sb).astype(BF16)

    ckv = _rms(_dot_nt(xn, wa_ref[o_ckv:o_ckv + kv_lora, :]), kvnw_ref[...])
    ckv_ref[...] = ckv
    kpe_pad = _rope_half(_dot_nt(xn, wa_ref[o_kpe:o_kpe + LANES, :]), cm, sa, sb)
    if prompt:
        kpe_ref[...] = kpe_pad.T[:ROPE_DIM, :]
    else:
        kpe_ref[...] = kpe_pad[:, :ROPE_DIM]
    if prompt:
        c16 = ckv.astype(BF16)
        kn_ref[...] = _dot(c16, wuk_ref[...]).astype(BF16)
        kp_ref[...] = kpe_pad.astype(BF16)
        v_t = _dot_nt(wuvt_ref[...], c16)
        for h in range(MLA_HEADS):
            vt_ref[h] = v_t[h * V_DIM:(h + 1) * V_DIM, :].astype(BF16)

        @pl.when(i % seq_tiles == seq_tiles - 1)
        def _():
            sfin_ref[...] = state_scr[...]


def _inproj(x2, tabs, nw, wa, qnw, kvnw, wuq, prompt_args, *, tab_tiles):
    n, d = x2.shape
    tm = min(TOKEN_TILE, n)
    qk_w, v_w = RET_HEADS * RET_DK, RET_HEADS * RET_DV
    kv_lora = kvnw.shape[1]
    prompt = prompt_args is not None
    tab_specs = [pl.BlockSpec((tm, t.shape[1]), lambda i: (i % tab_tiles, 0)) for t in tabs]
    in_specs = [_row_spec(tm, d), _const_spec(nw.shape), _const_spec(wa.shape), _const_spec(qnw.shape),
                _const_spec(kvnw.shape), _const_spec(wuq.shape)] + tab_specs
    mla_outs = [(MLA_HEADS * QK_PAD, BF16), (kv_lora, F32), (ROPE_DIM, F32)]
    scratch = []
    if prompt:
        wuk, wuvt, state0, gnw = prompt_args
        assert tm % RET_BLOCK == 0 and tm == ATTN_KV_TILE
        st_spec = pl.BlockSpec((None,) + state0.shape[1:], lambda i: (i // tab_tiles, 0, 0, 0))
        in_specs += [_const_spec(wuk.shape), _const_spec(wuvt.shape), st_spec, _const_spec(gnw.shape)]
        row_outs = [(v_w, BF16)] + mla_outs + [(MLA_HEADS * NOPE_DIM, BF16), (LANES, BF16)]
        out_specs = [_row_spec(tm, w) for w, _ in row_outs]
        out_shape = [jax.ShapeDtypeStruct((n, w), dt) for w, dt in row_outs]
        out_specs.insert(1, st_spec)
        out_shape.insert(1, jax.ShapeDtypeStruct(state0.shape, F32))
        seq = tab_tiles * tm
        out_specs[4] = pl.BlockSpec((None, ROPE_DIM, tm), lambda i: (i // tab_tiles, 0, i % tab_tiles))
        out_shape[4] = jax.ShapeDtypeStruct((n // seq, ROPE_DIM, seq), F32)
        out_specs.append(pl.BlockSpec((None, MLA_HEADS, None, V_DIM, tm),
                                      lambda i: (i // tab_tiles, 0, i % tab_tiles, 0, 0)))
        out_shape.append(jax.ShapeDtypeStruct((n // seq, MLA_HEADS, tab_tiles, V_DIM, tm), BF16))
        scratch = [pltpu.VMEM(state0.shape[1:], F32), pltpu.VMEM((RET_HEADS, RET_BLOCK, RET_BLOCK), F32),
                   pltpu.VMEM((RET_HEADS, RET_BLOCK, RET_DK), F32), pltpu.VMEM((RET_HEADS, RET_BLOCK, RET_DK), F32)]
    else:
        prompt_args = ()
        row_outs = [(qk_w, BF16), (qk_w, BF16), (v_w, BF16)] + mla_outs
        out_specs = [_row_spec(tm, w) for w, _ in row_outs]
        out_shape = [jax.ShapeDtypeStruct((n, w), dt) for w, dt in row_outs]
    return pl.pallas_call(
        functools.partial(_inproj_kernel, prompt=prompt, seq_tiles=tab_tiles),
        grid=(n // tm,),
        in_specs=in_specs,
        out_specs=out_specs,
        out_shape=out_shape,
        scratch_shapes=scratch,
        compiler_params=_params("arbitrary"),
        name="inproj",
    )(x2, nw, wa, qnw, kvnw, wuq, *tabs, *prompt_args)


def _retention_kernel(q_ref, k_ref, v_ref, s0_ref, gnw_ref, y_ref, sfin_ref, state_ref, *, blk, nblk):
    c = pl.program_id(1)

    @pl.when(c == 0)
    def _():
        state_ref[...] = s0_ref[...]

    for h in range(RET_HEADS):
        sl = slice(h * RET_DV, (h + 1) * RET_DV)
        q = q_ref[:, h * RET_DK:(h + 1) * RET_DK].astype(F32)
        k = k_ref[:, h * RET_DK:(h + 1) * RET_DK].astype(F32)
        y_ref[:, sl] = _ret_block(q, k, v_ref[:, sl], state_ref, h, _ret_decay(blk, h), gnw_ref[:, sl]).astype(BF16)

    @pl.when(c == nblk - 1)
    def _():
        sfin_ref[...] = state_ref[...]


def _retention(rq, rk, rv, state0, gnw, blk):
    b, s, _ = rq.shape
    nblk = s // blk
    qk_w, v_w = RET_HEADS * RET_DK, RET_HEADS * RET_DV
    st_spec = pl.BlockSpec((None, RET_HEADS, RET_DK, RET_DV), lambda i, c: (i, 0, 0, 0))
    return pl.pallas_call(
        functools.partial(_retention_kernel, blk=blk, nblk=nblk),
        grid=(b, nblk),
        in_specs=[pl.BlockSpec((None, blk, qk_w), lambda i, c: (i, c, 0)),
                  pl.BlockSpec((None, blk, qk_w), lambda i, c: (i, c, 0)),
                  pl.BlockSpec((None, blk, v_w), lambda i, c: (i, c, 0)),
                  st_spec,
                  pl.BlockSpec((1, v_w), lambda i, c: (0, 0))],
        out_specs=[pl.BlockSpec((None, blk, v_w), lambda i, c: (i, c, 0)), st_spec],
        out_shape=[jax.ShapeDtypeStruct((b, s, v_w), BF16),
                   jax.ShapeDtypeStruct((b, RET_HEADS, RET_DK, RET_DV), F32)],
        scratch_shapes=[pltpu.VMEM((RET_HEADS, RET_DK, RET_DV), F32)],
        compiler_params=_params("parallel", "arbitrary"),
        name="retention",
    )(rq, rk, rv, state0, gnw)


def _attn_prompt_kernel(q_ref, kn_ref, kp_ref, vt_ref, o_ref, s_scr, mx_scr, m_scr, acc_scr, *, tq, tk):
    nq = q_ref.shape[0] // tq
    assert tq == 4 * tk
    cb = ATTN_COL_BLOCK
    ones = jnp.ones((acc_scr.shape[1] - V_DIM, tk), BF16)

    def rows(ref, t, n):
        return ref[pl.ds(pl.multiple_of(t * n, n), n), :]

    def col_max(s):
        return jnp.broadcast_to(jnp.max(s, axis=0, keepdims=True), (mx_scr.shape[2], s.shape[1]))

    def park_block(b, qi, t, c):
        k = jnp.concatenate([rows(kn_ref, t, tk), rows(kp_ref, t, tk)], axis=1)
        s = _dot_nt(k, q_ref[pl.ds(pl.multiple_of(qi * tq + c * cb, cb), cb), :])
        s_scr[b, c] = s
        mx_scr[b, c] = col_max(s)

    def exp_block(held, b, t, c, col0, masked):
        s = s_scr[b, c]
        mx = mx_scr[b, c]
        if masked and c * cb - col0 < tk:
            key = lax.broadcasted_iota(jnp.int32, s.shape, 0) // CHUNK
            qry = (lax.broadcasted_iota(jnp.int32, s.shape, 1) + (c * cb - col0)) // CHUNK
            s = jnp.where(key <= qry, s, NEG_BIG)
            mx = col_max(s)
        m_old = m_scr[c]
        m_new = jnp.maximum(m_old, mx)
        m_scr[c] = m_new
        held[b, c] = (jnp.exp2(m_old - m_new), jnp.exp2(s - m_new[:1, :]).astype(BF16))

    def pv_block(held, b, t, c):
        alpha, p = held.pop((b, c))
        v_ext = jnp.concatenate([vt_ref[t], ones], axis=0)
        acc_scr[c] = alpha[:1, :] * acc_scr[c] + _dot(v_ext, p)

    def step(consumes, parks):
        held = {}
        blocks = [(b, t, c, col0, masked) for b, t, col0, masked in consumes for c in range(col0 // cb, tq // cb)]
        cons = []
        for n, (b, t, c, col0, masked) in enumerate(blocks):
            cons.append(functools.partial(exp_block, held, b, t, c, col0, masked))
            if n >= PV_LAG:
                cons.append(functools.partial(pv_block, held, *blocks[n - PV_LAG][:3]))
        for blk_ in blocks[max(len(blocks) - PV_LAG, 0):]:
            cons.append(functools.partial(pv_block, held, *blk_[:3]))
        prks = [functools.partial(park_block, b, qi, t, c)
                for b, qi, t, col0 in parks for c in range(col0 // cb, tq // cb)]
        i = j = 0
        while i < len(cons) or j < len(prks):
            if j < len(prks) and (i == len(cons) or j * len(cons) <= i * len(prks)):
                prks[j]()
                j += 1
            else:
                cons[i]()
                i += 1

    def restart():
        m_scr[...] = jnp.full_like(m_scr, NEG_BIG)
        acc_scr[...] = jnp.zeros_like(acc_scr)

    def diagonal(qi, second_parked):
        t = 4 * qi
        nxt = jnp.minimum(qi + 1, nq - 1)
        late = [(2, qi, t + 2, 2 * tk), (3, qi, t + 3, 3 * tk)]
        step([(0, t, 0, True)], late if second_parked else [(1, qi, t + 1, tk)] + late)
        step([(1, t + 1, tk, True)], [(0, nxt, 0, 0)])
        step([(2, t + 2, 2 * tk, True), (3, t + 3, 3 * tk, True)], [(1, nxt, 1, 0)])
        for c in range(tq // cb):
            acc = acc_scr[c]
            out_t = acc[:V_DIM, :] / acc[V_DIM:V_DIM + 1, :]
            o_ref[pl.ds(pl.multiple_of(qi * tq + c * cb, cb), cb), :] = out_t.T.astype(BF16)
        restart()

    def q_tile(qi, _):
        def trip(j, _):
            t = 4 * j
            step([(0, t, 0, False)], [(2, qi, t + 2, 0)])
            step([(1, t + 1, 0, False)], [(3, qi, t + 3, 0)])
            step([(2, t + 2, 0, False)], [(0, qi, t + 4, 0)])
            step([(3, t + 3, 0, False)], [(1, qi, t + 5, 0)])
            return 0

        lax.fori_loop(0, qi, trip, 0)
        diagonal(qi, True)
        return 0

    restart()
    step([], [(0, 0, 0, 0)])
    diagonal(0, False)
    lax.fori_loop(1, nq, q_tile, 0)


def _attn_prompt(q, kn, kp, vt):
    b, s, _ = q.shape
    tq, tk = ATTN_Q_TILE, ATTN_KV_TILE
    assert tq % (2 * tk) == 0 and s % tq == 0 and vt.shape[2:] == (s // tk, V_DIM, tk)
    cb = ATTN_COL_BLOCK
    nblk = tq // cb
    stat_rows = 8
    acc_rows = V_DIM + 16
    return pl.pallas_call(
        functools.partial(_attn_prompt_kernel, tq=tq, tk=tk),
        grid=(b, MLA_HEADS),
        in_specs=[pl.BlockSpec((None, s, QK_PAD), lambda i, h: (i, 0, h)),
                  pl.BlockSpec((None, s, NOPE_DIM), lambda i, h: (i, 0, h)),
                  pl.BlockSpec((None, s, LANES), lambda i, h: (i, 0, 0)),
                  pl.BlockSpec((None, None) + vt.shape[2:], lambda i, h: (i, h, 0, 0, 0))],
        out_specs=pl.BlockSpec((None, s, V_DIM), lambda i, h: (i, 0, h)),
        out_shape=jax.ShapeDtypeStruct((b, s, MLA_HEADS * V_DIM), BF16),
        scratch_shapes=[pltpu.VMEM((4, nblk, tk, cb), F32), pltpu.VMEM((4, nblk, stat_rows, cb), F32),
                        pltpu.VMEM((nblk, stat_rows, cb), F32), pltpu.VMEM((nblk, acc_rows, cb), F32)],
        compiler_params=_params("parallel", "parallel"),
        name="attn_prompt",
    )(q, kn, kp, vt)


def _attn_sample_kernel(q_ref, cckv_ref, ckpe_t_ref, nckv_ref, nkpe_ref, wuk_ref, wuv_ref, o_ref):
    chains = [_attn_sample_stream(q_ref.at[j], cckv_ref.at[j], ckpe_t_ref.at[j], nckv_ref.at[j], nkpe_ref.at[j],
                                  wuk_ref, wuv_ref, o_ref.at[j]) for j in range(q_ref.shape[0])]
    stages = 3
    for step in range(stages + len(chains) - 1):
        for j, chain in enumerate(chains):
            if 0 <= step - j < stages:
                next(chain)


def _attn_sample_stream(q_ref, cckv_ref, ckpe_t_ref, nckv_ref, nkpe_ref, wuk_ref, wuv_ref, o_ref):
    n = q_ref.shape[0]
    q_lat, q_rope = [], []
    for h in range(MLA_HEADS):
        q_nope = q_ref[:, h * QK_PAD:h * QK_PAD + NOPE_DIM]
        q_lat.append(_dot_nt(q_nope, wuk_ref[:, h * NOPE_DIM:(h + 1) * NOPE_DIM]).astype(BF16))
        q_rope.append(q_ref[:, h * QK_PAD + NOPE_DIM:h * QK_PAD + NOPE_DIM + ROPE_DIM])
    q_lat = jnp.concatenate(q_lat, axis=0)
    q_rope = jnp.concatenate(q_rope, axis=0)
    vc = cckv_ref[...].astype(BF16)
    vn = nckv_ref[...].astype(BF16)
    sc = _dot_nt(q_lat, vc) + _dot(q_rope, ckpe_t_ref[...].astype(BF16))
    sn = _dot_nt(q_lat, vn) + _dot_nt(q_rope, nkpe_ref[...].astype(BF16))
    yield
    m = jnp.maximum(jnp.max(sc, axis=-1, keepdims=True), jnp.max(sn, axis=-1, keepdims=True))
    pc = jnp.exp2(sc - m)
    pn = jnp.exp2(sn - m)
    l = jnp.sum(pc, axis=-1, keepdims=True) + jnp.sum(pn, axis=-1, keepdims=True)
    yield
    o_lat = ((_dot(pc.astype(BF16), vc) + _dot(pn.astype(BF16), vn)) / l).astype(BF16)
    for h in range(MLA_HEADS):
        sl = slice(h * V_DIM, (h + 1) * V_DIM)
        o_ref[:, sl] = _dot(o_lat[h * n:(h + 1) * n], wuv_ref[:, sl]).astype(BF16)
    yield


def _attn_sample(q, cache_ckv, cache_kpe_t, new_ckv, new_kpe, wuk, wuv):
    b, s, _ = q.shape
    g = SAMPLE_STREAMS
    assert b % g == 0

    def per_stream(a):
        return pl.BlockSpec((g,) + a.shape[1:], lambda i: (i, 0, 0))

    return pl.pallas_call(
        _attn_sample_kernel,
        grid=(b // g,),
        in_specs=[per_stream(q), per_stream(cache_ckv), per_stream(cache_kpe_t), per_stream(new_ckv),
                  per_stream(new_kpe), _const_spec(wuk.shape), _const_spec(wuv.shape)],
        out_specs=pl.BlockSpec((g, s, MLA_HEADS * V_DIM), lambda i: (i, 0, 0)),
        out_shape=jax.ShapeDtypeStruct((b, s, MLA_HEADS * V_DIM), BF16),
        compiler_params=_params("parallel"),
        name="attn_sample",
    )(q, cache_ckv, cache_kpe_t, new_ckv, new_kpe, wuk, wuv)


def _mix_kernel(x_ref, ret_ref, mla_ref, nw_ref, wg_ref, wro_ref, wmo_ref, wo_ref, h_ref):
    d = x_ref.shape[1]
    v_w = ret_ref.shape[1]
    x = x_ref[...]
    xn = _rms(x, nw_ref[...]).astype(BF16)
    rg = _dot_nt(xn, wg_ref[0:v_w, :])
    ret_b = _dot((ret_ref[...].astype(F32) * _silu(rg)).astype(BF16), wro_ref[...])
    ga = _dot_nt(xn, wg_ref[v_w:v_w + d, :])
    merged = _sigmoid(ga) * ret_b
    mla_b = _dot(mla_ref[...], wmo_ref[...])
    gb = _dot_nt(xn, wg_ref[v_w + d:v_w + 2 * d, :])
    merged = merged + _sigmoid(gb) * mla_b
    h_ref[...] = x + _dot(merged.astype(BF16), wo_ref[...])


def _mix(x2, ret2, mla2, nw, wg, wro, wmo, wo):
    n, d = x2.shape
    tm = min(WIDE_TOKEN_TILE, n)
    return pl.pallas_call(
        _mix_kernel,
        grid=(n // tm,),
        in_specs=[_row_spec(tm, d), _row_spec(tm, ret2.shape[1]), _row_spec(tm, mla2.shape[1]),
                  _const_spec(nw.shape), _const_spec(wg.shape), _const_spec(wro.shape),
                  _const_spec(wmo.shape), _const_spec(wo.shape)],
        out_specs=_row_spec(tm, d),
        out_shape=jax.ShapeDtypeStruct((n, d), F32),
        compiler_params=_params("parallel"),
        name="mix",
    )(x2, ret2, mla2, nw, wg, wro, wmo, wo)


def _ffn_kernel(h_ref, nw_ref, wgu_ref, wd_ref, fw_ref, y_ref, *, final_norm):
    h = h_ref[...]
    hn = _rms(h, nw_ref[...]).astype(BF16)
    acc = h
    d_ff = wd_ref.shape[0]
    for c in range(d_ff // FF_CHUNK):
        sl = slice(c * FF_CHUNK, (c + 1) * FF_CHUNK)
        g = _dot(hn, wgu_ref[:, sl])
        u = _dot(hn, wgu_ref[:, d_ff + c * FF_CHUNK:d_ff + (c + 1) * FF_CHUNK])
        acc = acc + _dot((_silu(g) * u).astype(BF16), wd_ref[sl, :])
    y_ref[...] = _rms(acc, fw_ref[...]) if final_norm else acc


def _ffn(h2, nw, wgu, wd, fw, final_norm):
    n, d = h2.shape
    tm = min(WIDE_TOKEN_TILE, n)
    return pl.pallas_call(
        functools.partial(_ffn_kernel, final_norm=final_norm),
        grid=(n // tm,),
        in_specs=[_row_spec(tm, d), _const_spec(nw.shape), _const_spec(wgu.shape), _const_spec(wd.shape),
                  _const_spec(fw.shape)],
        out_specs=_row_spec(tm, d),
        out_shape=jax.ShapeDtypeStruct((n, d), F32),
        compiler_params=_params("parallel"),
        name="ffn",
    )(h2, nw, wgu, wd, fw)


def _rope_tables(start, n, reps):
    pos = np.arange(start, start + n, dtype=np.float64)[:, None]

    def cs(d):
        inv = ROPE_BASE ** (-np.arange(0, d, 2, dtype=np.float64) / d)
        ang = pos * inv[None, :]
        return [np.tile(f(ang).astype(np.float32), (reps, 1)) for f in (np.cos, np.sin)]

    return tuple(jnp.asarray(t) for t in cs(RET_DK) + cs(ROPE_DIM))


def _regroup_kernel(w_ref, *out_refs, plans, axis):
    for o_ref, plan in zip(out_refs, plans):
        off = 0
        for start, width in plan:
            dst = (slice(off, off + width), slice(None))[::1 if axis == 0 else -1]
            if start is None:
                o_ref[dst] = jnp.zeros(o_ref[dst].shape, o_ref.dtype)
            else:
                src = (slice(start, start + width), slice(None))[::1 if axis == 0 else -1]
                o_ref[dst] = w_ref[src].astype(o_ref.dtype)
            off += width


def _regroup_bf16(w, plans, axis):
    other = w.shape[1 - axis]
    blk = min(other, WEIGHT_STREAM_BLOCK)
    widths = [sum(width for _, width in plan) for plan in plans]

    def spec(width):
        return (pl.BlockSpec((width, blk), lambda i: (0, i)) if axis == 0 else
                pl.BlockSpec((blk, width), lambda i: (i, 0)))

    def shape(width):
        return (width, other) if axis == 0 else (other, width)

    return pl.pallas_call(
        functools.partial(_regroup_kernel, plans=plans, axis=axis),
        grid=(other // blk,),
        in_specs=[spec(w.shape[axis])],
        out_specs=[spec(wd) for wd in widths],
        out_shape=[jax.ShapeDtypeStruct(shape(wd), BF16) for wd in widths],
        compiler_params=_params("parallel"),
        name="regroup",
    )(w)


def _prep_layer_weights(w_in, w_uq, w_ukv, w_ret_o, w_mla_o, w_out, w_gate_up, w_down):
    qk_w, v_w = RET_HEADS * RET_DK, RET_HEADS * RET_DV
    q_lora = w_uq.shape[0]
    kv_lora = w_ukv.shape[0]
    o_rg = 2 * qk_w + v_w
    o_cq = o_rg + v_w
    o_kpe = o_cq + q_lora + kv_lora
    o_ga = o_kpe + ROPE_DIM
    wa, wg = _regroup_bf16(jnp.swapaxes(w_in, 0, 1),
                           [[(0, o_rg), (o_cq, o_ga - o_cq), (None, LANES - ROPE_DIM)],
                            [(o_rg, v_w), (o_ga, w_in.shape[1] - o_ga)]], axis=0)
    head_w = NOPE_DIM + ROPE_DIM
    wuq, = _regroup_bf16(w_uq, [[piece for h in range(MLA_HEADS)
                                 for piece in ((h * head_w, head_w), (None, QK_PAD - head_w))]], axis=1)
    kv_w = NOPE_DIM + V_DIM
    wuk, wuv = _regroup_bf16(w_ukv, [[(h * kv_w, NOPE_DIM) for h in range(MLA_HEADS)],
                                     [(h * kv_w + NOPE_DIM, V_DIM) for h in range(MLA_HEADS)]], axis=1)
    assert w_down.shape[0] % FF_CHUNK == 0
    return dict(wa=wa, wg=wg, wuq=wuq, wuk=wuk, wuv=wuv, wro=w_ret_o.astype(BF16), wmo=w_mla_o.astype(BF16),
                wo=w_out.astype(BF16), wgu=w_gate_up.astype(BF16), wd=w_down.astype(BF16))


def _layer(x, tabs, tab_tiles, lw, norms, state0, cache, fw, final_norm):
    b, s, d = x.shape
    n = b * s
    x2 = x.reshape(n, d)
    nmw, qnw, kvnw, gnw, nfw = norms
    wukv = (lw["wuk"], lw["wuv"])
    if cache is None:
        ret, s_fin, q, ckv, kpe, kn, kp, vt = _inproj(x2, tabs, nmw, lw["wa"], qnw, kvnw, lw["wuq"],
                                                      (lw["wuk"], lw["wuv"].T, state0, gnw), tab_tiles=tab_tiles)
    else:
        rq, rk, rv, q, ckv, kpe = _inproj(x2, tabs, nmw, lw["wa"], qnw, kvnw, lw["wuq"], None, tab_tiles=tab_tiles)
        ret, s_fin = _retention(rq.reshape(b, s, -1), rk.reshape(b, s, -1), rv.reshape(b, s, -1), state0, gnw, s)
    ckv3 = ckv.reshape(b, s, -1)
    kpe3 = jnp.swapaxes(kpe, 1, 2) if cache is None else kpe.reshape(b, s, ROPE_DIM)
    q3 = q.reshape(b, s, MLA_HEADS * QK_PAD)
    if cache is None:
        mla = _attn_prompt(q3, kn.reshape(b, s, -1), kp.reshape(b, s, -1), vt)
    else:
        mla = _attn_sample(q3, cache[0].astype(F32), jnp.swapaxes(cache[1].astype(F32), 1, 2), ckv3, kpe3, *wukv)
    h2 = _mix(x2, ret.reshape(n, -1), mla.reshape(n, -1), nmw, lw["wg"], lw["wro"], lw["wmo"], lw["wo"])
    y2 = _ffn(h2, nfw, lw["wgu"], lw["wd"], fw, final_norm)
    return y2.reshape(b, s, d), (ckv3, kpe3, s_fin)


def kernel(x_prompt, x_sample, cache_ckv, cache_kpe, state_ret, norm_mix_w, w_in, q_norm_w, w_uq, kv_norm_w, w_ukv, ret_gn_w, w_ret_o, w_mla_o, w_out, norm_ffn_w, w_gate_up, w_down, norm_final_w):
    depth = w_in.shape[0]
    bp, sp, _ = x_prompt.shape
    bs, ss, _ = x_sample.shape
    past = cache_ckv.shape[2]
    assert sp % TOKEN_TILE == 0 and sp % ATTN_Q_TILE == 0 and (bs * ss) % min(TOKEN_TILE, bs * ss) == 0
    assert ss == CHUNK, "the sample group is one streaming chunk"

    tabs_p = _rope_tables(0, sp, 1)
    tabs_s = _rope_tables(past, ss, bs)
    fw = norm_final_w.reshape(1, -1)
    state0_p = jnp.zeros((bp, RET_HEADS, RET_DK, RET_DV), F32)

    hp, hs = x_prompt, x_sample
    outs = [[] for _ in range(6)]
    for l in range(depth):
        lw = _prep_layer_weights(w_in[l], w_uq[l], w_ukv[l], w_ret_o[l], w_mla_o[l], w_out[l], w_gate_up[l],
                                 w_down[l])
        norms = (norm_mix_w[l].reshape(1, -1), q_norm_w[l].reshape(1, -1), kv_norm_w[l].reshape(1, -1),
                 ret_gn_w[l].reshape(1, -1), norm_ffn_w[l].reshape(1, -1))
        final = l == depth - 1
        hp, (a, b_, c) = _layer(hp, tabs_p, sp // TOKEN_TILE, lw, norms, state0_p, None, fw, final)
        hs, (d_, e, f) = _layer(hs, tabs_s, 1, lw, norms, state_ret[l].astype(F32),
                                (cache_ckv[l], cache_kpe[l]), fw, final)
        for lst, val in zip(outs, (a, b_, c, d_, e, f)):
            lst.append(val)
    return (hp, hs) + tuple(o[0][None] if depth == 1 else jnp.stack(o) for o in outs)
```
